```python
import jax
import jax.numpy as jnp
from jax import lax
import numpy as np

D_MODEL = 1024
BATCH = 8
SEQ = 2048
DEPTH = 1
DEC_BATCH = 128
DEC_SEQ = 1
PAST_LEN = 2048
PAGE_SIZE = 128

HEAD_DIM = 64
NSA_HEADS = 8
NSA_KV_HEADS = 2
NSA_GROUP = NSA_HEADS // NSA_KV_HEADS
CMP_BLOCK = 32
CMP_STRIDE = 16
CMP_HIDDEN = 256
SEL_BLOCK = 64
N_SEL = 8
WINDOW = 512
NSA_Q_BLOCK = 64
MOBA_HEADS = 8
MOBA_BLOCK = 256
MOBA_TOPK = 3
MOBA_Q_BLOCK = 32
N_EXPERTS = 32
TOP_K = 4
D_FF = D_MODEL
SWIGLU_LIMIT = 7.0
SWIGLU_ALPHA = 1.702
MOE_BLOCK = 128
ROPE_THETA = 10000.0
EPS = 1e-6
NEG = -1e30
POS_BIG = 1e30

kernel_name = 'nsa_moba_gated_hybrid_moe_step'


def in_proj_widths():
    qa = NSA_HEADS * HEAD_DIM
    kva = NSA_KV_HEADS * HEAD_DIM
    qb = MOBA_HEADS * HEAD_DIM
    return [qa, kva, kva, kva, kva, kva, kva, 3 * NSA_HEADS, qb, qb, qb, 2 * D_MODEL]


def rms_norm(x, g):
    xf = x.astype(jnp.float32)
    y = xf * lax.rsqrt(jnp.mean(xf * xf, axis=-1, keepdims=True) + EPS)
    return (y * g.astype(jnp.float32)).astype(x.dtype)


def rope(x, pos):
    half = HEAD_DIM // 2
    inv = ROPE_THETA ** (-jnp.arange(half, dtype=jnp.float32) / half)
    ang = pos.astype(jnp.float32)[:, None] * inv[None, :]
    cos = jnp.cos(ang)[None, :, None, :]
    sin = jnp.sin(ang)[None, :, None, :]
    xf = x.astype(jnp.float32)
    x1, x2 = xf[..., :half], xf[..., half:]
    return jnp.concatenate([x1 * cos - x2 * sin, x2 * cos + x1 * sin], axis=-1).astype(x.dtype)


def masked_softmax(s, mask):
    s = jnp.where(mask, s, NEG)
    m = jnp.max(s, axis=-1, keepdims=True)
    e = jnp.where(mask, jnp.exp(s - m), 0.0)
    return e / jnp.maximum(jnp.sum(e, axis=-1, keepdims=True), 1e-30)


def mixer_inputs(x, pos, norm_attn, w_in, g_qa, g_ka, g_qb, g_kb):
    B, T, _ = x.shape
    cuts = [int(c) for c in np.cumsum(in_proj_widths())[:-1]]
    parts = jnp.split(rms_norm(x, norm_attn) @ w_in, cuts, axis=-1)
    q_a, k_c, v_c, k_s, v_s, k_w, v_w, g_nsa, q_b, k_b, v_b, g_mrg = parts
    heads = lambda t, h: t.reshape(B, T, h, HEAD_DIM)
    qk = lambda t, h, g: rope(rms_norm(heads(t, h), g), pos)
    return {
        'q_a': qk(q_a, NSA_HEADS, g_qa),
        'k_c': qk(k_c, NSA_KV_HEADS, g_ka[0]), 'v_c': heads(v_c, NSA_KV_HEADS),
        'k_s': qk(k_s, NSA_KV_HEADS, g_ka[1]), 'v_s': heads(v_s, NSA_KV_HEADS),
        'k_w': qk(k_w, NSA_KV_HEADS, g_ka[2]), 'v_w': heads(v_w, NSA_KV_HEADS),
        'g_nsa': jax.nn.sigmoid(g_nsa).reshape(B, T, NSA_HEADS, 3),
        'q_b': qk(q_b, MOBA_HEADS, g_qb), 'k_b': qk(k_b, MOBA_HEADS, g_kb), 'v_b': heads(v_b, MOBA_HEADS),
        'g_mrg': jax.nn.sigmoid(g_mrg),
    }


def compress(k, pos_emb, w1, w2):
    B, T, G, _ = k.shape
    n_c = (T - CMP_BLOCK) // CMP_STRIDE + 1
    idx = jnp.arange(n_c)[:, None] * CMP_STRIDE + jnp.arange(CMP_BLOCK)[None, :]
    blk = k[:, idx] + pos_emb[None, None, :, None, :].astype(k.dtype)
    flat = blk.transpose(0, 1, 3, 2, 4).reshape(B, n_c, G, CMP_BLOCK * HEAD_DIM)
    return jax.nn.gelu(flat @ w1) @ w2


def to_blocks(k, size):
    B, T, H, _ = k.shape
    n = -(-T // size)
    k = jnp.pad(k, ((0, 0), (0, n * size - T), (0, 0), (0, 0)))
    return k.reshape(B, n, size, H, HEAD_DIM).transpose(0, 3, 1, 2, 4)


def nsa_attend(q, gates, q_pos, kc, vc, k_blk, v_blk, k_win, v_win, win_pos):
    B, Tq = q.shape[:2]
    scale = HEAD_DIM ** -0.5
    qg = q.reshape(B, Tq, NSA_KV_HEADS, NSA_GROUP, HEAD_DIM)
    t = q_pos[:, None]
    n_c = kc.shape[1]
    cmp_end = jnp.arange(n_c) * CMP_STRIDE + CMP_BLOCK - 1
    s_c = jnp.einsum('bqgrd,bcgd->bqgrc', qg, kc).astype(jnp.float32) * scale
    p_c = masked_softmax(s_c, (cmp_end[None, :] <= t)[None, :, None, None, :])
    o_c = jnp.einsum('bqgrc,bcgd->bqgrd', p_c.astype(vc.dtype), vc)
    n_s = k_blk.shape[2]
    ci = jnp.arange(n_c)[:, None] * CMP_STRIDE
    sj = jnp.arange(n_s)[None, :] * SEL_BLOCK
    overlap = ((ci < sj + SEL_BLOCK) & (ci + CMP_BLOCK > sj)).astype(jnp.float32)
    imp = jnp.einsum('bqgrc,cs->bqgs', p_c, overlap)
    blk = jnp.arange(n_s)[None, :]
    cur = t // SEL_BLOCK
    forced = (blk == 0) | (blk == cur) | (blk == cur - 1)
    score = jnp.where(forced[None, :, None, :], POS_BIG,
                      jnp.where((blk <= cur)[None, :, None, :], imp, NEG))
    _, top = lax.top_k(score, min(N_SEL, n_s))
    n = top.shape[-1]
    ok = top <= cur[None, :, :, None]
    bi = jnp.arange(B)[:, None, None, None]
    gi = jnp.arange(NSA_KV_HEADS)[None, None, :, None]
    ks = k_blk[bi, gi, top]
    vs = v_blk[bi, gi, top]
    kpos = top[..., None] * SEL_BLOCK + jnp.arange(SEL_BLOCK)
    m_s = ok[..., None] & (kpos <= q_pos[None, :, None, None, None])
    s_s = jnp.einsum('bqgrd,bqgnsd->bqgrns', qg, ks).astype(jnp.float32) * scale
    p_s = masked_softmax(s_s.reshape(B, Tq, NSA_KV_HEADS, NSA_GROUP, n * SEL_BLOCK),
                         m_s.reshape(B, Tq, NSA_KV_HEADS, 1, n * SEL_BLOCK))
    o_s = jnp.einsum('bqgrk,bqgkd->bqgrd', p_s.astype(vs.dtype),
                     vs.reshape(B, Tq, NSA_KV_HEADS, n * SEL_BLOCK, HEAD_DIM))
    wp = win_pos[None, :]
    m_w = (wp >= 0) & (wp <= t) & (wp > t - WINDOW)
    s_w = jnp.einsum('bqgrd,bkgd->bqgrk', qg, k_win).astype(jnp.float32) * scale
    p_w = masked_softmax(s_w, m_w[None, :, None, None, :])
    o_w = jnp.einsum('bqgrk,bkgd->bqgrd', p_w.astype(v_win.dtype), v_win)
    g = gates.reshape(B, Tq, NSA_KV_HEADS, NSA_GROUP, 3)
    o = g[..., 0:1] * o_c + g[..., 1:2] * o_s + g[..., 2:3] * o_w
    return o.reshape(B, Tq, NSA_HEADS * HEAD_DIM)


def nsa_prompt(q, gates, kc, vc, k_blk, v_blk, k_win, v_win):
    B, T = q.shape[:2]
    pad = ((0, 0), (WINDOW, 0), (0, 0), (0, 0))
    kw = jnp.pad(k_win, pad)
    vw = jnp.pad(v_win, pad)
    span = WINDOW + NSA_Q_BLOCK

    def body(b):
        s = b * NSA_Q_BLOCK
        return nsa_attend(lax.dynamic_slice_in_dim(q, s, NSA_Q_BLOCK, 1),
                          lax.dynamic_slice_in_dim(gates, s, NSA_Q_BLOCK, 1),
                          s + jnp.arange(NSA_Q_BLOCK, dtype=jnp.int32), kc, vc, k_blk, v_blk,
                          lax.dynamic_slice_in_dim(kw, s, span, 1),
                          lax.dynamic_slice_in_dim(vw, s, span, 1),
                          s - WINDOW + jnp.arange(span, dtype=jnp.int32))

    o = lax.map(body, jnp.arange(T // NSA_Q_BLOCK, dtype=jnp.int32))
    return o.transpose(1, 0, 2, 3).reshape(B, T, NSA_HEADS * HEAD_DIM)


def moba_attend(q, q_pos, k_blk, v_blk, k_mean):
    B, Tq = q.shape[:2]
    n_b = k_blk.shape[2]
    scale = HEAD_DIM ** -0.5
    cur = q_pos[:, None] // MOBA_BLOCK
    past = jnp.arange(n_b)[None, :] < cur
    s_g = jnp.einsum('bqhd,bhnd->bqhn', q.astype(jnp.float32), k_mean)
    s_g = jnp.where(past[None, :, None, :], s_g, NEG)
    _, top = lax.top_k(s_g, min(MOBA_TOPK, n_b))
    own = jnp.broadcast_to(cur[None, :, :, None], (B, Tq, MOBA_HEADS, 1)).astype(top.dtype)
    sel = jnp.concatenate([top, own], axis=-1)
    ok = jnp.concatenate([top < cur[None, :, :, None], jnp.ones_like(own, dtype=bool)], axis=-1)
    n = sel.shape[-1]
    bi = jnp.arange(B)[:, None, None, None]
    hi = jnp.arange(MOBA_HEADS)[None, None, :, None]
    ks = k_blk[bi, hi, sel]
    vs = v_blk[bi, hi, sel]
    kpos = sel[..., None] * MOBA_BLOCK + jnp.arange(MOBA_BLOCK)
    mask = ok[..., None] & (kpos <= q_pos[None, :, None, None, None])
    s = jnp.einsum('bqhd,bqhnsd->bqhns', q, ks).astype(jnp.float32) * scale
    p = masked_softmax(s.reshape(B, Tq, MOBA_HEADS, n * MOBA_BLOCK),
                       mask.reshape(B, Tq, MOBA_HEADS, n * MOBA_BLOCK))
    o = jnp.einsum('bqhk,bqhkd->bqhd', p.astype(vs.dtype),
                   vs.reshape(B, Tq, MOBA_HEADS, n * MOBA_BLOCK, HEAD_DIM))
    return o.reshape(B, Tq, MOBA_HEADS * HEAD_DIM)


def block_means(k_blk):
    return jnp.mean(k_blk.astype(jnp.float32), axis=3)


def moba_prompt(q, k, v):
    B, T = q.shape[:2]
    k_blk = to_blocks(k, MOBA_BLOCK)
    v_blk = to_blocks(v, MOBA_BLOCK)
    k_mean = block_means(k_blk)

    def body(b):
        s = b * MOBA_Q_BLOCK
        return moba_attend(lax.dynamic_slice_in_dim(q, s, MOBA_Q_BLOCK, 1),
                           s + jnp.arange(MOBA_Q_BLOCK, dtype=jnp.int32), k_blk, v_blk, k_mean)

    o = lax.map(body, jnp.arange(T // MOBA_Q_BLOCK, dtype=jnp.int32))
    return o.transpose(1, 0, 2, 3).reshape(B, T, MOBA_HEADS * HEAD_DIM)


def merge_branches(x, o_a, o_b, g_mrg, w_up_a, w_up_b, w_out):
    g_a, g_b = jnp.split(g_mrg, 2, axis=-1)
    return x + (g_a * (o_a @ w_up_a) + g_b * (o_b @ w_up_b)) @ w_out


def moe_ffn(x, norm_ffn, w_router, b_router, w_gu, b_gu, w_down, b_down):
    n_tok = x.shape[0]
    xn = rms_norm(x, norm_ffn)
    logits = (xn @ w_router).astype(jnp.float32) + b_router.astype(jnp.float32)
    top_v, top_e = lax.top_k(logits, TOP_K)
    gates = jax.nn.softmax(top_v, axis=-1)
    n_as = n_tok * TOP_K
    flat_e = top_e.reshape(-1)
    flat_t = jnp.arange(n_as, dtype=jnp.int32) // TOP_K
    order = jnp.argsort(flat_e)
    se = flat_e[order]
    counts = jnp.bincount(flat_e, length=N_EXPERTS)
    padded = (counts + MOE_BLOCK - 1) // MOE_BLOCK * MOE_BLOCK
    pend = jnp.cumsum(padded)
    slot = (pend - padded)[se] + jnp.arange(n_as) - (jnp.cumsum(counts) - counts)[se]
    n_blocks = -(-n_as // MOE_BLOCK) + N_EXPERTS
    cap = n_blocks * MOE_BLOCK
    slot_tok = jnp.zeros((cap,), jnp.int32).at[slot].set(flat_t[order])
    slot_gate = jnp.zeros((cap,), x.dtype).at[slot].set(gates.reshape(-1)[order].astype(x.dtype))
    block_e = jnp.minimum(jnp.searchsorted(pend, jnp.arange(n_blocks) * MOE_BLOCK, side='right'),
                          N_EXPERTS - 1)

    def expert_block(args):
        tok, e = args
        h = xn[tok] @ w_gu[e] + b_gu[e]
        gate = jnp.minimum(h[:, :D_FF], SWIGLU_LIMIT)
        up = jnp.clip(h[:, D_FF:], -SWIGLU_LIMIT, SWIGLU_LIMIT)
        return ((up + 1.0) * gate * jax.nn.sigmoid(SWIGLU_ALPHA * gate)) @ w_down[e] + b_down[e]

    y = lax.map(expert_block, (slot_tok.reshape(n_blocks, MOE_BLOCK), block_e))
    out = jnp.zeros_like(x).at[slot_tok].add(y.reshape(cap, -1) * slot_gate[:, None])
    return x + out


def setup_inputs(seed: int = 0) -> dict:
    key = jax.random.key(seed)
    ks = jax.random.split(key, 40)
    f32 = jnp.float32

    def rnd(i, shape, scale=1.0):
        return scale * jax.random.normal(ks[i], shape, f32)

    def gain(i, shape):
        return 1.0 + rnd(i, shape, 0.01)

    n_pages = PAST_LEN // PAGE_SIZE
    used = DEC_BATCH * n_pages
    n_pool = used + max(1, used // 4)
    win_buf = min(WINDOW, PAST_LEN)
    kva = (n_pool, PAGE_SIZE, NSA_KV_HEADS, HEAD_DIM)
    kvb = (n_pool, PAGE_SIZE, MOBA_HEADS, HEAD_DIM)
    winb = (DEC_BATCH, win_buf, NSA_KV_HEADS, HEAD_DIM)
    page_table = jax.random.permutation(ks[39], n_pool)[:used].reshape(DEC_BATCH, n_pages).astype(jnp.int32)
    w_in_cols = int(sum(in_proj_widths()))
    lk = CMP_BLOCK * HEAD_DIM
    wa = NSA_HEADS * HEAD_DIM
    wb = MOBA_HEADS * HEAD_DIM
    return {
        'x_prompt': rnd(0, (BATCH, SEQ, D_MODEL)),
        'x_sample': rnd(1, (DEC_BATCH, DEC_SEQ, D_MODEL)),
        'cache_nsa_cmp_k': rnd(2, kva), 'cache_nsa_cmp_v': rnd(3, kva),
        'cache_nsa_sel_k': rnd(4, kva), 'cache_nsa_sel_v': rnd(5, kva),
        'cache_nsa_win_k': rnd(6, winb), 'cache_nsa_win_v': rnd(7, winb),
        'cache_moba_k': rnd(8, kvb), 'cache_moba_v': rnd(9, kvb),
        'page_table': page_table,
        'norm_attn': gain(10, (D_MODEL,)),
        'w_in': rnd(11, (D_MODEL, w_in_cols), D_MODEL ** -0.5),
        'g_qa': gain(12, (HEAD_DIM,)),
        'g_ka': gain(13, (3, HEAD_DIM)),
        'g_kc': gain(14, (HEAD_DIM,)),
        'cmp_pos_k': rnd(15, (CMP_BLOCK, HEAD_DIM), 0.1),
        'cmp_w1_k': rnd(16, (lk, CMP_HIDDEN), lk ** -0.5),
        'cmp_w2_k': rnd(17, (CMP_HIDDEN, HEAD_DIM), CMP_HIDDEN ** -0.5),
        'cmp_pos_v': rnd(18, (CMP_BLOCK, HEAD_DIM), 0.1),
        'cmp_w1_v': rnd(19, (lk, CMP_HIDDEN), lk ** -0.5),
        'cmp_w2_v': rnd(20, (CMP_HIDDEN, HEAD_DIM), CMP_HIDDEN ** -0.5),
        'g_qb': gain(21, (HEAD_DIM,)),
        'g_kb': gain(22, (HEAD_DIM,)),
        'w_up_a': rnd(23, (wa, D_MODEL), wa ** -0.5),
        'w_up_b': rnd(24, (wb, D_MODEL), wb ** -0.5),
        'w_out': rnd(25, (D_MODEL, D_MODEL), D_MODEL ** -0.5),
        'norm_ffn': gain(26, (D_MODEL,)),
        'w_router': rnd(27, (D_MODEL, N_EXPERTS), D_MODEL ** -0.5),
        'b_router': rnd(28, (N_EXPERTS,), 0.01),
        'w_gu': rnd(29, (N_EXPERTS, D_MODEL, 2 * D_FF), D_MODEL ** -0.5),
        'b_gu': rnd(30, (N_EXPERTS, 2 * D_FF), 0.01),
        'w_down': rnd(31, (N_EXPERTS, D_FF, D_MODEL), D_FF ** -0.5),
        'b_down': rnd(32, (N_EXPERTS, D_MODEL), 0.01),
    }


def reference(x_prompt, x_sample, cache_nsa_cmp_k, cache_nsa_cmp_v, cache_nsa_sel_k, cache_nsa_sel_v,
              cache_nsa_win_k, cache_nsa_win_v, cache_moba_k, cache_moba_v, page_table,
              norm_attn, w_in, g_qa, g_ka, g_kc, cmp_pos_k, cmp_w1_k, cmp_w2_k, cmp_pos_v, cmp_w1_v, cmp_w2_v,
              g_qb, g_kb, w_up_a, w_up_b, w_out, norm_ffn, w_router, b_router, w_gu, b_gu, w_down, b_down):
    B, T, _ = x_prompt.shape
    DB, DS, _ = x_sample.shape
    past = page_table.shape[1] * PAGE_SIZE
    pos_p = jnp.arange(T, dtype=jnp.int32)
    pos_s = past + jnp.arange(DS, dtype=jnp.int32)
    P = mixer_inputs(x_prompt, pos_p, norm_attn, w_in, g_qa, g_ka, g_qb, g_kb)
    S = mixer_inputs(x_sample, pos_s, norm_attn, w_in, g_qa, g_ka, g_qb, g_kb)

    def cmp_kv(k_raw, v_raw):
        kc = rms_norm(compress(k_raw, cmp_pos_k, cmp_w1_k, cmp_w2_k), g_kc)
        vc = compress(v_raw, cmp_pos_v, cmp_w1_v, cmp_w2_v)
        return kc, vc

    def paged(cache, new):
        rows = cache[page_table].reshape((DB, past) + cache.shape[2:])
        return jnp.concatenate([rows, new], axis=1)

    kc_p, vc_p = cmp_kv(P['k_c'], P['v_c'])
    o_a_p = nsa_prompt(P['q_a'], P['g_nsa'], kc_p, vc_p, to_blocks(P['k_s'], SEL_BLOCK),
                       to_blocks(P['v_s'], SEL_BLOCK), P['k_w'], P['v_w'])
    o_b_p = moba_prompt(P['q_b'], P['k_b'], P['v_b'])
    h_p = merge_branches(x_prompt, o_a_p, o_b_p, P['g_mrg'], w_up_a, w_up_b, w_out)

    kc_s, vc_s = cmp_kv(paged(cache_nsa_cmp_k, S['k_c']), paged(cache_nsa_cmp_v, S['v_c']))
    kw_all = jnp.concatenate([cache_nsa_win_k, S['k_w']], axis=1)
    vw_all = jnp.concatenate([cache_nsa_win_v, S['v_w']], axis=1)
    win_pos = past - cache_nsa_win_k.shape[1] + jnp.arange(kw_all.shape[1], dtype=jnp.int32)
    o_a_s = nsa_attend(S['q_a'], S['g_nsa'], pos_s, kc_s, vc_s,
                       to_blocks(paged(cache_nsa_sel_k, S['k_s']), SEL_BLOCK),
                       to_blocks(paged(cache_nsa_sel_v, S['v_s']), SEL_BLOCK),
                       kw_all, vw_all, win_pos)
    kb_blk = to_blocks(paged(cache_moba_k, S['k_b']), MOBA_BLOCK)
    vb_blk = to_blocks(paged(cache_moba_v, S['v_b']), MOBA_BLOCK)
    o_b_s = moba_attend(S['q_b'], pos_s, kb_blk, vb_blk, block_means(kb_blk))
    h_s = merge_branches(x_sample, o_a_s, o_b_s, S['g_mrg'], w_up_a, w_up_b, w_out)

    y = moe_ffn(jnp.concatenate([h_p.reshape(-1, D_MODEL), h_s.reshape(-1, D_MODEL)], axis=0),
                norm_ffn, w_router, b_router, w_gu, b_gu, w_down, b_down)
    y_prompt = y[:B * T].reshape(B, T, D_MODEL)
    y_sample = y[B * T:].reshape(DB, DS, D_MODEL)
    wp = min(WINDOW, T)
    ws = min(WINDOW, kw_all.shape[1])
    return (y_prompt, y_sample,
            P['k_c'], P['v_c'], P['k_s'], P['v_s'], P['k_w'][:, T - wp:], P['v_w'][:, T - wp:], P['k_b'], P['v_b'],
            S['k_c'], S['v_c'], S['k_s'], S['v_s'], kw_all[:, -ws:], vw_all[:, -ws:], S['k_b'], S['v_b'])
```

```python
import functools

import numpy as np
import jax
import jax.numpy as jnp
from jax import lax
from jax.experimental import pallas as pl
from jax.experimental.pallas import tpu as pltpu

F32 = jnp.float32
BF16 = jnp.bfloat16

D_MODEL = 1024
PAGE_SIZE = 128
HEAD_DIM = 64
NSA_HEADS = 8
NSA_KV_HEADS = 2
NSA_GROUP = NSA_HEADS // NSA_KV_HEADS
CMP_BLOCK = 32
CMP_STRIDE = 16
CMP_HIDDEN = 256
SEL_BLOCK = 64
N_SEL = 8
WINDOW = 512
MOBA_HEADS = 8
MOBA_BLOCK = 256
MOBA_TOPK = 3
N_EXPERTS = 32
TOP_K = 4
D_FF = D_MODEL
SWIGLU_LIMIT = 7.0
SWIGLU_ALPHA = 1.702
ROPE_THETA = 10000.0
EPS = 1e-6
NEG = -1e30
POS_BIG = 1e30

LANES = 128
HALF = HEAD_DIM
ATT_SCALE = HEAD_DIM ** -0.5
Q_TILE = 128
SEL_CHUNK = 512
MOE_TILE = 256
VMEM_LIMIT = 56 * 1024 * 1024


def _cparams(*sem):
    return pltpu.CompilerParams(dimension_semantics=sem, vmem_limit_bytes=VMEM_LIMIT)


def _lane_iota(shape):
    return lax.broadcasted_iota(jnp.int32, shape, len(shape) - 1)


def _row_iota(shape):
    return lax.broadcasted_iota(jnp.int32, shape, 0)


def _dot(a, b):
    return jnp.dot(a, b, preferred_element_type=F32)


def _dot_nt(a, b):
    return lax.dot_general(a, b, (((1,), (1,)), ((), ())), preferred_element_type=F32)


def _split(a):
    hi = a.astype(BF16)
    lo = (a - hi.astype(F32)).astype(BF16)
    return hi, lo


def _split_dot(a, b):
    hi, lo = _split(a)
    return _dot(hi, b) + _dot(lo, b)


def _dot3(a, b, nt=False):
    mm = _dot_nt if nt else _dot
    a_hi, a_lo = _split(a)
    b_hi, b_lo = b if isinstance(b, tuple) else _split(b)
    return mm(a_hi, b_hi) + (mm(a_hi, b_lo) + mm(a_lo, b_hi))


def _mm(a, w, precise):
    return _dot3(a, w) if precise else _dot(a.astype(BF16), w)


def _masked_softmax(s, mask):
    s = jnp.where(mask, s, NEG)
    m = jnp.max(s, axis=-1, keepdims=True)
    e = jnp.where(mask, jnp.exp(s - m), 0.0)
    return e / jnp.maximum(jnp.sum(e, axis=-1, keepdims=True), 1e-30)


def _head_norm(y, gain, bd):
    ss = _split_dot(y * y, bd)
    return y * lax.rsqrt(ss * (1.0 / HEAD_DIM) + EPS) * gain


def _rope(y, cos, sin_signed):
    first = (_lane_iota(y.shape) % HEAD_DIM) < (HEAD_DIM // 2)
    partner = jnp.where(first, pltpu.roll(y, LANES - HEAD_DIM // 2, 1), pltpu.roll(y, HEAD_DIM // 2, 1))
    return y * cos + partner * sin_signed


def _select_top(score, n_pick, idx_f, axis):
    sel = jnp.zeros(score.shape, jnp.bool_)
    for _ in range(n_pick):
        m = jnp.max(score, axis=axis, keepdims=True)
        first = jnp.min(jnp.where(score == m, idx_f, 1e9), axis=axis, keepdims=True)
        hit = idx_f == first
        sel = jnp.logical_or(sel, hit)
        score = jnp.where(hit, -jnp.inf, score)
    return sel


def _inproj_kernel(precise, x_ref, cos_ref, sin_ref, gn_ref, wqa, wkva, wgn, wqb, wkb, wvb, wgm,
                   gqa, gka, gqb, gkb, bd_ref,
                   qa_o, kc_o, vc_o, ks_o, vs_o, kw_o, vw_o, gnsa_o, qb_o, kb_o, vb_o, gm_o):
    x = x_ref[...]
    xn = x * lax.rsqrt(jnp.mean(x * x, axis=-1, keepdims=True) + EPS) * gn_ref[...]
    cos = cos_ref[...]
    sin = sin_ref[...]
    bd = bd_ref[...]
    mm = functools.partial(_mm, xn if precise else xn.astype(BF16), precise=precise)

    def tile(j):
        return slice(j * LANES, (j + 1) * LANES)

    def qk(w_ref, gain, j):
        return _rope(_head_norm(mm(w_ref[:, tile(j)]), gain, bd), cos, sin)

    for j in range(4):
        qa_o[:, tile(j)] = qk(wqa, gqa[...], j).astype(qa_o.dtype)
    for i, (k_o, v_o) in enumerate(((kc_o, vc_o), (ks_o, vs_o), (kw_o, vw_o))):
        k_o[...] = qk(wkva, gka[:, tile(i)], 2 * i)
        v_o[...] = mm(wkva[:, tile(2 * i + 1)])
    gnsa_o[...] = jax.nn.sigmoid(mm(wgn[...]))
    for j in range(4):
        qb_o[:, tile(j)] = qk(wqb, gqb[...], j).astype(qb_o.dtype)
        kb_o[:, tile(j)] = qk(wkb, gkb[...], j)
        vb_o[:, tile(j)] = mm(wvb[:, tile(j)])
    for j in range(2 * D_MODEL // LANES):
        gm_o[:, tile(j)] = jax.nn.sigmoid(mm(wgm[:, tile(j)]))


_INPROJ_W = ('wqa', 'wkva', 'wgn', 'wqb', 'wkb', 'wvb', 'wgm')


def _in_proj(x2d, cos, sin, prm, tm, precise):
    n = x2d.shape[0]
    n_pos = cos.shape[0] // tm
    row = lambda i: (i, 0)
    const = lambda i: (0, 0)
    pos = lambda i: (i % n_pos, 0)
    ws = [prm[k + ('_f32' if precise else '')] for k in _INPROJ_W] + [prm[k] for k in ('gqa', 'gka', 'gqb', 'gkb', 'bd')]
    in_specs = [pl.BlockSpec((tm, D_MODEL), row), pl.BlockSpec((tm, LANES), pos), pl.BlockSpec((tm, LANES), pos),
                pl.BlockSpec((1, D_MODEL), const)] + [pl.BlockSpec(w.shape, const) for w in ws]
    qdt = F32 if precise else BF16
    widths = [(512, qdt)] + [(128, F32)] * 6 + [(128, F32), (512, qdt), (512, F32), (512, F32), (2048, F32)]
    out_shape = [jax.ShapeDtypeStruct((n, w), dt) for w, dt in widths]
    out_specs = [pl.BlockSpec((tm, w), row) for w, _ in widths]
    return pl.pallas_call(
        functools.partial(_inproj_kernel, precise), grid=(n // tm,), in_specs=in_specs, out_specs=out_specs,
        out_shape=out_shape, compiler_params=_cparams("parallel"),
        name="in_proj_sample" if precise else "in_proj",
    )(x2d, cos, sin, prm['norm_attn'], *ws)


def _compress_hidden(xs, pos_a, pos_b, w1a, w1b, precise):
    lo = _lane_iota((LANES, LANES)) < HALF
    t0, t1 = [], []
    for l in range(0, CMP_STRIDE, 2):
        a, b = xs[l], xs[l + 1]
        t0.append(jnp.where(lo, a, pltpu.roll(b, HALF, 1)))
        t1.append(jnp.where(lo, pltpu.roll(a, HALF, 1), b))
    x = jnp.concatenate([jnp.concatenate(t0, axis=1), jnp.concatenate(t1, axis=1)], axis=0)
    first = _mm(x + pos_a, w1a, precise)
    second = _mm(x + pos_b, w1b, precise)
    n = LANES
    shifted = jnp.concatenate([pltpu.roll(second[:n], n - 1, 0), pltpu.roll(second[n:], n - 1, 0)], axis=0)
    return jax.nn.gelu(first + shifted)


_CW_NAMES = ('pka', 'pkb', 'k1a', 'k1b', 'k2g0', 'k2g1', 'pva', 'pvb', 'v1a', 'v1b', 'v2g0', 'v2g1', 'gkc', 'bd')


def _compress_prompt_kernel(k_ref, v_ref, *rest):
    cw = dict(zip(_CW_NAMES, rest[:len(_CW_NAMES)]))
    kc_o, vc_o = rest[len(_CW_NAMES):]
    n = k_ref.shape[1] // CMP_STRIDE

    def compress(ref, p):
        xs = [ref[0, pl.ds(l, n, stride=CMP_STRIDE), :] for l in range(CMP_STRIDE)]
        h = _compress_hidden(xs, cw['p%sa' % p][...], cw['p%sb' % p][...], cw['%s1a' % p][...], cw['%s1b' % p][...],
                             False).astype(BF16)
        return _dot(h[:LANES], cw['%s2g0' % p][...]) + _dot(h[LANES:], cw['%s2g1' % p][...])

    kc_o[0] = _head_norm(compress(k_ref, 'k'), cw['gkc'][...], cw['bd'][...])
    vc_o[0] = compress(v_ref, 'v')


def _compress_prompt(k_c, v_c, prm):
    b, t, _ = k_c.shape
    assert t // CMP_STRIDE == LANES
    ws = [prm[k] for k in _CW_NAMES]
    per_b = lambda i: (i, 0, 0)
    const = lambda i: (0, 0)
    return pl.pallas_call(
        _compress_prompt_kernel, grid=(b,),
        in_specs=[pl.BlockSpec((1, t, LANES), per_b)] * 2 + [pl.BlockSpec(w.shape, const) for w in ws],
        out_specs=[pl.BlockSpec((1, LANES, LANES), per_b)] * 2,
        out_shape=[jax.ShapeDtypeStruct((b, LANES, LANES), F32)] * 2,
        compiler_params=_cparams("parallel"), name="compress_prompt",
    )(k_c, v_c, *ws)


def _stack_nsa_q(q):
    lo = _lane_iota((q.shape[0], LANES)) < HALF
    zero = jnp.zeros((q.shape[0], LANES), q.dtype)
    tiles = [q[:, j * LANES:(j + 1) * LANES] for j in range(4)]
    parts = [jnp.where(lo, t, zero) for t in tiles] + [jnp.where(lo, zero, t) for t in tiles]
    return jnp.concatenate(parts, axis=0)


def _nsa_select(imp, blk, cur):
    forced = (blk == 0) | (blk == cur) | (blk == cur - 1)
    valid = blk <= cur
    score = jnp.where(forced, POS_BIG, jnp.where(valid, imp, -jnp.inf))
    sel = _select_top(score, N_SEL, blk.astype(F32), -1)
    return jnp.logical_and(sel, valid)


def _nsa_prompt_kernel(q_ref, g_ref, kc_ref, vc_ref, ks_ref, vs_ref, kw_ref, vw_ref, ov_ref, e_ref,
                       o_ref, m_ref, l_ref, acc_ref):
    qt = Q_TILE
    s0 = pl.program_id(1) * qt
    q = _stack_nsa_q(q_ref[0]) * ATT_SCALE

    def t_of(shape):
        return s0 + (_row_iota(shape) & (qt - 1))

    n_c = kc_ref.shape[1] - 1
    s_c = _dot_nt(q, kc_ref[0].astype(BF16))
    c = _lane_iota(s_c.shape)
    p_c = _masked_softmax(s_c, (c * CMP_STRIDE + CMP_BLOCK - 1 <= t_of(s_c.shape)) & (c < n_c))
    o_c = _dot(p_c.astype(BF16), vc_ref[0].astype(BF16))
    g4 = NSA_GROUP * qt
    imp = jnp.concatenate([sum(p_c[g * g4 + r * qt: g * g4 + (r + 1) * qt] for r in range(NSA_GROUP))
                           for g in range(NSA_KV_HEADS)], axis=0)
    imp = _split_dot(imp, ov_ref[...])
    blk = _lane_iota(imp.shape)
    sel = _nsa_select(imp, blk, t_of(imp.shape) // SEL_BLOCK)
    sel = jnp.where(sel, 1.0, 0.0).astype(BF16)

    m_ref[...] = jnp.full(m_ref.shape, NEG, F32)
    l_ref[...] = jnp.zeros(l_ref.shape, F32)
    acc_ref[...] = jnp.zeros(acc_ref.shape, F32)

    def body(ci, carry):
        off = pl.multiple_of(ci * SEL_CHUNK, SEL_CHUNK)
        k = ks_ref[0, pl.ds(off, SEL_CHUNK), :].astype(BF16)
        v = vs_ref[0, pl.ds(off, SEL_CHUNK), :].astype(BF16)
        s = _dot_nt(q, k)
        picked = _dot(sel, e_ref[ci])
        picked = jnp.concatenate([picked[:qt]] * NSA_GROUP + [picked[qt:]] * NSA_GROUP, axis=0)
        kpos = off + _lane_iota(s.shape)
        mask = (picked > 0.5) & (kpos <= t_of(s.shape))
        s = jnp.where(mask, s, NEG)
        m_prev = m_ref[...]
        m_new = jnp.maximum(m_prev, jnp.max(s, axis=-1, keepdims=True))
        alpha = jnp.exp(m_prev - m_new)
        p = jnp.where(mask, jnp.exp(s - m_new), 0.0)
        l_ref[...] = alpha * l_ref[...] + jnp.sum(p, axis=-1, keepdims=True)
        acc_ref[...] = alpha * acc_ref[...] + _dot(p.astype(BF16), v)
        m_ref[...] = m_new
        return carry

    lax.fori_loop(0, (s0 + qt + SEL_CHUNK - 1) // SEL_CHUNK, body, 0)
    o_s = acc_ref[...] / jnp.maximum(l_ref[...], 1e-30)

    span = WINDOW + qt
    start = pl.multiple_of(jnp.maximum(s0 - WINDOW, 0), qt)
    s_w = _dot_nt(q, kw_ref[0, pl.ds(start, span), :].astype(BF16))
    wp = start + _lane_iota(s_w.shape)
    t_w = t_of(s_w.shape)
    p_w = _masked_softmax(s_w, (wp <= t_w) & (wp > t_w - WINDOW))
    o_w = _dot(p_w.astype(BF16), vw_ref[0, pl.ds(start, span), :].astype(BF16))

    g = g_ref[0]
    lo = _lane_iota((qt, LANES)) < HALF

    def head_out(h):
        r = slice(h * qt, (h + 1) * qt)
        gate = lambda j: g[:, j * NSA_HEADS + h: j * NSA_HEADS + h + 1]
        return gate(0) * o_c[r] + gate(1) * o_s[r] + gate(2) * o_w[r]

    for j in range(4):
        o_ref[0, :, j * LANES:(j + 1) * LANES] = jnp.where(lo, head_out(j), head_out(4 + j)).astype(BF16)


def _nsa_prompt(q_a, g_nsa, kc, vc, k_s, v_s, k_w, v_w, prm):
    b, t, _ = q_a.shape
    qt = Q_TILE
    rows = NSA_HEADS * qt
    per_q = lambda i, j: (i, j, 0)
    per_b = lambda i, j: (i, 0, 0)
    return pl.pallas_call(
        _nsa_prompt_kernel, grid=(b, t // qt),
        in_specs=[pl.BlockSpec((1, qt, 512), per_q), pl.BlockSpec((1, qt, LANES), per_q),
                  pl.BlockSpec((1, LANES, LANES), per_b), pl.BlockSpec((1, LANES, LANES), per_b)]
        + [pl.BlockSpec((1, t, LANES), per_b)] * 4
        + [pl.BlockSpec(prm['ov'].shape, lambda i, j: (0, 0)),
           pl.BlockSpec(prm['e_sel'].shape, lambda i, j: (0, 0, 0))],
        out_specs=pl.BlockSpec((1, qt, 512), per_q),
        out_shape=jax.ShapeDtypeStruct((b, t, 512), BF16),
        scratch_shapes=[pltpu.VMEM((rows, 1), F32), pltpu.VMEM((rows, 1), F32), pltpu.VMEM((rows, LANES), F32)],
        compiler_params=_cparams("arbitrary", "arbitrary"), name="nsa_prompt",
    )(q_a, g_nsa, kc, vc, k_s, v_s, k_w, v_w, prm['ov'], prm['e_sel'])


def _moba_prompt_kernel(q_ref, k_ref, v_ref, o_ref, kmean_ref, m_ref, l_ref, acc_ref):
    qt = Q_TILE
    qi = pl.program_id(1)
    s0 = qi * qt
    cur = s0 // MOBA_BLOCK
    n_b = k_ref.shape[1] // MOBA_BLOCK

    @pl.when(qi == 0)
    def _():
        kmean_ref[...] = jnp.zeros(kmean_ref.shape, F32)
        for n in range(n_b):
            blk_rows = k_ref[0, n * MOBA_BLOCK:(n + 1) * MOBA_BLOCK, :]
            kmean_ref[n:n + 1, :] = jnp.mean(blk_rows, axis=0, keepdims=True)

    lo = _lane_iota((qt, LANES)) < HALF
    for j in range(4):
        tl = slice(j * LANES, (j + 1) * LANES)
        qj = q_ref[0, :, tl]
        zero = jnp.zeros_like(qj)
        qp = jnp.concatenate([jnp.where(lo, qj, zero), jnp.where(lo, zero, qj)], axis=0)
        km_hi, km_lo = _split(kmean_ref[:, tl])
        s_g = _dot_nt(qp, km_hi) + _dot_nt(qp, km_lo)
        blk = _lane_iota(s_g.shape)
        past = blk < cur
        sel = _select_top(jnp.where(past, s_g, -jnp.inf), MOBA_TOPK, blk.astype(F32), -1)
        sel = jnp.where(jnp.logical_and(sel, past), 1.0, 0.0)
        qs = qp * ATT_SCALE

        m_ref[...] = jnp.full(m_ref.shape, NEG, F32)
        l_ref[...] = jnp.zeros(l_ref.shape, F32)
        acc_ref[...] = jnp.zeros(acc_ref.shape, F32)

        def body(n, carry):
            off = pl.multiple_of(n * MOBA_BLOCK, MOBA_BLOCK)
            k = k_ref[0, pl.ds(off, MOBA_BLOCK), tl].astype(BF16)
            v = v_ref[0, pl.ds(off, MOBA_BLOCK), tl].astype(BF16)
            s = _dot_nt(qs, k)
            sel_n = jnp.sum(jnp.where(blk == n, sel, 0.0), axis=-1, keepdims=True)
            kpos = off + _lane_iota(s.shape)
            t = s0 + (_row_iota(s.shape) & (qt - 1))
            own = jnp.where(n == cur, 1.0, 0.0)
            mask = (jnp.where(kpos <= t, own, 0.0) + sel_n) > 0.5
            s = jnp.where(mask, s, NEG)
            m_prev = m_ref[...]
            m_new = jnp.maximum(m_prev, jnp.max(s, axis=-1, keepdims=True))
            alpha = jnp.exp(m_prev - m_new)
            p = jnp.where(mask, jnp.exp(s - m_new), 0.0)
            l_ref[...] = alpha * l_ref[...] + jnp.sum(p, axis=-1, keepdims=True)
            acc_ref[...] = alpha * acc_ref[...] + _dot(p.astype(BF16), v)
            m_ref[...] = m_new
            return carry

        lax.fori_loop(0, cur + 1, body, 0)
        o = acc_ref[...] / jnp.maximum(l_ref[...], 1e-30)
        o_ref[0, :, tl] = jnp.where(lo, o[:qt], o[qt:]).astype(BF16)


def _moba_prompt(q_b, k_b, v_b):
    b, t, w = q_b.shape
    qt = Q_TILE
    assert MOBA_BLOCK % qt == 0
    per_q = lambda i, j: (i, j, 0)
    per_b = lambda i, j: (i, 0, 0)
    return pl.pallas_call(
        _moba_prompt_kernel, grid=(b, t // qt),
        in_specs=[pl.BlockSpec((1, qt, w), per_q), pl.BlockSpec((1, t, w), per_b), pl.BlockSpec((1, t, w), per_b)],
        out_specs=pl.BlockSpec((1, qt, w), per_q),
        out_shape=jax.ShapeDtypeStruct((b, t, w), BF16),
        scratch_shapes=[pltpu.VMEM((LANES, w), F32), pltpu.VMEM((2 * qt, 1), F32), pltpu.VMEM((2 * qt, 1), F32),
                        pltpu.VMEM((2 * qt, LANES), F32)],
        compiler_params=_cparams("arbitrary", "arbitrary"), name="moba_prompt",
    )(q_b, k_b, v_b)


def _merge_kernel(precise, x_ref, oa_ref, ob_ref, gm_ref, wa_ref, wb_ref, wo_ref, nf_ref, wr_hi, wr_lo, br_ref,
                  h_o, xn_o, lg_o):
    gm = gm_ref[...]
    u = (gm[:, :D_MODEL] * _mm(oa_ref[...], wa_ref[...], precise)
         + gm[:, D_MODEL:] * _mm(ob_ref[...], wb_ref[...], precise))
    h = x_ref[...] + _mm(u, wo_ref[...], precise)
    h_o[...] = h
    xn = h * lax.rsqrt(jnp.mean(h * h, axis=-1, keepdims=True) + EPS) * nf_ref[...]
    xn_o[...] = xn.astype(BF16)
    lg_o[...] = _dot3(xn, (wr_hi[...], wr_lo[...])) + br_ref[...]


def _merge(x2d, o_a, o_b, g_mrg, prm, tm, precise):
    n = x2d.shape[0]
    row = lambda i: (i, 0)
    const = lambda i: (0, 0)
    sfx = '_f32' if precise else ''
    ws = [prm[k] for k in ('w_up_a' + sfx, 'w_up_b' + sfx, 'w_out' + sfx, 'norm_ffn', 'wr_hi', 'wr_lo', 'b_router')]
    return pl.pallas_call(
        functools.partial(_merge_kernel, precise), grid=(n // tm,),
        in_specs=[pl.BlockSpec((tm, D_MODEL), row), pl.BlockSpec((tm, 512), row), pl.BlockSpec((tm, 512), row),
                  pl.BlockSpec((tm, 2 * D_MODEL), row)] + [pl.BlockSpec(w.shape, const) for w in ws],
        out_specs=[pl.BlockSpec((tm, D_MODEL), row), pl.BlockSpec((tm, D_MODEL), row), pl.BlockSpec((tm, LANES), row)],
        out_shape=[jax.ShapeDtypeStruct((n, D_MODEL), F32), jax.ShapeDtypeStruct((n, D_MODEL), BF16),
                   jax.ShapeDtypeStruct((n, LANES), F32)],
        compiler_params=_cparams("parallel"), name="merge_router_sample" if precise else "merge_router",
    )(x2d, o_a, o_b, g_mrg, *ws)


def _moe_kernel(be_ref, bv_ref, x_ref, g_ref, wgu_ref, bgu_ref, wd_ref, bd_ref, y_ref):
    i = pl.program_id(0)

    @pl.when(bv_ref[i] > 0)
    def _():
        h = _dot(x_ref[...], wgu_ref[0]) + bgu_ref[0]
        gate = jnp.minimum(h[:, :D_FF], SWIGLU_LIMIT)
        up = jnp.clip(h[:, D_FF:], -SWIGLU_LIMIT, SWIGLU_LIMIT)
        act = (up + 1.0) * gate * jax.nn.sigmoid(SWIGLU_ALPHA * gate)
        y = _dot(act.astype(BF16), wd_ref[0]) + bd_ref[0]
        y_ref[...] = y * g_ref[...]

    @pl.when(bv_ref[i] == 0)
    def _():
        y_ref[...] = jnp.zeros(y_ref.shape, F32)


def _moe_blocks(block_e, block_valid, x_sorted, slot_gate, w_gu, b_gu, w_down, b_down):
    cap = x_sorted.shape[0]
    tm = MOE_TILE
    row = lambda i, be, bv: (i, 0)
    exp = lambda i, be, bv: (be[i], 0, 0)
    grid_spec = pltpu.PrefetchScalarGridSpec(
        num_scalar_prefetch=2, grid=(cap // tm,),
        in_specs=[pl.BlockSpec((tm, D_MODEL), row), pl.BlockSpec((tm, 1), row),
                  pl.BlockSpec((1, D_MODEL, 2 * D_FF), exp), pl.BlockSpec((1, 1, 2 * D_FF), exp),
                  pl.BlockSpec((1, D_FF, D_MODEL), exp), pl.BlockSpec((1, 1, D_MODEL), exp)],
        out_specs=pl.BlockSpec((tm, D_MODEL), row))
    return pl.pallas_call(
        _moe_kernel, grid_spec=grid_spec, out_shape=jax.ShapeDtypeStruct((cap, D_MODEL), F32),
        compiler_params=_cparams("arbitrary"), name="moe_experts",
    )(block_e, block_valid, x_sorted, slot_gate, w_gu, b_gu, w_down, b_down)


def _route(logits):
    n = logits.shape[0]
    top_v, top_e = lax.top_k(logits, TOP_K)
    gates = jax.nn.softmax(top_v, axis=-1)
    onehot = jnp.sum((top_e[:, :, None] == jnp.arange(N_EXPERTS)[None, None, :]).astype(jnp.int32), axis=1)
    csum = jnp.cumsum(onehot, axis=0)
    counts = csum[-1]
    padded = (counts + MOE_TILE - 1) // MOE_TILE * MOE_TILE
    pend = jnp.cumsum(padded)
    rank = jnp.take_along_axis(csum - onehot, top_e, axis=1)
    slot = (pend - padded)[top_e] + rank
    n_blocks = -(-(n * TOP_K) // MOE_TILE) + N_EXPERTS
    cap = n_blocks * MOE_TILE
    starts = jnp.arange(n_blocks, dtype=jnp.int32) * MOE_TILE
    block_e = jnp.minimum(jnp.searchsorted(pend, starts, side='right'), N_EXPERTS - 1).astype(jnp.int32)
    block_valid = (starts < pend[-1]).astype(jnp.int32)
    tok = jnp.broadcast_to(jnp.arange(n, dtype=jnp.int32)[:, None], (n, TOP_K))
    slot_tok = jnp.zeros((cap,), jnp.int32).at[slot.reshape(-1)].set(tok.reshape(-1))
    slot_gate = jnp.zeros((cap,), F32).at[slot.reshape(-1)].set(gates.reshape(-1))
    return slot, slot_tok, slot_gate.reshape(cap, 1), block_e, block_valid


def _column(ref, b):
    x = ref[...]
    return jnp.sum(jnp.where(_lane_iota(x.shape) == b, x, 0.0), axis=-1, keepdims=True)


def _fold_lanes(x):
    return sum(x[..., i * LANES:(i + 1) * LANES] for i in range(x.shape[-1] // LANES))


def _gqa_scores(q3, kt):
    g = NSA_GROUP
    return jnp.concatenate([jnp.sum(q3[i * g:(i + 1) * g] * kt[i * HEAD_DIM:(i + 1) * HEAD_DIM][None], axis=1)
                            for i in range(NSA_KV_HEADS)], axis=0)


def _gqa_weighted(p, vt):
    g = NSA_GROUP
    return jnp.concatenate([p[i * g:(i + 1) * g][:, None, :] * vt[i * HEAD_DIM:(i + 1) * HEAD_DIM][None]
                            for i in range(NSA_KV_HEADS)], axis=0)


_CWS_NAMES = ('pka', 'pkb', 'k1a_hi', 'k1a_lo', 'k1b_hi', 'k1b_lo', 'k2t',
              'pva', 'pvb', 'v1a_hi', 'v1a_lo', 'v1b_hi', 'v1b_lo', 'v2t', 'gkc_col')


def _nsa_sample_kernel(pt_ref, qt_ref, g_ref, ksn_ref, vsn_ref, kwn_ref, vwn_ref, wk_ref, wv_ref, *rest):
    n_pg = (len(rest) - len(_CWS_NAMES) - 3) // 4
    ck_pages = rest[0:n_pg]
    cv_pages = rest[n_pg:2 * n_pg]
    sk_pages = rest[2 * n_pg:3 * n_pg]
    sv_pages = rest[3 * n_pg:4 * n_pg]
    cw = dict(zip(_CWS_NAMES, rest[4 * n_pg:4 * n_pg + len(_CWS_NAMES)]))
    ov_ref, e_ref, o_ref = rest[4 * n_pg + len(_CWS_NAMES):]
    b = pl.program_id(0)
    past = n_pg * PAGE_SIZE
    per_pg = PAGE_SIZE // CMP_STRIDE

    def compress_t(pages, p):
        xs = [jnp.concatenate([pg[0, pl.ds(l, per_pg, stride=CMP_STRIDE), :] for pg in pages], axis=0)
              for l in range(CMP_STRIDE)]
        h = _compress_hidden(xs, cw['p%sa' % p][...], cw['p%sb' % p][...],
                             (cw['%s1a_hi' % p][...], cw['%s1a_lo' % p][...]),
                             (cw['%s1b_hi' % p][...], cw['%s1b_lo' % p][...]), True)
        w2t = cw['%s2t' % p][...]
        return jnp.concatenate([_dot3(w2t, h[:LANES], nt=True), _dot3(w2t, h[LANES:], nt=True)], axis=0)

    kct = compress_t(ck_pages, 'k').reshape(NSA_KV_HEADS, HEAD_DIM, LANES)
    ss = jnp.sum(kct * kct, axis=1, keepdims=True)
    kct = (kct * lax.rsqrt(ss * (1.0 / HEAD_DIM) + EPS) * cw['gkc_col'][...][None]).reshape(LANES, LANES)
    vct = compress_t(cv_pages, 'v')

    q3 = (_column(qt_ref, b) * ATT_SCALE).reshape(NSA_HEADS, HEAD_DIM, 1)

    n_c = (past + 1 - CMP_BLOCK) // CMP_STRIDE + 1
    s_c = _gqa_scores(q3, kct)
    p_c = _masked_softmax(s_c, _lane_iota(s_c.shape) < n_c)
    o_c = jnp.sum(_gqa_weighted(p_c, vct), axis=-1, keepdims=True)
    row = _row_iota((NSA_HEADS, LANES))
    i0 = jnp.sum(p_c[:NSA_GROUP], axis=0, keepdims=True)
    i1 = jnp.sum(p_c[NSA_GROUP:], axis=0, keepdims=True)
    imp = _split_dot(jnp.where(row < NSA_GROUP, i0, i1), ov_ref[...])
    sel = _nsa_select(imp, _lane_iota(imp.shape), past // SEL_BLOCK)
    picked = _dot(jnp.where(sel, 1.0, 0.0).astype(BF16), e_ref[...]) > 0.5

    def attend_with_new(s, valid, kn_ref, vn_ref, weighted):
        kn = _column(kn_ref, b)
        vn = _column(vn_ref, b)
        s = jnp.where(valid, s, NEG)
        s_n = _gqa_scores(q3, kn)
        m = jnp.maximum(jnp.max(s, axis=-1, keepdims=True), s_n)
        p = jnp.where(valid, jnp.exp(s - m), 0.0)
        p_n = jnp.exp(s_n - m)
        num = jnp.sum(weighted(p), axis=-1, keepdims=True) + _gqa_weighted(p_n, vn)
        den = jnp.maximum(jnp.sum(p, axis=-1, keepdims=True) + p_n, 1e-30)
        return num / den[:, :, None]

    s_s = jnp.concatenate([_gqa_scores(q3, pg[0]) for pg in sk_pages], axis=1)
    o_s = attend_with_new(
        s_s, picked, ksn_ref, vsn_ref,
        lambda p: sum(_gqa_weighted(p[:, i * PAGE_SIZE:(i + 1) * PAGE_SIZE], pg[0]) for i, pg in enumerate(sv_pages)))

    wb = wk_ref.shape[2]
    s_w = _gqa_scores(q3, wk_ref[0])
    inside = (past - wb + _lane_iota(s_w.shape)) > (past - WINDOW)
    o_w = attend_with_new(s_w, inside, kwn_ref, vwn_ref, lambda p: _fold_lanes(_gqa_weighted(p, wv_ref[0])))

    g = jnp.broadcast_to(g_ref[0], (NSA_HEADS, LANES))
    lane = _lane_iota(g.shape)
    gate = lambda j: jnp.sum(jnp.where(lane == j * NSA_HEADS + row, g, 0.0), axis=-1, keepdims=True)[:, :, None]
    o = (gate(0) * o_c + gate(1) * o_s + gate(2) * o_w).reshape(NSA_HEADS * HEAD_DIM, 1)

    @pl.when(b == 0)
    def _():
        o_ref[...] = jnp.zeros(o_ref.shape, F32)

    o_ref[...] = jnp.where(_lane_iota(o_ref.shape) == b, o, o_ref[...])


def _nsa_sample(page_table, qt, g_nsa, ksn, vsn, kwn, vwn, win_kt, win_vt, cache_ck, cache_cv, cache_skt, cache_svt,
                prm):
    db, n_pg = page_table.shape
    wb = win_kt.shape[2]
    per_b = lambda i, pt: (i, 0, 0)
    const2 = lambda i, pt: (0, 0)

    def page_spec(p):
        return pl.BlockSpec((1, PAGE_SIZE, LANES), lambda i, pt, p=p: (pt[i, p], 0, 0))

    cws = [prm['s_' + k] for k in _CWS_NAMES]
    full = lambda a: pl.BlockSpec(a.shape, const2)
    in_specs = ([full(qt), pl.BlockSpec((1, 1, LANES), per_b), full(ksn), full(vsn), full(kwn), full(vwn),
                 pl.BlockSpec((1, LANES, wb), per_b), pl.BlockSpec((1, LANES, wb), per_b)]
                + [page_spec(p) for p in range(n_pg)] * 4
                + [full(w) for w in cws] + [full(prm['ov']), full(prm['e_key'])])
    grid_spec = pltpu.PrefetchScalarGridSpec(num_scalar_prefetch=1, grid=(db,), in_specs=in_specs,
                                             out_specs=pl.BlockSpec((NSA_HEADS * HEAD_DIM, db), const2))
    pages = [cache_ck] * n_pg + [cache_cv] * n_pg + [cache_skt] * n_pg + [cache_svt] * n_pg
    return pl.pallas_call(
        _nsa_sample_kernel, grid_spec=grid_spec, out_shape=jax.ShapeDtypeStruct((NSA_HEADS * HEAD_DIM, db), F32),
        compiler_params=_cparams("arbitrary"), name="nsa_sample",
    )(page_table, qt, g_nsa, ksn, vsn, kwn, vwn, win_kt, win_vt, *pages, *cws, prm['ov'], prm['e_key'])


def _moba_sample_kernel(pt_ref, qt_ref, kn_ref, vn_ref, *rest):
    n_pg = (len(rest) - 1) // 2
    k_pages = rest[:n_pg]
    v_pages = rest[n_pg:2 * n_pg]
    o_ref = rest[2 * n_pg]
    b = pl.program_id(0)
    pages_per_blk = MOBA_BLOCK // PAGE_SIZE
    n_b = n_pg // pages_per_blk
    heads3 = lambda x: x.reshape(MOBA_HEADS, HEAD_DIM, x.shape[-1])

    q3 = heads3(_column(qt_ref, b) * ATT_SCALE)
    kn3 = heads3(_column(kn_ref, b))
    vn3 = heads3(_column(vn_ref, b))
    s_pages = [jnp.sum(heads3(pg[0]) * q3, axis=1) for pg in k_pages]
    s_new = jnp.sum(q3 * kn3, axis=1)

    lane = _lane_iota((MOBA_HEADS, LANES))
    s_g = jnp.full((MOBA_HEADS, LANES), -jnp.inf, F32)
    for n in range(n_b):
        tot = sum(s_pages[n * pages_per_blk + i] for i in range(pages_per_blk))
        s_g = jnp.where(lane == n, jnp.sum(tot, axis=-1, keepdims=True), s_g)
    sel = _select_top(s_g, min(MOBA_TOPK, n_b), lane.astype(F32), -1)
    sel_blk = [jnp.sum(jnp.where(jnp.logical_and(sel, lane == n), 1.0, 0.0), axis=-1, keepdims=True) > 0.5
               for n in range(n_b)]

    m = s_new
    for i in range(n_pg):
        m = jnp.maximum(m, jnp.max(jnp.where(sel_blk[i // pages_per_blk], s_pages[i], NEG), axis=-1, keepdims=True))
    p_new = jnp.exp(s_new - m)
    l = p_new
    acc = jnp.zeros((MOBA_HEADS, HEAD_DIM, PAGE_SIZE), F32)
    for i in range(n_pg):
        p = jnp.where(sel_blk[i // pages_per_blk], jnp.exp(s_pages[i] - m), 0.0)
        l = l + jnp.sum(p, axis=-1, keepdims=True)
        acc = acc + heads3(v_pages[i][0]) * p[:, None, :]
    o = (jnp.sum(acc, axis=-1, keepdims=True) + p_new[:, :, None] * vn3) / jnp.maximum(l, 1e-30)[:, :, None]

    @pl.when(b == 0)
    def _():
        o_ref[...] = jnp.zeros(o_ref.shape, F32)

    o_ref[...] = jnp.where(_lane_iota(o_ref.shape) == b, o.reshape(MOBA_HEADS * HEAD_DIM, 1), o_ref[...])


def _moba_sample(page_table, qt, knt, vnt, cache_kt, cache_vt):
    db, n_pg = page_table.shape
    w = qt.shape[0]
    assert (n_pg * PAGE_SIZE) % MOBA_BLOCK == 0
    const2 = lambda i, pt: (0, 0)

    def page_spec(p):
        return pl.BlockSpec((1, w, PAGE_SIZE), lambda i, pt, p=p: (pt[i, p], 0, 0))

    in_specs = [pl.BlockSpec((w, db), const2)] * 3 + [page_spec(p) for p in range(n_pg)] * 2
    grid_spec = pltpu.PrefetchScalarGridSpec(num_scalar_prefetch=1, grid=(db,), in_specs=in_specs,
                                             out_specs=pl.BlockSpec((w, db), const2))
    return pl.pallas_call(
        _moba_sample_kernel, grid_spec=grid_spec, out_shape=jax.ShapeDtypeStruct((w, db), F32),
        compiler_params=_cparams("arbitrary"), name="moba_sample",
    )(page_table, qt, knt, vnt, *([cache_kt] * n_pg), *([cache_vt] * n_pg))


def _split_weight(w):
    c = w * (2.0 ** 16 + 1.0)
    hi = c - (c - w)
    return hi.astype(BF16), (w - hi).astype(BF16)


def _nsa_head_order():
    return np.concatenate([np.r_[j * HEAD_DIM:(j + 1) * HEAD_DIM, (4 + j) * HEAD_DIM:(5 + j) * HEAD_DIM]
                           for j in range(4)])


def _prepare(norm_attn, w_in, g_qa, g_ka, g_kc, cmp_pos_k, cmp_w1_k, cmp_w2_k, cmp_pos_v, cmp_w1_v, cmp_w2_v,
             g_qb, g_kb, w_up_a, w_up_b, w_out, norm_ffn, w_router, b_router):
    qa = NSA_HEADS * HEAD_DIM
    kva = NSA_KV_HEADS * HEAD_DIM
    qb = MOBA_HEADS * HEAD_DIM
    widths = [qa, kva, kva, kva, kva, kva, kva, 3 * NSA_HEADS, qb, qb, qb, 2 * D_MODEL]
    cuts = np.cumsum([0] + widths)
    part = lambda i: w_in[:, cuts[i]:cuts[i + 1]]
    order = _nsa_head_order()
    gcols = np.array([h * 3 + j for j in range(3) for h in range(NSA_HEADS)])
    wgn = jnp.zeros((D_MODEL, LANES), F32).at[:, :3 * NSA_HEADS].set(part(7)[:, gcols])
    two = lambda g: jnp.concatenate([g, g]).reshape(1, LANES)
    lane = np.arange(LANES)
    c_idx = np.arange(LANES)[:, None] * CMP_STRIDE
    s_idx = np.arange(LANES)[None, :] * SEL_BLOCK
    ov = ((c_idx < s_idx + SEL_BLOCK) & (c_idx + CMP_BLOCK > s_idx) & (np.arange(LANES)[:, None] < LANES - 1))
    half_w = CMP_STRIDE * HEAD_DIM

    def w2wide(w2):
        z = jnp.zeros_like(w2)
        return jnp.concatenate([w2, z], axis=1).astype(BF16), jnp.concatenate([z, w2], axis=1).astype(BF16)

    k2g0, k2g1 = w2wide(cmp_w2_k)
    v2g0, v2g1 = w2wide(cmp_w2_v)
    wr_hi, wr_lo = _split_weight(jnp.zeros((D_MODEL, LANES), F32).at[:, :N_EXPERTS].set(w_router))
    prm = {
        'norm_attn': norm_attn.reshape(1, D_MODEL),
        'wqa_f32': part(0)[:, order], 'wkva_f32': w_in[:, cuts[1]:cuts[7]], 'wgn_f32': wgn,
        'wqb_f32': part(8), 'wkb_f32': part(9), 'wvb_f32': part(10), 'wgm_f32': part(11),
        'gqa': two(g_qa), 'gka': jnp.concatenate([two(g_ka[i]) for i in range(3)], axis=1),
        'gqb': two(g_qb), 'gkb': two(g_kb), 'gkc': two(g_kc),
        'bd': jnp.asarray((lane[:, None] // HEAD_DIM) == (lane[None, :] // HEAD_DIM), BF16),
        'pka': cmp_pos_k[:CMP_STRIDE].reshape(1, half_w), 'pkb': cmp_pos_k[CMP_STRIDE:].reshape(1, half_w),
        'pva': cmp_pos_v[:CMP_STRIDE].reshape(1, half_w), 'pvb': cmp_pos_v[CMP_STRIDE:].reshape(1, half_w),
        'k1a': cmp_w1_k[:half_w].astype(BF16), 'k1b': cmp_w1_k[half_w:].astype(BF16),
        'v1a': cmp_w1_v[:half_w].astype(BF16), 'v1b': cmp_w1_v[half_w:].astype(BF16),
        'k2g0': k2g0, 'k2g1': k2g1, 'v2g0': v2g0, 'v2g1': v2g1,
        'ov': jnp.asarray(ov, BF16),
        'w_up_a_f32': w_up_a[order], 'w_up_b_f32': w_up_b, 'w_out_f32': w_out,
        'norm_ffn': norm_ffn.reshape(1, D_MODEL),
        'wr_hi': wr_hi, 'wr_lo': wr_lo,
        'b_router': jnp.zeros((1, LANES), F32).at[0, :N_EXPERTS].set(b_router),
    }
    for k in _INPROJ_W + ('w_up_a', 'w_up_b', 'w_out'):
        prm[k] = prm[k + '_f32'].astype(BF16)
    for p, w1, w2 in (('k', cmp_w1_k, cmp_w2_k), ('v', cmp_w1_v, cmp_w2_v)):
        for half, w in (('a', w1[:half_w]), ('b', w1[half_w:])):
            prm['s_%s1%s_hi' % (p, half)], prm['s_%s1%s_lo' % (p, half)] = _split_weight(w)
        prm['s_%s2t' % p] = w2.T
        prm['s_p%sa' % p] = prm['p%sa' % p]
        prm['s_p%sb' % p] = prm['p%sb' % p]
    prm['s_gkc_col'] = g_kc.reshape(HEAD_DIM, 1)
    return prm


def _rope_tables(pos):
    half = HEAD_DIM // 2
    inv = ROPE_THETA ** (-np.arange(half, dtype=np.float64) / half)
    ang = np.asarray(pos, np.float64)[:, None] * inv[None, :]
    cos, sin = np.cos(ang), np.sin(ang)
    cos = np.concatenate([cos, cos, cos, cos], axis=1)
    sin = np.concatenate([-sin, sin, -sin, sin], axis=1)
    return jnp.asarray(cos, F32), jnp.asarray(sin, F32)


def _expand_matrix(n_keys, block, chunk=None):
    e = (np.arange(n_keys)[None, :] // block) == np.arange(LANES)[:, None]
    if chunk is not None:
        e = e.reshape(LANES, n_keys // chunk, chunk).transpose(1, 0, 2)
    return jnp.asarray(e, BF16)


def kernel(x_prompt, x_sample, cache_nsa_cmp_k, cache_nsa_cmp_v, cache_nsa_sel_k, cache_nsa_sel_v, cache_nsa_win_k, cache_nsa_win_v, cache_moba_k, cache_moba_v, page_table, norm_attn, w_in, g_qa, g_ka, g_kc, cmp_pos_k, cmp_w1_k, cmp_w2_k, cmp_pos_v, cmp_w1_v, cmp_w2_v, g_qb, g_kb, w_up_a, w_up_b, w_out, norm_ffn, w_router, b_router, w_gu, b_gu, w_down, b_down):
    B, T, _ = x_prompt.shape
    DB, DS, _ = x_sample.shape
    assert DS == 1
    n_pg = page_table.shape[1]
    past = n_pg * PAGE_SIZE
    n_pool = cache_nsa_cmp_k.shape[0]
    wb = cache_nsa_win_k.shape[1]

    prm = _prepare(norm_attn, w_in, g_qa, g_ka, g_kc, cmp_pos_k, cmp_w1_k, cmp_w2_k, cmp_pos_v, cmp_w1_v, cmp_w2_v,
                   g_qb, g_kb, w_up_a, w_up_b, w_out, norm_ffn, w_router, b_router)
    prm['e_sel'] = _expand_matrix(T, SEL_BLOCK, SEL_CHUNK)
    prm['e_key'] = _expand_matrix(past, SEL_BLOCK)
    order = _nsa_head_order()

    cos_p, sin_p = _rope_tables(np.arange(T))
    (qa_p, kc_p, vc_p, ks_p, vs_p, kw_p, vw_p, gn_p, qb_p, kb_p, vb_p, gm_p) = _in_proj(
        x_prompt.reshape(B * T, D_MODEL), cos_p, sin_p, prm, 256, False)
    r3 = lambda a: a.reshape(B, T, a.shape[-1])
    kcc_p, vcc_p = _compress_prompt(r3(kc_p), r3(vc_p), prm)
    oa_p = _nsa_prompt(r3(qa_p), r3(gn_p), kcc_p, vcc_p, r3(ks_p), r3(vs_p), r3(kw_p), r3(vw_p), prm)
    ob_p = _moba_prompt(r3(qb_p), r3(kb_p), r3(vb_p))
    h_p, xn_p, lg_p = _merge(x_prompt.reshape(B * T, D_MODEL), oa_p.reshape(B * T, 512), ob_p.reshape(B * T, 512),
                             gm_p, prm, 256, False)

    cos_s, sin_s = _rope_tables(np.full((DB,), past))
    (qa_s, kc_s, vc_s, ks_s, vs_s, kw_s, vw_s, gn_s, qb_s, kb_s, vb_s, gm_s) = _in_proj(
        x_sample.reshape(DB, D_MODEL), cos_s, sin_s, prm, DB, True)
    rows2 = lambda c: c.reshape(c.shape[0], c.shape[1], c.shape[2] * c.shape[3])
    cols2 = lambda c: jnp.transpose(c, (0, 2, 3, 1)).reshape(c.shape[0], c.shape[2] * c.shape[3], c.shape[1])
    oa_t = _nsa_sample(page_table, qa_s[:, np.argsort(order)].T, gn_s.reshape(DB, 1, LANES),
                       ks_s.T, vs_s.T, kw_s.T, vw_s.T, cols2(cache_nsa_win_k), cols2(cache_nsa_win_v),
                       rows2(cache_nsa_cmp_k), rows2(cache_nsa_cmp_v), cols2(cache_nsa_sel_k), cols2(cache_nsa_sel_v),
                       prm)
    ob_t = _moba_sample(page_table, qb_s.T, kb_s.T, vb_s.T, cols2(cache_moba_k), cols2(cache_moba_v))
    h_s, xn_s, lg_s = _merge(x_sample.reshape(DB, D_MODEL), oa_t.T[:, order], ob_t.T, gm_s, prm, DB, True)

    h_all = jnp.concatenate([h_p, h_s], axis=0)
    xn_all = jnp.concatenate([xn_p, xn_s], axis=0)
    logits = jnp.concatenate([lg_p, lg_s], axis=0)[:, :N_EXPERTS]
    slot, slot_tok, slot_gate, block_e, block_valid = _route(logits)
    y_slot = _moe_blocks(block_e, block_valid, xn_all[slot_tok], slot_gate,
                         w_gu.astype(BF16), b_gu.reshape(N_EXPERTS, 1, 2 * D_FF),
                         w_down.astype(BF16), b_down.reshape(N_EXPERTS, 1, D_MODEL))
    y = h_all + jnp.sum(y_slot[slot], axis=1)
    y_prompt = y[:B * T].reshape(B, T, D_MODEL)
    y_sample = y[B * T:].reshape(DB, DS, D_MODEL)

    kv4 = lambda a, n, h: a.reshape(n, -1, h, HEAD_DIM)
    wp = min(WINDOW, T)
    ws = min(WINDOW, wb + DS)
    new4 = lambda a: a.reshape(DB, 1, NSA_KV_HEADS, HEAD_DIM)
    kw_all = jnp.concatenate([cache_nsa_win_k, new4(kw_s)], axis=1)[:, -ws:]
    vw_all = jnp.concatenate([cache_nsa_win_v, new4(vw_s)], axis=1)[:, -ws:]
    return (y_prompt, y_sample,
            kv4(kc_p, B, NSA_KV_HEADS), kv4(vc_p, B, NSA_KV_HEADS), kv4(ks_p, B, NSA_KV_HEADS),
            kv4(vs_p, B, NSA_KV_HEADS), kv4(kw_p, B, NSA_KV_HEADS)[:, T - wp:], kv4(vw_p, B, NSA_KV_HEADS)[:, T - wp:],
            kv4(kb_p, B, MOBA_HEADS), kv4(vb_p, B, MOBA_HEADS),
            kv4(kc_s, DB, NSA_KV_HEADS), kv4(vc_s, DB, NSA_KV_HEADS), kv4(ks_s, DB, NSA_KV_HEADS),
            kv4(vs_s, DB, NSA_KV_HEADS), kw_all, vw_all,
            kv4(kb_s, DB, MOBA_HEADS), kv4(vb_s, DB, MOBA_HEADS))
```

```python
import functools

import numpy as np
import jax
import jax.numpy as jnp
from jax import lax
from jax.experimental import pallas as pl
from jax.experimental.pallas import tpu as pltpu

F32 = jnp.float32
BF16 = jnp.bfloat16

D_MODEL = 1024
PAGE_SIZE = 128
HEAD_DIM = 64
NSA_HEADS = 8
NSA_KV_HEADS = 2
NSA_GROUP = NSA_HEADS // NSA_KV_HEADS
CMP_BLOCK = 32
CMP_STRIDE = 16
CMP_HIDDEN = 256
SEL_BLOCK = 64
N_SEL = 8
WINDOW = 512
MOBA_HEADS = 8
MOBA_BLOCK = 256
MOBA_TOPK = 3
N_EXPERTS = 32
TOP_K = 4
D_FF = D_MODEL
SWIGLU_LIMIT = 7.0
SWIGLU_ALPHA = 1.702
ROPE_THETA = 10000.0
EPS = 1e-6
NEG = -1e30
POS_BIG = 1e30

LANES = 128
HALF = HEAD_DIM
ATT_SCALE = HEAD_DIM ** -0.5
Q_TILE = 128
SEL_CHUNK = 512
MOE_TILE = 256
VMEM_LIMIT = 56 * 1024 * 1024


def _cparams(*sem):
    return pltpu.CompilerParams(dimension_semantics=sem, vmem_limit_bytes=VMEM_LIMIT)


def _lane_iota(shape):
    return lax.broadcasted_iota(jnp.int32, shape, len(shape) - 1)


def _row_iota(shape):
    return lax.broadcasted_iota(jnp.int32, shape, 0)


def _dot(a, b):
    return jnp.dot(a, b, preferred_element_type=F32)


def _dot_nt(a, b):
    return lax.dot_general(a, b, (((1,), (1,)), ((), ())), preferred_element_type=F32)


def _split(a):
    hi = a.astype(BF16)
    lo = (a - hi.astype(F32)).astype(BF16)
    return hi, lo


def _split_dot(a, b):
    hi, lo = _split(a)
    return _dot(hi, b) + _dot(lo, b)


def _dot3(a, b, nt=False):
    mm = _dot_nt if nt else _dot
    a_hi, a_lo = _split(a)
    b_hi, b_lo = b if isinstance(b, tuple) else _split(b)
    return mm(a_hi, b_hi) + (mm(a_hi, b_lo) + mm(a_lo, b_hi))


def _mm(a, w, precise):
    return _dot3(a, w) if precise else _dot(a.astype(BF16), w)


def _masked_softmax(s, mask):
    s = jnp.where(mask, s, NEG)
    m = jnp.max(s, axis=-1, keepdims=True)
    e = jnp.where(mask, jnp.exp(s - m), 0.0)
    return e / jnp.maximum(jnp.sum(e, axis=-1, keepdims=True), 1e-30)


def _head_norm(y, gain, bd):
    ss = _split_dot(y * y, bd)
    return y * lax.rsqrt(ss * (1.0 / HEAD_DIM) + EPS) * gain


def _rope(y, cos, sin_signed):
    first = (_lane_iota(y.shape) % HEAD_DIM) < (HEAD_DIM // 2)
    partner = jnp.where(first, pltpu.roll(y, LANES - HEAD_DIM // 2, 1), pltpu.roll(y, HEAD_DIM // 2, 1))
    return y * cos + partner * sin_signed


def _select_top(score, n_pick, idx_f, axis):
    sel = jnp.zeros(score.shape, jnp.bool_)
    for _ in range(n_pick):
        m = jnp.max(score, axis=axis, keepdims=True)
        first = jnp.min(jnp.where(score == m, idx_f, 1e9), axis=axis, keepdims=True)
        hit = idx_f == first
        sel = jnp.logical_or(sel, hit)
        score = jnp.where(hit, -jnp.inf, score)
    return sel


def _inproj_kernel(precise, x_ref, cos_ref, sin_ref, cost_ref, sint_ref, gn_ref, gkc_row,
                   wqa, wkvc, wkvat, wgn, wqb, wkbt, wvbt, wgm, gqa, gqb, gk_col, bd_ref, *outs):
    (qa_o, kcr_o, vcr_o, kc_o, vc_o, ks_o, vs_o, kw_o, vw_o, ksb_o, vsb_o, kwb_o, vwb_o,
     gnsa_o, qb_o, kb_o, vb_o, kbb_o, vbb_o, gm_o) = outs
    x = x_ref[...]
    xn = x * lax.rsqrt(jnp.mean(x * x, axis=-1, keepdims=True) + EPS) * gn_ref[...]
    cos = cos_ref[...]
    sin = sin_ref[...]
    bd = bd_ref[...]
    xa = xn if precise else xn.astype(BF16)
    mm = functools.partial(_mm, xa, precise=precise)
    mm_t = (lambda wt: _dot3(wt, xn, nt=True)) if precise else (lambda wt: _dot_nt(wt, xa))
    half = HEAD_DIM // 2
    cos_t = cost_ref[...][None]
    sin_t = sint_ref[...][None]

    def tile(j):
        return slice(j * LANES, (j + 1) * LANES)

    def qk(w_ref, gain, j):
        return _rope(_head_norm(mm(w_ref[:, tile(j)]), gain, bd), cos, sin)

    def k_t(wt_ref, j, gain_col):
        y = mm_t(wt_ref[tile(j), :])
        y = y.reshape(2, HEAD_DIM, y.shape[-1])
        y = y * lax.rsqrt(jnp.mean(y * y, axis=1, keepdims=True) + EPS) * gain_col[None]
        x1, x2 = y[:, :half], y[:, half:]
        return jnp.concatenate([x1 * cos_t - x2 * sin_t, x2 * cos_t + x1 * sin_t], axis=1).reshape(LANES, -1)

    for j in range(4):
        qa_o[:, tile(j)] = qk(wqa, gqa[...], j).astype(qa_o.dtype)
    kcr_o[...] = qk(wkvc, gkc_row[...], 0)
    vcr_o[...] = mm(wkvc[:, tile(1)])
    nsa_kv = ((kc_o, vc_o, None, None), (ks_o, vs_o, ksb_o, vsb_o), (kw_o, vw_o, kwb_o, vwb_o))
    for i, (k_o, v_o, kb16_o, vb16_o) in enumerate(nsa_kv):
        k = k_t(wkvat, 2 * i, gk_col[:, i:i + 1])
        v = mm_t(wkvat[tile(2 * i + 1), :])
        k_o[0] = k
        v_o[0] = v
        if kb16_o is not None:
            kb16_o[0, 0] = k.astype(BF16)
            vb16_o[0, 0] = v.astype(BF16)
    gnsa_o[...] = jax.nn.sigmoid(mm(wgn[...]))
    for j in range(4):
        qb_o[:, tile(j)] = qk(wqb, gqb[...], j).astype(qb_o.dtype)
        k = k_t(wkbt, j, gk_col[:, 3:4])
        v = mm_t(wvbt[tile(j), :])
        kb_o[0, tile(j), :] = k
        vb_o[0, tile(j), :] = v
        kbb_o[0, 0, tile(j), :] = k.astype(BF16)
        vbb_o[0, 0, tile(j), :] = v.astype(BF16)
    for j in range(2 * D_MODEL // LANES):
        gm_o[:, tile(j)] = jax.nn.sigmoid(mm(wgm[:, tile(j)]))


_INPROJ_W = ('wqa', 'wkvc', 'wkvat', 'wgn', 'wqb', 'wkbt', 'wvbt', 'wgm')
_INPROJ_OUT = ('qa', 'kc_row', 'vc_row', 'kc', 'vc', 'ks', 'vs', 'kw', 'vw', 'ks_blk', 'vs_blk', 'kw_blk', 'vw_blk',
               'gn', 'qb', 'kb', 'vb', 'kb_blk', 'vb_blk', 'gm')


def _in_proj(x2d, n_batch, rope, prm, tm, precise):
    n = x2d.shape[0]
    t = n // n_batch
    n_pos = t // tm
    cos, sin, cos_t, sin_t = rope
    row = lambda i: (i, 0)
    const = lambda i: (0, 0)
    pos = lambda i: (i % n_pos, 0)
    pos_t = lambda i: (0, i % n_pos)
    tr = lambda i: (i // n_pos, 0, i % n_pos)
    blk = lambda i: (i // n_pos, i % n_pos, 0, 0)
    ws = [prm[k + ('_f32' if precise else '')] for k in _INPROJ_W] + [prm[k] for k in ('gqa', 'gqb', 'gk_col', 'bd')]
    in_specs = ([pl.BlockSpec((tm, D_MODEL), row), pl.BlockSpec((tm, LANES), pos), pl.BlockSpec((tm, LANES), pos),
                 pl.BlockSpec((HEAD_DIM // 2, tm), pos_t), pl.BlockSpec((HEAD_DIM // 2, tm), pos_t),
                 pl.BlockSpec((1, D_MODEL), const), pl.BlockSpec((1, LANES), const)]
                + [pl.BlockSpec(w.shape, const) for w in ws])
    qdt = F32 if precise else BF16
    rows_out = lambda w, dt: (jax.ShapeDtypeStruct((n, w), dt), pl.BlockSpec((tm, w), row))
    t_out = lambda w: (jax.ShapeDtypeStruct((n_batch, w, t), F32), pl.BlockSpec((1, w, tm), tr))
    b_out = lambda w: (jax.ShapeDtypeStruct((n_batch, n_pos, w, tm), BF16), pl.BlockSpec((1, 1, w, tm), blk))
    outs = ([rows_out(512, qdt), rows_out(LANES, F32), rows_out(LANES, F32)] + [t_out(LANES)] * 6 + [b_out(LANES)] * 4
            + [rows_out(LANES, F32), rows_out(512, qdt), t_out(512), t_out(512), b_out(512), b_out(512),
               rows_out(2 * D_MODEL, F32)])
    res = pl.pallas_call(
        functools.partial(_inproj_kernel, precise), grid=(n // tm,), in_specs=in_specs,
        out_specs=[o[1] for o in outs], out_shape=[o[0] for o in outs], compiler_params=_cparams("parallel"),
        name="in_proj_sample" if precise else "in_proj",
    )(x2d, cos, sin, cos_t, sin_t, prm['norm_attn'], prm['gka0_row'], *ws)
    return dict(zip(_INPROJ_OUT, res))


def _compress_hidden(xs, pos_a, pos_b, w1a, w1b, precise):
    lo = _lane_iota((LANES, LANES)) < HALF
    t0, t1 = [], []
    for l in range(0, CMP_STRIDE, 2):
        a, b = xs[l], xs[l + 1]
        t0.append(jnp.where(lo, a, pltpu.roll(b, HALF, 1)))
        t1.append(jnp.where(lo, pltpu.roll(a, HALF, 1), b))
    x = jnp.concatenate([jnp.concatenate(t0, axis=1), jnp.concatenate(t1, axis=1)], axis=0)
    first = _mm(x + pos_a, w1a, precise)
    second = _mm(x + pos_b, w1b, precise)
    n = LANES
    shifted = jnp.concatenate([pltpu.roll(second[:n], n - 1, 0), pltpu.roll(second[n:], n - 1, 0)], axis=0)
    return jax.nn.gelu(first + shifted)


_CW_NAMES = ('pka', 'pkb', 'k1a', 'k1b', 'k2g0', 'k2g1', 'pva', 'pvb', 'v1a', 'v1b', 'v2g0', 'v2g1', 'gkc', 'bd')


def _compress_prompt_kernel(k_ref, v_ref, *rest):
    cw = dict(zip(_CW_NAMES, rest[:len(_CW_NAMES)]))
    kc_o, vc_o = rest[len(_CW_NAMES):]
    n = k_ref.shape[1] // CMP_STRIDE

    def compress(ref, p):
        xs = [ref[0, pl.ds(l, n, stride=CMP_STRIDE), :] for l in range(CMP_STRIDE)]
        h = _compress_hidden(xs, cw['p%sa' % p][...], cw['p%sb' % p][...], cw['%s1a' % p][...], cw['%s1b' % p][...],
                             False).astype(BF16)
        return _dot(h[:LANES], cw['%s2g0' % p][...]) + _dot(h[LANES:], cw['%s2g1' % p][...])

    kc_o[0] = _head_norm(compress(k_ref, 'k'), cw['gkc'][...], cw['bd'][...])
    vc_o[0] = compress(v_ref, 'v')


def _compress_prompt(k_c, v_c, prm):
    b, t, _ = k_c.shape
    assert t // CMP_STRIDE == LANES
    ws = [prm[k] for k in _CW_NAMES]
    per_b = lambda i: (i, 0, 0)
    const = lambda i: (0, 0)
    return pl.pallas_call(
        _compress_prompt_kernel, grid=(b,),
        in_specs=[pl.BlockSpec((1, t, LANES), per_b)] * 2 + [pl.BlockSpec(w.shape, const) for w in ws],
        out_specs=[pl.BlockSpec((1, LANES, LANES), per_b)] * 2,
        out_shape=[jax.ShapeDtypeStruct((b, LANES, LANES), F32)] * 2,
        compiler_params=_cparams("parallel"), name="compress_prompt",
    )(k_c, v_c, *ws)


def _stack_nsa_q(q):
    lo = _lane_iota((q.shape[0], LANES)) < HALF
    zero = jnp.zeros((q.shape[0], LANES), q.dtype)
    tiles = [q[:, j * LANES:(j + 1) * LANES] for j in range(4)]
    parts = [jnp.where(lo, t, zero) for t in tiles] + [jnp.where(lo, zero, t) for t in tiles]
    return jnp.concatenate(parts, axis=0)


def _nsa_select(imp, blk, cur):
    forced = (blk == 0) | (blk == cur) | (blk == cur - 1)
    valid = blk <= cur
    score = jnp.where(forced, POS_BIG, jnp.where(valid, imp, -jnp.inf))
    sel = _select_top(score, N_SEL, blk.astype(F32), -1)
    return jnp.logical_and(sel, valid)


def _online_update(m_ref, l_ref, acc_ref, i, s, vt, first):
    m_cur = jnp.max(s, axis=-1, keepdims=True)
    if first:
        p = jnp.exp(s - m_cur)
        m_ref[i] = m_cur
        l_ref[i] = jnp.sum(p, axis=-1, keepdims=True)
        acc_ref[i] = _dot_nt(p.astype(BF16), vt)
    else:
        m_prev = m_ref[i]
        m_new = jnp.maximum(m_prev, m_cur)
        alpha = jnp.exp(m_prev - m_new)
        p = jnp.exp(s - m_new)
        m_ref[i] = m_new
        l_ref[i] = alpha * l_ref[i] + jnp.sum(p, axis=-1, keepdims=True)
        acc_ref[i] = alpha * acc_ref[i] + _dot_nt(p.astype(BF16), vt)


def _nsa_prompt_kernel(q_ref, g_ref, kc_ref, vc_ref, ks_ref, vs_ref, kw_ref, vw_ref, ov_ref,
                       o_ref, m_ref, l_ref, acc_ref):
    qt = Q_TILE
    s0 = pl.program_id(1) * qt
    q = _stack_nsa_q(q_ref[0]) * ATT_SCALE

    def t_of(shape):
        return s0 + (_row_iota(shape) & (qt - 1))

    n_c = kc_ref.shape[1] - 1
    s_c = _dot_nt(q, kc_ref[0].astype(BF16))
    c = _lane_iota(s_c.shape)
    p_c = _masked_softmax(s_c, (c * CMP_STRIDE + CMP_BLOCK - 1 <= t_of(s_c.shape)) & (c < n_c))
    o_c = _dot(p_c.astype(BF16), vc_ref[0].astype(BF16))
    g4 = NSA_GROUP * qt
    imp = jnp.concatenate([sum(p_c[g * g4 + r * qt: g * g4 + (r + 1) * qt] for r in range(NSA_GROUP))
                           for g in range(NSA_KV_HEADS)], axis=0)
    imp = _split_dot(imp, ov_ref[...])
    blk = _lane_iota(imp.shape)
    sel = _nsa_select(imp, blk, t_of(imp.shape) // SEL_BLOCK)
    bias = jnp.where(sel, 0.0, NEG).astype(BF16)
    bias = jnp.concatenate([bias[:qt]] * NSA_GROUP + [bias[qt:]] * NSA_GROUP, axis=0)
    qaug = jnp.concatenate([q, bias], axis=1)
    kb = ks_ref.shape[3]
    per_kb = kb // SEL_BLOCK

    def sel_block(c, causal, first):
        j = _row_iota((LANES, kb))
        key = _lane_iota((LANES, kb))
        onehot = jnp.where(j - c * per_kb == key // SEL_BLOCK, 1.0, 0.0).astype(BF16)
        s = _dot(qaug, jnp.concatenate([ks_ref[0, c], onehot], axis=0))
        if causal:
            s = jnp.where(c * kb + _lane_iota(s.shape) <= t_of(s.shape), s, NEG)
        _online_update(m_ref, l_ref, acc_ref, 0, s, vs_ref[0, c], first)

    c_last = s0 // kb
    sel_block(0, True, True)

    def body(c, carry):
        sel_block(c, False, False)
        return carry

    lax.fori_loop(1, c_last, body, 0)

    @pl.when(c_last > 0)
    def _():
        sel_block(c_last, True, False)

    o_s = acc_ref[0] / jnp.maximum(l_ref[0], 1e-30)

    w0 = s0 // kb - (WINDOW // kb)
    n_wb = WINDOW // kb + 1
    idx = [jnp.maximum(w0 + j, 0) for j in range(n_wb)]
    s_w = jnp.concatenate([_dot(q, kw_ref[0, i]) for i in idx], axis=1)
    wp = w0 * kb + _lane_iota(s_w.shape)
    t_w = t_of(s_w.shape)
    p_w = _masked_softmax(s_w, (wp >= 0) & (wp <= t_w) & (wp > t_w - WINDOW)).astype(BF16)
    o_w = sum(_dot_nt(p_w[:, j * kb:(j + 1) * kb], vw_ref[0, i]) for j, i in enumerate(idx))

    g = g_ref[0]
    lo = _lane_iota((qt, LANES)) < HALF

    def head_out(h):
        r = slice(h * qt, (h + 1) * qt)
        gate = lambda j: g[:, j * NSA_HEADS + h: j * NSA_HEADS + h + 1]
        return gate(0) * o_c[r] + gate(1) * o_s[r] + gate(2) * o_w[r]

    for j in range(4):
        o_ref[0, :, j * LANES:(j + 1) * LANES] = jnp.where(lo, head_out(j), head_out(4 + j)).astype(BF16)


def _nsa_prompt(q_a, g_nsa, kc, vc, ks_blk, vs_blk, kw_blk, vw_blk, prm):
    b, t, _ = q_a.shape
    qt = Q_TILE
    kb = ks_blk.shape[3]
    assert kb % qt == 0 and WINDOW % kb == 0 and kb % SEL_BLOCK == 0 and t // SEL_BLOCK <= LANES
    rows = NSA_HEADS * qt
    per_q = lambda i, j: (i, j, 0)
    per_b = lambda i, j: (i, 0, 0)
    return pl.pallas_call(
        _nsa_prompt_kernel, grid=(b, t // qt),
        in_specs=[pl.BlockSpec((1, qt, 512), per_q), pl.BlockSpec((1, qt, LANES), per_q),
                  pl.BlockSpec((1, LANES, LANES), per_b), pl.BlockSpec((1, LANES, LANES), per_b)]
        + [pl.BlockSpec((1,) + ks_blk.shape[1:], lambda i, j: (i, 0, 0, 0))] * 4
        + [pl.BlockSpec(prm['ov'].shape, lambda i, j: (0, 0))],
        out_specs=pl.BlockSpec((1, qt, 512), per_q),
        out_shape=jax.ShapeDtypeStruct((b, t, 512), BF16),
        scratch_shapes=[pltpu.VMEM((1, rows, 1), F32), pltpu.VMEM((1, rows, 1), F32),
                        pltpu.VMEM((1, rows, LANES), F32)],
        compiler_params=_cparams("arbitrary", "arbitrary"), name="nsa_prompt",
    )(q_a, g_nsa, kc, vc, ks_blk, vs_blk, kw_blk, vw_blk, prm['ov'])


def _moba_prompt_kernel(q_ref, kt_ref, kb_ref, vb_ref, o_ref, kmean_ref, qaug_ref, m_ref, l_ref, acc_ref):
    qt = MOBA_BLOCK
    cur = pl.program_id(1)
    n_b = kb_ref.shape[1]
    n_pair = MOBA_HEADS // 2

    @pl.when(cur == 0)
    def _():
        lane = _lane_iota(kmean_ref.shape)
        km = jnp.zeros(kmean_ref.shape, F32)
        for n in range(n_b):
            km = jnp.where(lane == n, jnp.mean(kt_ref[0, :, n * qt:(n + 1) * qt], axis=-1, keepdims=True), km)
        kmean_ref[...] = km

    lo = _lane_iota((qt, LANES)) < HALF
    causal = _lane_iota((2 * qt, qt)) <= (_row_iota((2 * qt, qt)) & (qt - 1))
    for j in range(n_pair):
        tl = slice(j * LANES, (j + 1) * LANES)
        qj = q_ref[0, :, tl]
        zero = jnp.zeros_like(qj)
        qp = jnp.concatenate([jnp.where(lo, qj, zero), jnp.where(lo, zero, qj)], axis=0)
        km_hi, km_lo = _split(kmean_ref[tl, :])
        s_g = _dot(qp, km_hi) + _dot(qp, km_lo)
        blk = _lane_iota(s_g.shape)
        past = blk < cur
        sel = _select_top(jnp.where(past, s_g, -jnp.inf), MOBA_TOPK, blk.astype(F32), -1)
        bias = jnp.where(jnp.logical_and(sel, past), 0.0, NEG).astype(BF16)
        qs = qp * ATT_SCALE
        qaug_ref[j] = jnp.concatenate([qs, bias], axis=1)
        s = jnp.where(causal, _dot(qs, kb_ref[0, cur, tl, :]), NEG)
        _online_update(m_ref, l_ref, acc_ref, j, s, vb_ref[0, cur, tl, :], True)

    def body(n, carry):
        onehot = jnp.where(_row_iota((LANES, qt)) == n, 1.0, 0.0).astype(BF16)
        for j in range(n_pair):
            tl = slice(j * LANES, (j + 1) * LANES)
            s = _dot(qaug_ref[j], jnp.concatenate([kb_ref[0, n, tl, :], onehot], axis=0))
            _online_update(m_ref, l_ref, acc_ref, j, s, vb_ref[0, n, tl, :], False)
        return carry

    lax.fori_loop(0, cur, body, 0)
    for j in range(n_pair):
        o = acc_ref[j] / jnp.maximum(l_ref[j], 1e-30)
        o_ref[0, :, j * LANES:(j + 1) * LANES] = jnp.where(lo, o[:qt], o[qt:]).astype(BF16)


def _moba_prompt(q_b, kt, kb_blk, vb_blk):
    b, t, w = q_b.shape
    qt = MOBA_BLOCK
    assert kb_blk.shape[3] == qt and t // qt <= LANES
    per_q = lambda i, j: (i, j, 0)
    per_b = lambda i, j: (i, 0, 0)
    per_b4 = lambda i, j: (i, 0, 0, 0)
    n_pair = MOBA_HEADS // 2
    return pl.pallas_call(
        _moba_prompt_kernel, grid=(b, t // qt),
        in_specs=[pl.BlockSpec((1, qt, w), per_q), pl.BlockSpec((1, w, t), per_b),
                  pl.BlockSpec((1,) + kb_blk.shape[1:], per_b4), pl.BlockSpec((1,) + vb_blk.shape[1:], per_b4)],
        out_specs=pl.BlockSpec((1, qt, w), per_q),
        out_shape=jax.ShapeDtypeStruct((b, t, w), BF16),
        scratch_shapes=[pltpu.VMEM((w, LANES), F32), pltpu.VMEM((n_pair, 2 * qt, 2 * LANES), BF16),
                        pltpu.VMEM((n_pair, 2 * qt, 1), F32), pltpu.VMEM((n_pair, 2 * qt, 1), F32),
                        pltpu.VMEM((n_pair, 2 * qt, LANES), F32)],
        compiler_params=_cparams("arbitrary", "arbitrary"), name="moba_prompt",
    )(q_b, kt, kb_blk, vb_blk)


def _merge_kernel(precise, x_ref, oa_ref, ob_ref, gm_ref, wa_ref, wb_ref, wo_ref, nf_ref, wr_hi, wr_lo, br_ref,
                  h_o, xn_o, lg_o):
    gm = gm_ref[...]
    u = (gm[:, :D_MODEL] * _mm(oa_ref[...], wa_ref[...], precise)
         + gm[:, D_MODEL:] * _mm(ob_ref[...], wb_ref[...], precise))
    h = x_ref[...] + _mm(u, wo_ref[...], precise)
    h_o[...] = h
    xn = h * lax.rsqrt(jnp.mean(h * h, axis=-1, keepdims=True) + EPS) * nf_ref[...]
    xn_o[...] = xn.astype(BF16)
    lg_o[...] = _dot3(xn, (wr_hi[...], wr_lo[...])) + br_ref[...]


def _merge(x2d, o_a, o_b, g_mrg, prm, tm, precise):
    n = x2d.shape[0]
    row = lambda i: (i, 0)
    const = lambda i: (0, 0)
    sfx = '_f32' if precise else ''
    ws = [prm[k] for k in ('w_up_a' + sfx, 'w_up_b' + sfx, 'w_out' + sfx, 'norm_ffn', 'wr_hi', 'wr_lo', 'b_router')]
    return pl.pallas_call(
        functools.partial(_merge_kernel, precise), grid=(n // tm,),
        in_specs=[pl.BlockSpec((tm, D_MODEL), row), pl.BlockSpec((tm, 512), row), pl.BlockSpec((tm, 512), row),
                  pl.BlockSpec((tm, 2 * D_MODEL), row)] + [pl.BlockSpec(w.shape, const) for w in ws],
        out_specs=[pl.BlockSpec((tm, D_MODEL), row), pl.BlockSpec((tm, D_MODEL), row), pl.BlockSpec((tm, LANES), row)],
        out_shape=[jax.ShapeDtypeStruct((n, D_MODEL), F32), jax.ShapeDtypeStruct((n, D_MODEL), BF16),
                   jax.ShapeDtypeStruct((n, LANES), F32)],
        compiler_params=_cparams("parallel"), name="merge_router_sample" if precise else "merge_router",
    )(x2d, o_a, o_b, g_mrg, *ws)


def _moe_kernel(be_ref, bv_ref, x_ref, g_ref, wgu_ref, bgu_ref, wd_ref, bd_ref, y_ref):
    i = pl.program_id(0)

    @pl.when(bv_ref[i] > 0)
    def _():
        h = _dot(x_ref[...], wgu_ref[0]) + bgu_ref[0]
        gate = jnp.minimum(h[:, :D_FF], SWIGLU_LIMIT)
        up = jnp.clip(h[:, D_FF:], -SWIGLU_LIMIT, SWIGLU_LIMIT)
        act = (up + 1.0) * gate * jax.nn.sigmoid(SWIGLU_ALPHA * gate)
        y = _dot(act.astype(BF16), wd_ref[0]) + bd_ref[0]
        y_ref[...] = y * g_ref[...]

    @pl.when(bv_ref[i] == 0)
    def _():
        y_ref[...] = jnp.zeros(y_ref.shape, F32)


def _moe_blocks(block_e, block_valid, x_sorted, slot_gate, w_gu, b_gu, w_down, b_down):
    cap = x_sorted.shape[0]
    tm = MOE_TILE
    row = lambda i, be, bv: (i, 0)
    exp = lambda i, be, bv: (be[i], 0, 0)
    grid_spec = pltpu.PrefetchScalarGridSpec(
        num_scalar_prefetch=2, grid=(cap // tm,),
        in_specs=[pl.BlockSpec((tm, D_MODEL), row), pl.BlockSpec((tm, 1), row),
                  pl.BlockSpec((1, D_MODEL, 2 * D_FF), exp), pl.BlockSpec((1, 1, 2 * D_FF), exp),
                  pl.BlockSpec((1, D_FF, D_MODEL), exp), pl.BlockSpec((1, 1, D_MODEL), exp)],
        out_specs=pl.BlockSpec((tm, D_MODEL), row))
    return pl.pallas_call(
        _moe_kernel, grid_spec=grid_spec, out_shape=jax.ShapeDtypeStruct((cap, D_MODEL), F32),
        compiler_params=_cparams("arbitrary"), name="moe_experts",
    )(block_e, block_valid, x_sorted, slot_gate, w_gu, b_gu, w_down, b_down)


def _route(logits):
    n = logits.shape[0]
    top_v, top_e = lax.top_k(logits, TOP_K)
    gates = jax.nn.softmax(top_v, axis=-1)
    onehot = jnp.sum((top_e[:, :, None] == jnp.arange(N_EXPERTS)[None, None, :]).astype(jnp.int32), axis=1)
    csum = jnp.cumsum(onehot, axis=0)
    counts = csum[-1]
    padded = (counts + MOE_TILE - 1) // MOE_TILE * MOE_TILE
    pend = jnp.cumsum(padded)
    rank = jnp.take_along_axis(csum - onehot, top_e, axis=1)
    slot = (pend - padded)[top_e] + rank
    n_blocks = -(-(n * TOP_K) // MOE_TILE) + N_EXPERTS
    cap = n_blocks * MOE_TILE
    starts = jnp.arange(n_blocks, dtype=jnp.int32) * MOE_TILE
    block_e = jnp.minimum(jnp.sum((pend[None, :] <= starts[:, None]).astype(jnp.int32), axis=1), N_EXPERTS - 1)
    block_valid = (starts < pend[-1]).astype(jnp.int32)
    tok = jnp.broadcast_to(jnp.arange(n, dtype=jnp.int32)[:, None], (n, TOP_K))
    slot_tok = jnp.zeros((cap,), jnp.int32).at[slot.reshape(-1)].set(tok.reshape(-1))
    slot_gate = jnp.zeros((cap,), F32).at[slot.reshape(-1)].set(gates.reshape(-1))
    return slot, slot_tok, slot_gate.reshape(cap, 1), block_e, block_valid


def _column(ref, b):
    x = ref[...]
    return jnp.sum(jnp.where(_lane_iota(x.shape) == b, x, 0.0), axis=-1, keepdims=True)


def _fold_lanes(x):
    return sum(x[..., i * LANES:(i + 1) * LANES] for i in range(x.shape[-1] // LANES))


def _gqa_scores(q3, kt):
    g = NSA_GROUP
    return jnp.concatenate([jnp.sum(q3[i * g:(i + 1) * g] * kt[i * HEAD_DIM:(i + 1) * HEAD_DIM][None], axis=1)
                            for i in range(NSA_KV_HEADS)], axis=0)


def _gqa_weighted(p, vt):
    g = NSA_GROUP
    return jnp.concatenate([p[i * g:(i + 1) * g][:, None, :] * vt[i * HEAD_DIM:(i + 1) * HEAD_DIM][None]
                            for i in range(NSA_KV_HEADS)], axis=0)


_CWS_NAMES = ('pka', 'pkb', 'k1a_hi', 'k1a_lo', 'k1b_hi', 'k1b_lo', 'k2t',
              'pva', 'pvb', 'v1a_hi', 'v1a_lo', 'v1b_hi', 'v1b_lo', 'v2t', 'gkc_col')


def _nsa_sample_kernel(pt_ref, qt_ref, g_ref, ksn_ref, vsn_ref, kwn_ref, vwn_ref, wk_ref, wv_ref, *rest):
    n_pg = (len(rest) - len(_CWS_NAMES) - 3) // 4
    ck_pages = rest[0:n_pg]
    cv_pages = rest[n_pg:2 * n_pg]
    sk_pages = rest[2 * n_pg:3 * n_pg]
    sv_pages = rest[3 * n_pg:4 * n_pg]
    cw = dict(zip(_CWS_NAMES, rest[4 * n_pg:4 * n_pg + len(_CWS_NAMES)]))
    ov_ref, e_ref, o_ref = rest[4 * n_pg + len(_CWS_NAMES):]
    b = pl.program_id(0)
    past = n_pg * PAGE_SIZE
    per_pg = PAGE_SIZE // CMP_STRIDE

    def compress_t(pages, p):
        xs = [jnp.concatenate([pg[0, pl.ds(l, per_pg, stride=CMP_STRIDE), :] for pg in pages], axis=0)
              for l in range(CMP_STRIDE)]
        h = _compress_hidden(xs, cw['p%sa' % p][...], cw['p%sb' % p][...],
                             (cw['%s1a_hi' % p][...], cw['%s1a_lo' % p][...]),
                             (cw['%s1b_hi' % p][...], cw['%s1b_lo' % p][...]), True)
        w2t = cw['%s2t' % p][...]
        return jnp.concatenate([_dot3(w2t, h[:LANES], nt=True), _dot3(w2t, h[LANES:], nt=True)], axis=0)

    kct = compress_t(ck_pages, 'k').reshape(NSA_KV_HEADS, HEAD_DIM, LANES)
    ss = jnp.sum(kct * kct, axis=1, keepdims=True)
    kct = (kct * lax.rsqrt(ss * (1.0 / HEAD_DIM) + EPS) * cw['gkc_col'][...][None]).reshape(LANES, LANES)
    vct = compress_t(cv_pages, 'v')

    q3 = (_column(qt_ref, b) * ATT_SCALE).reshape(NSA_HEADS, HEAD_DIM, 1)

    n_c = (past + 1 - CMP_BLOCK) // CMP_STRIDE + 1
    s_c = _gqa_scores(q3, kct)
    p_c = _masked_softmax(s_c, _lane_iota(s_c.shape) < n_c)
    o_c = jnp.sum(_gqa_weighted(p_c, vct), axis=-1, keepdims=True)
    row = _row_iota((NSA_HEADS, LANES))
    i0 = jnp.sum(p_c[:NSA_GROUP], axis=0, keepdims=True)
    i1 = jnp.sum(p_c[NSA_GROUP:], axis=0, keepdims=True)
    imp = _split_dot(jnp.where(row < NSA_GROUP, i0, i1), ov_ref[...])
    sel = _nsa_select(imp, _lane_iota(imp.shape), past // SEL_BLOCK)
    picked = _dot(jnp.where(sel, 1.0, 0.0).astype(BF16), e_ref[...]) > 0.5

    def attend_with_new(s, valid, kn_ref, vn_ref, weighted):
        kn = _column(kn_ref, b)
        vn = _column(vn_ref, b)
        s = jnp.where(valid, s, NEG)
        s_n = _gqa_scores(q3, kn)
        m = jnp.maximum(jnp.max(s, axis=-1, keepdims=True), s_n)
        p = jnp.where(valid, jnp.exp(s - m), 0.0)
        p_n = jnp.exp(s_n - m)
        num = jnp.sum(weighted(p), axis=-1, keepdims=True) + _gqa_weighted(p_n, vn)
        den = jnp.maximum(jnp.sum(p, axis=-1, keepdims=True) + p_n, 1e-30)
        return num / den[:, :, None]

    s_s = jnp.concatenate([_gqa_scores(q3, pg[0]) for pg in sk_pages], axis=1)
    o_s = attend_with_new(
        s_s, picked, ksn_ref, vsn_ref,
        lambda p: sum(_gqa_weighted(p[:, i * PAGE_SIZE:(i + 1) * PAGE_SIZE], pg[0]) for i, pg in enumerate(sv_pages)))

    wb = wk_ref.shape[2]
    s_w = _gqa_scores(q3, wk_ref[0])
    inside = (past - wb + _lane_iota(s_w.shape)) > (past - WINDOW)
    o_w = attend_with_new(s_w, inside, kwn_ref, vwn_ref, lambda p: _fold_lanes(_gqa_weighted(p, wv_ref[0])))

    g = jnp.broadcast_to(g_ref[0], (NSA_HEADS, LANES))
    lane = _lane_iota(g.shape)
    gate = lambda j: jnp.sum(jnp.where(lane == j * NSA_HEADS + row, g, 0.0), axis=-1, keepdims=True)[:, :, None]
    o = (gate(0) * o_c + gate(1) * o_s + gate(2) * o_w).reshape(NSA_HEADS * HEAD_DIM, 1)

    @pl.when(b == 0)
    def _():
        o_ref[...] = jnp.zeros(o_ref.shape, F32)

    o_ref[...] = jnp.where(_lane_iota(o_ref.shape) == b, o, o_ref[...])


def _nsa_sample(page_table, qt, g_nsa, ksn, vsn, kwn, vwn, win_kt, win_vt, cache_ck, cache_cv, cache_skt, cache_svt,
                prm):
    db, n_pg = page_table.shape
    wb = win_kt.shape[2]
    per_b = lambda i, pt: (i, 0, 0)
    const2 = lambda i, pt: (0, 0)

    def page_spec(p):
        return pl.BlockSpec((1, PAGE_SIZE, LANES), lambda i, pt, p=p: (pt[i, p], 0, 0))

    cws = [prm['s_' + k] for k in _CWS_NAMES]
    full = lambda a: pl.BlockSpec(a.shape, const2)
    in_specs = ([full(qt), pl.BlockSpec((1, 1, LANES), per_b), full(ksn), full(vsn), full(kwn), full(vwn),
                 pl.BlockSpec((1, LANES, wb), per_b), pl.BlockSpec((1, LANES, wb), per_b)]
                + [page_spec(p) for p in range(n_pg)] * 4
                + [full(w) for w in cws] + [full(prm['ov']), full(prm['e_key'])])
    grid_spec = pltpu.PrefetchScalarGridSpec(num_scalar_prefetch=1, grid=(db,), in_specs=in_specs,
                                             out_specs=pl.BlockSpec((NSA_HEADS * HEAD_DIM, db), const2))
    pages = [cache_ck] * n_pg + [cache_cv] * n_pg + [cache_skt] * n_pg + [cache_svt] * n_pg
    return pl.pallas_call(
        _nsa_sample_kernel, grid_spec=grid_spec, out_shape=jax.ShapeDtypeStruct((NSA_HEADS * HEAD_DIM, db), F32),
        compiler_params=_cparams("arbitrary"), name="nsa_sample",
    )(page_table, qt, g_nsa, ksn, vsn, kwn, vwn, win_kt, win_vt, *pages, *cws, prm['ov'], prm['e_key'])


def _moba_sample_kernel(pt_ref, qt_ref, kn_ref, vn_ref, *rest):
    n_pg = (len(rest) - 1) // 2
    k_pages = rest[:n_pg]
    v_pages = rest[n_pg:2 * n_pg]
    o_ref = rest[2 * n_pg]
    b = pl.program_id(0)
    pages_per_blk = MOBA_BLOCK // PAGE_SIZE
    n_b = n_pg // pages_per_blk
    heads3 = lambda x: x.reshape(MOBA_HEADS, HEAD_DIM, x.shape[-1])

    q3 = heads3(_column(qt_ref, b) * ATT_SCALE)
    kn3 = heads3(_column(kn_ref, b))
    vn3 = heads3(_column(vn_ref, b))
    s_pages = [jnp.sum(heads3(pg[0]) * q3, axis=1) for pg in k_pages]
    s_new = jnp.sum(q3 * kn3, axis=1)

    lane = _lane_iota((MOBA_HEADS, LANES))
    s_g = jnp.full((MOBA_HEADS, LANES), -jnp.inf, F32)
    for n in range(n_b):
        tot = sum(s_pages[n * pages_per_blk + i] for i in range(pages_per_blk))
        s_g = jnp.where(lane == n, jnp.sum(tot, axis=-1, keepdims=True), s_g)
    sel = _select_top(s_g, min(MOBA_TOPK, n_b), lane.astype(F32), -1)
    sel_blk = [jnp.sum(jnp.where(jnp.logical_and(sel, lane == n), 1.0, 0.0), axis=-1, keepdims=True) > 0.5
               for n in range(n_b)]

    m = s_new
    for i in range(n_pg):
        m = jnp.maximum(m, jnp.max(jnp.where(sel_blk[i // pages_per_blk], s_pages[i], NEG), axis=-1, keepdims=True))
    p_new = jnp.exp(s_new - m)
    l = p_new
    acc = jnp.zeros((MOBA_HEADS, HEAD_DIM, PAGE_SIZE), F32)
    for i in range(n_pg):
        p = jnp.where(sel_blk[i // pages_per_blk], jnp.exp(s_pages[i] - m), 0.0)
        l = l + jnp.sum(p, axis=-1, keepdims=True)
        acc = acc + heads3(v_pages[i][0]) * p[:, None, :]
    o = (jnp.sum(acc, axis=-1, keepdims=True) + p_new[:, :, None] * vn3) / jnp.maximum(l, 1e-30)[:, :, None]

    @pl.when(b == 0)
    def _():
        o_ref[...] = jnp.zeros(o_ref.shape, F32)

    o_ref[...] = jnp.where(_lane_iota(o_ref.shape) == b, o.reshape(MOBA_HEADS * HEAD_DIM, 1), o_ref[...])


def _moba_sample(page_table, qt, knt, vnt, cache_kt, cache_vt):
    db, n_pg = page_table.shape
    w = qt.shape[0]
    assert (n_pg * PAGE_SIZE) % MOBA_BLOCK == 0
    const2 = lambda i, pt: (0, 0)

    def page_spec(p):
        return pl.BlockSpec((1, w, PAGE_SIZE), lambda i, pt, p=p: (pt[i, p], 0, 0))

    in_specs = [pl.BlockSpec((w, db), const2)] * 3 + [page_spec(p) for p in range(n_pg)] * 2
    grid_spec = pltpu.PrefetchScalarGridSpec(num_scalar_prefetch=1, grid=(db,), in_specs=in_specs,
                                             out_specs=pl.BlockSpec((w, db), const2))
    return pl.pallas_call(
        _moba_sample_kernel, grid_spec=grid_spec, out_shape=jax.ShapeDtypeStruct((w, db), F32),
        compiler_params=_cparams("arbitrary"), name="moba_sample",
    )(page_table, qt, knt, vnt, *([cache_kt] * n_pg), *([cache_vt] * n_pg))


def _split_weight(w):
    c = w * (2.0 ** 16 + 1.0)
    hi = c - (c - w)
    return hi.astype(BF16), (w - hi).astype(BF16)


def _nsa_head_order():
    return np.concatenate([np.r_[j * HEAD_DIM:(j + 1) * HEAD_DIM, (4 + j) * HEAD_DIM:(5 + j) * HEAD_DIM]
                           for j in range(4)])


def _prepare(norm_attn, w_in, g_qa, g_ka, g_kc, cmp_pos_k, cmp_w1_k, cmp_w2_k, cmp_pos_v, cmp_w1_v, cmp_w2_v,
             g_qb, g_kb, w_up_a, w_up_b, w_out, norm_ffn, w_router, b_router):
    qa = NSA_HEADS * HEAD_DIM
    kva = NSA_KV_HEADS * HEAD_DIM
    qb = MOBA_HEADS * HEAD_DIM
    widths = [qa, kva, kva, kva, kva, kva, kva, 3 * NSA_HEADS, qb, qb, qb, 2 * D_MODEL]
    cuts = np.cumsum([0] + widths)
    part = lambda i: w_in[:, cuts[i]:cuts[i + 1]]
    order = _nsa_head_order()
    gcols = np.array([h * 3 + j for j in range(3) for h in range(NSA_HEADS)])
    wgn = jnp.zeros((D_MODEL, LANES), F32).at[:, :3 * NSA_HEADS].set(part(7)[:, gcols])
    two = lambda g: jnp.concatenate([g, g]).reshape(1, LANES)
    lane = np.arange(LANES)
    c_idx = np.arange(LANES)[:, None] * CMP_STRIDE
    s_idx = np.arange(LANES)[None, :] * SEL_BLOCK
    ov = ((c_idx < s_idx + SEL_BLOCK) & (c_idx + CMP_BLOCK > s_idx) & (np.arange(LANES)[:, None] < LANES - 1))
    half_w = CMP_STRIDE * HEAD_DIM

    def w2wide(w2):
        z = jnp.zeros_like(w2)
        return jnp.concatenate([w2, z], axis=1).astype(BF16), jnp.concatenate([z, w2], axis=1).astype(BF16)

    k2g0, k2g1 = w2wide(cmp_w2_k)
    v2g0, v2g1 = w2wide(cmp_w2_v)
    wr_hi, wr_lo = _split_weight(jnp.zeros((D_MODEL, LANES), F32).at[:, :N_EXPERTS].set(w_router))
    prm = {
        'norm_attn': norm_attn.reshape(1, D_MODEL),
        'wqa_f32': part(0)[:, order], 'wkvc_f32': w_in[:, cuts[1]:cuts[3]], 'wkvat_f32': w_in[:, cuts[1]:cuts[7]].T,
        'wgn_f32': wgn, 'wqb_f32': part(8), 'wkbt_f32': part(9).T, 'wvbt_f32': part(10).T, 'wgm_f32': part(11),
        'gqa': two(g_qa), 'gka0_row': two(g_ka[0]), 'gqb': two(g_qb), 'gkc': two(g_kc),
        'gk_col': jnp.stack([g_ka[0], g_ka[1], g_ka[2], g_kb], axis=1),
        'bd': jnp.asarray((lane[:, None] // HEAD_DIM) == (lane[None, :] // HEAD_DIM), BF16),
        'pka': cmp_pos_k[:CMP_STRIDE].reshape(1, half_w), 'pkb': cmp_pos_k[CMP_STRIDE:].reshape(1, half_w),
        'pva': cmp_pos_v[:CMP_STRIDE].reshape(1, half_w), 'pvb': cmp_pos_v[CMP_STRIDE:].reshape(1, half_w),
        'k1a': cmp_w1_k[:half_w].astype(BF16), 'k1b': cmp_w1_k[half_w:].astype(BF16),
        'v1a': cmp_w1_v[:half_w].astype(BF16), 'v1b': cmp_w1_v[half_w:].astype(BF16),
        'k2g0': k2g0, 'k2g1': k2g1, 'v2g0': v2g0, 'v2g1': v2g1,
        'ov': jnp.asarray(ov, BF16),
        'w_up_a_f32': w_up_a[order], 'w_up_b_f32': w_up_b, 'w_out_f32': w_out,
        'norm_ffn': norm_ffn.reshape(1, D_MODEL),
        'wr_hi': wr_hi, 'wr_lo': wr_lo,
        'b_router': jnp.zeros((1, LANES), F32).at[0, :N_EXPERTS].set(b_router),
    }
    for k in _INPROJ_W + ('w_up_a', 'w_up_b', 'w_out'):
        prm[k] = prm[k + '_f32'].astype(BF16)
    for p, w1, w2 in (('k', cmp_w1_k, cmp_w2_k), ('v', cmp_w1_v, cmp_w2_v)):
        for half, w in (('a', w1[:half_w]), ('b', w1[half_w:])):
            prm['s_%s1%s_hi' % (p, half)], prm['s_%s1%s_lo' % (p, half)] = _split_weight(w)
        prm['s_%s2t' % p] = w2.T
        prm['s_p%sa' % p] = prm['p%sa' % p]
        prm['s_p%sb' % p] = prm['p%sb' % p]
    prm['s_gkc_col'] = g_kc.reshape(HEAD_DIM, 1)
    return prm


def _rope_tables(pos):
    half = HEAD_DIM // 2
    inv = ROPE_THETA ** (-np.arange(half, dtype=np.float64) / half)
    ang = np.asarray(pos, np.float64)[:, None] * inv[None, :]
    cos, sin = np.cos(ang), np.sin(ang)
    cos_row = np.concatenate([cos, cos, cos, cos], axis=1)
    sin_row = np.concatenate([-sin, sin, -sin, sin], axis=1)
    return tuple(jnp.asarray(a, F32) for a in (cos_row, sin_row, cos.T, sin.T))


def _expand_matrix(n_keys, block, chunk=None):
    e = (np.arange(n_keys)[None, :] // block) == np.arange(LANES)[:, None]
    if chunk is not None:
        e = e.reshape(LANES, n_keys // chunk, chunk).transpose(1, 0, 2)
    return jnp.asarray(e, BF16)


def kernel(x_prompt, x_sample, cache_nsa_cmp_k, cache_nsa_cmp_v, cache_nsa_sel_k, cache_nsa_sel_v, cache_nsa_win_k, cache_nsa_win_v, cache_moba_k, cache_moba_v, page_table, norm_attn, w_in, g_qa, g_ka, g_kc, cmp_pos_k, cmp_w1_k, cmp_w2_k, cmp_pos_v, cmp_w1_v, cmp_w2_v, g_qb, g_kb, w_up_a, w_up_b, w_out, norm_ffn, w_router, b_router, w_gu, b_gu, w_down, b_down):
    B, T, _ = x_prompt.shape
    DB, DS, _ = x_sample.shape
    assert DS == 1
    n_pg = page_table.shape[1]
    past = n_pg * PAGE_SIZE
    n_pool = cache_nsa_cmp_k.shape[0]
    wb = cache_nsa_win_k.shape[1]

    prm = _prepare(norm_attn, w_in, g_qa, g_ka, g_kc, cmp_pos_k, cmp_w1_k, cmp_w2_k, cmp_pos_v, cmp_w1_v, cmp_w2_v,
                   g_qb, g_kb, w_up_a, w_up_b, w_out, norm_ffn, w_router, b_router)
    prm['e_key'] = _expand_matrix(past, SEL_BLOCK)
    order = _nsa_head_order()

    P = _in_proj(x_prompt.reshape(B * T, D_MODEL), B, _rope_tables(np.arange(T)), prm, MOBA_BLOCK, False)
    r3 = lambda a: a.reshape(B, T, a.shape[-1])
    kcc_p, vcc_p = _compress_prompt(r3(P['kc_row']), r3(P['vc_row']), prm)
    oa_p = _nsa_prompt(r3(P['qa']), r3(P['gn']), kcc_p, vcc_p, P['ks_blk'], P['vs_blk'], P['kw_blk'], P['vw_blk'], prm)
    ob_p = _moba_prompt(r3(P['qb']), P['kb'], P['kb_blk'], P['vb_blk'])
    h_p, xn_p, lg_p = _merge(x_prompt.reshape(B * T, D_MODEL), oa_p.reshape(B * T, 512), ob_p.reshape(B * T, 512),
                             P['gm'], prm, 256, False)

    S = _in_proj(x_sample.reshape(DB, D_MODEL), 1, _rope_tables(np.full((DB,), past)), prm, DB, True)
    rows2 = lambda c: c.reshape(c.shape[0], c.shape[1], c.shape[2] * c.shape[3])
    cols2 = lambda c: jnp.transpose(c, (0, 2, 3, 1)).reshape(c.shape[0], c.shape[2] * c.shape[3], c.shape[1])
    oa_t = _nsa_sample(page_table, S['qa'][:, np.argsort(order)].T, S['gn'].reshape(DB, 1, LANES),
                       S['ks'][0], S['vs'][0], S['kw'][0], S['vw'][0],
                       cols2(cache_nsa_win_k), cols2(cache_nsa_win_v),
                       rows2(cache_nsa_cmp_k), rows2(cache_nsa_cmp_v), cols2(cache_nsa_sel_k), cols2(cache_nsa_sel_v),
                       prm)
    ob_t = _moba_sample(page_table, S['qb'].T, S['kb'][0], S['vb'][0], cols2(cache_moba_k), cols2(cache_moba_v))
    h_s, xn_s, lg_s = _merge(x_sample.reshape(DB, D_MODEL), oa_t.T[:, order], ob_t.T, S['gm'], prm, DB, True)

    h_all = jnp.concatenate([h_p, h_s], axis=0)
    xn_all = jnp.concatenate([xn_p, xn_s], axis=0)
    logits = jnp.concatenate([lg_p, lg_s], axis=0)[:, :N_EXPERTS]
    slot, slot_tok, slot_gate, block_e, block_valid = _route(logits)
    y_slot = _moe_blocks(block_e, block_valid, xn_all[slot_tok], slot_gate,
                         w_gu.astype(BF16), b_gu.reshape(N_EXPERTS, 1, 2 * D_FF),
                         w_down.astype(BF16), b_down.reshape(N_EXPERTS, 1, D_MODEL))
    y = h_all + jnp.sum(y_slot[slot], axis=1)
    y_prompt = y[:B * T].reshape(B, T, D_MODEL)
    y_sample = y[B * T:].reshape(DB, DS, D_MODEL)

    def kv4(a):
        n, w, t = a.shape
        return jnp.transpose(a.reshape(n, w // HEAD_DIM, HEAD_DIM, t), (0, 3, 1, 2))

    new4 = lambda a: kv4(a).reshape(DB, 1, a.shape[1] // HEAD_DIM, HEAD_DIM)
    wp = min(WINDOW, T)
    ws = min(WINDOW, wb + DS)
    kw_all = jnp.concatenate([cache_nsa_win_k, new4(S['kw'])], axis=1)[:, -ws:]
    vw_all = jnp.concatenate([cache_nsa_win_v, new4(S['vw'])], axis=1)[:, -ws:]
    return (y_prompt, y_sample,
            kv4(P['kc']), kv4(P['vc']), kv4(P['ks']), kv4(P['vs']),
            kv4(P['kw'][:, :, T - wp:]), kv4(P['vw'][:, :, T - wp:]), kv4(P['kb']), kv4(P['vb']),
            new4(S['kc']), new4(S['vc']), new4(S['ks']), new4(S['vs']), kw_all, vw_all,
            new4(S['kb']), new4(S['vb']))
```

```python
import functools

import numpy as np
import jax
import jax.numpy as jnp
from jax import lax
from jax.experimental import pallas as pl
from jax.experimental.pallas import tpu as pltpu

F32 = jnp.float32
BF16 = jnp.bfloat16

D_MODEL = 1024
PAGE_SIZE = 128
HEAD_DIM = 64
NSA_HEADS = 8
NSA_KV_HEADS = 2
NSA_GROUP = NSA_HEADS // NSA_KV_HEADS
CMP_BLOCK = 32
CMP_STRIDE = 16
CMP_HIDDEN = 256
SEL_BLOCK = 64
N_SEL = 8
WINDOW = 512
MOBA_HEADS = 8
MOBA_BLOCK = 256
MOBA_TOPK = 3
N_EXPERTS = 32
TOP_K = 4
D_FF = D_MODEL
SWIGLU_LIMIT = 7.0
SWIGLU_ALPHA = 1.702
ROPE_THETA = 10000.0
EPS = 1e-6
NEG = -1e30
POS_BIG = 1e30

LANES = 128
HALF = HEAD_DIM
ATT_SCALE = HEAD_DIM ** -0.5
Q_TILE = 128
KEY_BLOCK = MOBA_BLOCK
MOE_TILE = 256
VMEM_LIMIT = 56 * 1024 * 1024
FAR = -(1 << 20)


def _cparams(*sem):
    return pltpu.CompilerParams(dimension_semantics=sem, vmem_limit_bytes=VMEM_LIMIT)


def _lane_iota(shape):
    return lax.broadcasted_iota(jnp.int32, shape, len(shape) - 1)


def _row_iota(shape):
    return lax.broadcasted_iota(jnp.int32, shape, 0)


def _dot(a, b):
    return jnp.dot(a, b, preferred_element_type=F32)


def _dot_nt(a, b):
    return lax.dot_general(a, b, (((1,), (1,)), ((), ())), preferred_element_type=F32)


def _split(a):
    hi = a.astype(BF16)
    lo = (a - hi.astype(F32)).astype(BF16)
    return hi, lo


def _split_dot(a, b):
    hi, lo = _split(a)
    return _dot(hi, b) + _dot(lo, b)


def _dot_split(a, b):
    hi, lo = _split(b)
    return _dot(a, hi) + _dot(a, lo)


def _dot3(a, b, nt=False):
    mm = _dot_nt if nt else _dot
    a_hi, a_lo = _split(a)
    b_hi, b_lo = b if isinstance(b, tuple) else _split(b)
    return mm(a_hi, b_hi) + (mm(a_hi, b_lo) + mm(a_lo, b_hi))


def _mm(a, w, precise):
    return _dot3(a, w) if precise else _dot(a.astype(BF16), w)


def _masked_softmax(s, mask):
    s = jnp.where(mask, s, NEG)
    m = jnp.max(s, axis=-1, keepdims=True)
    e = jnp.where(mask, jnp.exp(s - m), 0.0)
    return e / jnp.maximum(jnp.sum(e, axis=-1, keepdims=True), 1e-30)


def _head_norm(y, gain, bd):
    ss = _split_dot(y * y, bd)
    return y * lax.rsqrt(ss * (1.0 / HEAD_DIM) + EPS) * gain


def _select_top(score, n_pick, idx_f, axis):
    sel = jnp.zeros(score.shape, jnp.bool_)
    for _ in range(n_pick):
        m = jnp.max(score, axis=axis, keepdims=True)
        first = jnp.min(jnp.where(score == m, idx_f, 1e9), axis=axis, keepdims=True)
        hit = idx_f == first
        sel = jnp.logical_or(sel, hit)
        score = jnp.where(hit, -jnp.inf, score)
    return sel


def _nsa_select(imp, blk, cur, axis):
    forced = (blk == 0) | (blk == cur) | (blk == cur - 1)
    valid = blk <= cur
    score = jnp.where(forced, POS_BIG, jnp.where(valid, imp, -jnp.inf))
    sel = _select_top(score, N_SEL, blk.astype(F32), axis)
    return jnp.logical_and(sel, valid)


def _online_update(m_ref, l_ref, acc_ref, i, s, vt, first):
    m_cur = jnp.max(s, axis=0, keepdims=True)
    if first:
        p = jnp.exp(s - m_cur)
        m_ref[i] = m_cur
        l_ref[i] = jnp.sum(p, axis=0, keepdims=True)
        acc_ref[i] = _dot(vt, p.astype(BF16))
    else:
        m_prev = m_ref[i]
        m_new = jnp.maximum(m_prev, m_cur)
        alpha = jnp.exp(m_prev - m_new)
        p = jnp.exp(s - m_new)
        m_ref[i] = m_new
        l_ref[i] = alpha * l_ref[i] + jnp.sum(p, axis=0, keepdims=True)
        acc_ref[i] = alpha * acc_ref[i] + _dot(vt, p.astype(BF16))


def _inproj_kernel(precise, x_ref, cos_ref, sin_ref, gn_ref, wqat, wkvat, wgnt, wqbt, wkbt, wvbt, wgm, g_col, *outs):
    (qa_o, kcr_o, vcr_o, kc_o, vc_o, ks_o, vs_o, kw_o, vw_o, ksr_o, vsb_o, kwr_o, vwb_o,
     gn_o, qb_o, kb_o, vb_o, kbr_o, vbb_o, gm_o) = outs
    x = x_ref[...]
    xn = x * lax.rsqrt(jnp.mean(x * x, axis=-1, keepdims=True) + EPS) * gn_ref[...]
    xa = xn if precise else xn.astype(BF16)
    mm_t = (lambda wt: _dot3(wt, xn, nt=True)) if precise else (lambda wt: _dot_nt(wt, xa))
    half = HEAD_DIM // 2
    cos = cos_ref[...][None]
    sin = sin_ref[...][None]

    def tile(j):
        return slice(j * LANES, (j + 1) * LANES)

    def qk_t(wt_ref, j, gain_col):
        y = mm_t(wt_ref[tile(j), :])
        y = y.reshape(2, HEAD_DIM, y.shape[-1])
        y = y * lax.rsqrt(jnp.mean(y * y, axis=1, keepdims=True) + EPS) * gain_col[None]
        x1, x2 = y[:, :half], y[:, half:]
        return jnp.concatenate([x1 * cos - x2 * sin, x2 * cos + x1 * sin], axis=1).reshape(LANES, -1)

    for j in range(4):
        qa_o[0, tile(j), :] = qk_t(wqat, j, g_col[:, 0:1]).astype(qa_o.dtype)
        qb_o[0, tile(j), :] = qk_t(wqbt, j, g_col[:, 1:2]).astype(qb_o.dtype)
    nsa_kv = ((kc_o, vc_o, kcr_o, vcr_o), (ks_o, vs_o, ksr_o, vsb_o), (kw_o, vw_o, kwr_o, vwb_o))
    for i, (k_o, v_o, k2_o, v2_o) in enumerate(nsa_kv):
        k = qk_t(wkvat, 2 * i, g_col[:, 2 + i:3 + i])
        v = mm_t(wkvat[tile(2 * i + 1), :])
        k_o[0] = k
        v_o[0] = v
        if i == 0:
            k2_o[...] = k.T
            v2_o[...] = v.T
        else:
            k2_o[0, 0] = k.T.astype(BF16)
            v2_o[0, 0] = v.astype(BF16)
    gn_o[0] = jax.nn.sigmoid(mm_t(wgnt[...]))
    for j in range(4):
        k = qk_t(wkbt, j, g_col[:, 5:6])
        v = mm_t(wvbt[tile(j), :])
        kb_o[0, tile(j), :] = k
        vb_o[0, tile(j), :] = v
        kbr_o[0, 0, :, tile(j)] = k.T.astype(BF16)
        vbb_o[0, 0, tile(j), :] = v.astype(BF16)
    for j in range(2 * D_MODEL // LANES):
        gm_o[:, tile(j)] = jax.nn.sigmoid(_mm(xa, wgm[:, tile(j)], precise))


_INPROJ_W = ('wqat', 'wkvat', 'wgnt', 'wqbt', 'wkbt', 'wvbt', 'wgm')
_INPROJ_OUT = ('qa', 'kc_row', 'vc_row', 'kc', 'vc', 'ks', 'vs', 'kw', 'vw', 'ks_rows', 'vs_blk', 'kw_rows', 'vw_blk',
               'gn', 'qb', 'kb', 'vb', 'kb_rows', 'vb_blk', 'gm')


def _in_proj(x2d, n_batch, rope, prm, tm, precise):
    n = x2d.shape[0]
    t = n // n_batch
    nb = t // tm
    cos_t, sin_t = rope
    row = lambda i: (i, 0)
    const = lambda i: (0, 0)
    pos_t = lambda i: (0, i % nb)
    tr = lambda i: (i // nb, 0, i % nb)
    blk = lambda i: (i // nb, i % nb, 0, 0)
    ws = [prm[k + ('_f32' if precise else '')] for k in _INPROJ_W] + [prm['g_col']]
    in_specs = ([pl.BlockSpec((tm, D_MODEL), row), pl.BlockSpec((HEAD_DIM // 2, tm), pos_t),
                 pl.BlockSpec((HEAD_DIM // 2, tm), pos_t), pl.BlockSpec((1, D_MODEL), const)]
                + [pl.BlockSpec(w.shape, const) for w in ws])
    qdt = F32 if precise else BF16
    rows_out = lambda w, dt: (jax.ShapeDtypeStruct((n, w), dt), pl.BlockSpec((tm, w), row))
    t_out = lambda w, dt=F32: (jax.ShapeDtypeStruct((n_batch, w, t), dt), pl.BlockSpec((1, w, tm), tr))
    r_out = lambda w: (jax.ShapeDtypeStruct((n_batch, nb, tm, w), BF16), pl.BlockSpec((1, 1, tm, w), blk))
    b_out = lambda w: (jax.ShapeDtypeStruct((n_batch, nb, w, tm), BF16), pl.BlockSpec((1, 1, w, tm), blk))
    outs = ([t_out(512, qdt), rows_out(LANES, F32), rows_out(LANES, F32)] + [t_out(LANES)] * 6
            + [r_out(LANES), b_out(LANES), r_out(LANES), b_out(LANES)]
            + [t_out(LANES), t_out(512, qdt), t_out(512), t_out(512), r_out(512), b_out(512),
               rows_out(2 * D_MODEL, F32)])
    res = pl.pallas_call(
        functools.partial(_inproj_kernel, precise), grid=(n // tm,), in_specs=in_specs,
        out_specs=[o[1] for o in outs], out_shape=[o[0] for o in outs], compiler_params=_cparams("parallel"),
        name="in_proj_sample" if precise else "in_proj",
    )(x2d, cos_t, sin_t, prm['norm_attn'], *ws)
    return dict(zip(_INPROJ_OUT, res))


def _compress_hidden(xs, pos_a, pos_b, w1a, w1b, precise):
    lo = _lane_iota((LANES, LANES)) < HALF
    t0, t1 = [], []
    for l in range(0, CMP_STRIDE, 2):
        a, b = xs[l], xs[l + 1]
        t0.append(jnp.where(lo, a, pltpu.roll(b, HALF, 1)))
        t1.append(jnp.where(lo, pltpu.roll(a, HALF, 1), b))
    x = jnp.concatenate([jnp.concatenate(t0, axis=1), jnp.concatenate(t1, axis=1)], axis=0)
    first = _mm(x + pos_a, w1a, precise)
    second = _mm(x + pos_b, w1b, precise)
    n = LANES
    shifted = jnp.concatenate([pltpu.roll(second[:n], n - 1, 0), pltpu.roll(second[n:], n - 1, 0)], axis=0)
    return jax.nn.gelu(first + shifted)


_CW_NAMES = ('pka', 'pkb', 'k1a', 'k1b', 'k2g0', 'k2g1', 'pva', 'pvb', 'v1a', 'v1b', 'v2t', 'gkc', 'bd')


def _compress_prompt_kernel(k_ref, v_ref, *rest):
    cw = dict(zip(_CW_NAMES, rest[:len(_CW_NAMES)]))
    kc_o, vct_o = rest[len(_CW_NAMES):]
    n = k_ref.shape[1] // CMP_STRIDE

    def hidden(ref, p):
        xs = [ref[0, pl.ds(l, n, stride=CMP_STRIDE), :] for l in range(CMP_STRIDE)]
        return _compress_hidden(xs, cw['p%sa' % p][...], cw['p%sb' % p][...], cw['%s1a' % p][...],
                                cw['%s1b' % p][...], False).astype(BF16)

    hk = hidden(k_ref, 'k')
    kc = _dot(hk[:LANES], cw['k2g0'][...]) + _dot(hk[LANES:], cw['k2g1'][...])
    kc_o[0] = _head_norm(kc, cw['gkc'][...], cw['bd'][...])
    hv = hidden(v_ref, 'v')
    w2t = cw['v2t'][...]
    vct_o[0] = jnp.concatenate([_dot_nt(w2t, hv[:LANES]), _dot_nt(w2t, hv[LANES:])], axis=0)


def _compress_prompt(k_c, v_c, prm):
    b, t, _ = k_c.shape
    assert t // CMP_STRIDE == LANES
    ws = [prm[k] for k in _CW_NAMES]
    per_b = lambda i: (i, 0, 0)
    const = lambda i: (0, 0)
    return pl.pallas_call(
        _compress_prompt_kernel, grid=(b,),
        in_specs=[pl.BlockSpec((1, t, LANES), per_b)] * 2 + [pl.BlockSpec(w.shape, const) for w in ws],
        out_specs=[pl.BlockSpec((1, LANES, LANES), per_b)] * 2,
        out_shape=[jax.ShapeDtypeStruct((b, LANES, LANES), F32)] * 2,
        compiler_params=_cparams("parallel"), name="compress_prompt",
    )(k_c, v_c, *ws)


def _nsa_prompt_kernel(q_ref, g_ref, kc_ref, vct_ref, ks_ref, vs_ref, kw_ref, vw_ref, ovt_ref,
                       o_ref, qaug_ref, m_ref, l_ref, acc_ref):
    qt = Q_TILE
    cols = NSA_HEADS * qt
    kb = ks_ref.shape[2]
    s0 = pl.program_id(1) * qt
    qx = q_ref[0] * ATT_SCALE
    zero = jnp.zeros((HEAD_DIM, qt), qx.dtype)
    blocks = []
    for h in range(NSA_HEADS):
        qh = qx[h * HEAD_DIM:(h + 1) * HEAD_DIM]
        blocks.append(jnp.concatenate([qh, zero] if h < NSA_GROUP else [zero, qh], axis=0))
    qst = jnp.concatenate(blocks, axis=1)
    t_col = s0 + (_lane_iota((1, cols)) & (qt - 1))

    n_c = kc_ref.shape[1] - 1
    s_c = _dot(kc_ref[0].astype(BF16), qst)
    c_end = _row_iota(s_c.shape) * CMP_STRIDE + (CMP_BLOCK - 1)
    ok = c_end <= jnp.minimum(t_col, (n_c - 1) * CMP_STRIDE + CMP_BLOCK - 1)
    s_c = jnp.where(ok, s_c, NEG)
    e = jnp.where(ok, jnp.exp(s_c - jnp.max(s_c, axis=0, keepdims=True)), 0.0)
    p_c = e / jnp.maximum(jnp.sum(e, axis=0, keepdims=True), 1e-30)
    o_c = _dot(vct_ref[0].astype(BF16), p_c.astype(BF16))
    g4 = NSA_GROUP * qt
    imp = jnp.concatenate([sum(p_c[:, g * g4 + r * qt: g * g4 + (r + 1) * qt] for r in range(NSA_GROUP))
                           for g in range(NSA_KV_HEADS)], axis=1)
    imp = _dot_split(ovt_ref[...], imp)
    cur = (s0 + (_lane_iota((1, 2 * qt)) & (qt - 1))) // SEL_BLOCK
    sel = _nsa_select(imp, _row_iota(imp.shape), cur, 0)

    bias = jnp.where(sel, 0.0, NEG).astype(BF16)
    bias = jnp.concatenate([bias[:, :qt]] * NSA_GROUP + [bias[:, qt:]] * NSA_GROUP, axis=1)
    qaug_ref[...] = jnp.concatenate([qst, bias], axis=0)
    per_kb = kb // SEL_BLOCK

    def sel_block(c, causal, first):
        key = _row_iota((kb, LANES))
        onehot = jnp.where(_lane_iota((kb, LANES)) - c * per_kb == key // SEL_BLOCK, 1.0, 0.0).astype(BF16)
        s = _dot(jnp.concatenate([ks_ref[0, c], onehot], axis=1), qaug_ref[...])
        if causal:
            s = jnp.where(c * kb + _row_iota(s.shape) <= t_col, s, NEG)
        _online_update(m_ref, l_ref, acc_ref, 0, s, vs_ref[0, c], first)

    c_last = s0 // kb
    sel_block(0, True, True)

    def body(c, carry):
        sel_block(c, False, False)
        return carry

    lax.fori_loop(1, c_last, body, 0)

    @pl.when(c_last > 0)
    def _():
        sel_block(c_last, True, False)

    o_s = acc_ref[0] / jnp.maximum(l_ref[0], 1e-30)

    w0 = s0 // kb - (WINDOW // kb)
    n_wb = WINDOW // kb + 1
    idx = [jnp.maximum(w0 + j, 0) for j in range(n_wb)]
    s_parts = []
    for j in range(n_wb):
        base = jnp.where(w0 + j < 0, FAR, (w0 + j) * kb)
        dist = t_col - (base + _row_iota((kb, cols)))
        s_j = _dot(kw_ref[0, idx[j]], qst)
        s_parts.append(jnp.where(dist >= 0, jnp.where(dist < WINDOW, s_j, NEG), NEG))
    m_w = functools.reduce(jnp.maximum, [jnp.max(s_j, axis=0, keepdims=True) for s_j in s_parts])
    p_parts = [jnp.exp(s_j - m_w) for s_j in s_parts]
    l_w = sum(jnp.sum(p_j, axis=0, keepdims=True) for p_j in p_parts)
    o_w = sum(_dot(vw_ref[0, idx[j]], p_parts[j].astype(BF16)) for j in range(n_wb)) / l_w

    g = g_ref[0]
    gate = lambda j: jnp.concatenate([g[j * NSA_HEADS + h: j * NSA_HEADS + h + 1] for h in range(NSA_HEADS)], axis=1)
    o = gate(0) * o_c + gate(1) * o_s + gate(2) * o_w
    heads = [o[(h // NSA_GROUP) * HEAD_DIM:(h // NSA_GROUP + 1) * HEAD_DIM, h * qt:(h + 1) * qt]
             for h in range(NSA_HEADS)]
    o_ref[0] = jnp.concatenate(heads, axis=0).T.astype(BF16)


def _nsa_prompt(q_a, g_nsa, kc, vct, ks_rows, vs_blk, kw_rows, vw_blk, prm):
    b, w, t = q_a.shape
    qt = Q_TILE
    kb = ks_rows.shape[2]
    assert kb % qt == 0 and WINDOW % kb == 0 and kb % SEL_BLOCK == 0 and t // SEL_BLOCK <= LANES
    cols = NSA_HEADS * qt
    per_q = lambda i, j: (i, 0, j)
    per_b = lambda i, j: (i, 0, 0)
    per_b4 = lambda i, j: (i, 0, 0, 0)
    return pl.pallas_call(
        _nsa_prompt_kernel, grid=(b, t // qt),
        in_specs=[pl.BlockSpec((1, w, qt), per_q), pl.BlockSpec((1, LANES, qt), per_q),
                  pl.BlockSpec((1, LANES, LANES), per_b), pl.BlockSpec((1, LANES, LANES), per_b),
                  pl.BlockSpec((1,) + ks_rows.shape[1:], per_b4), pl.BlockSpec((1,) + vs_blk.shape[1:], per_b4),
                  pl.BlockSpec((1,) + kw_rows.shape[1:], per_b4), pl.BlockSpec((1,) + vw_blk.shape[1:], per_b4),
                  pl.BlockSpec(prm['ovt'].shape, lambda i, j: (0, 0))],
        out_specs=pl.BlockSpec((1, qt, w), lambda i, j: (i, j, 0)),
        out_shape=jax.ShapeDtypeStruct((b, t, w), BF16),
        scratch_shapes=[pltpu.VMEM((2 * LANES, cols), BF16), pltpu.VMEM((1, 1, cols), F32),
                        pltpu.VMEM((1, 1, cols), F32), pltpu.VMEM((1, LANES, cols), F32)],
        compiler_params=_cparams("arbitrary", "arbitrary"), name="nsa_prompt",
    )(q_a, g_nsa, kc, vct, ks_rows, vs_blk, kw_rows, vw_blk, prm['ovt'])


def _moba_prompt_kernel(q_ref, kt_ref, kb_ref, vb_ref, o_ref, kmean_ref, qaug_ref, m_ref, l_ref, acc_ref):
    qt = MOBA_BLOCK
    cur = pl.program_id(1)
    n_b = kb_ref.shape[1]
    n_pair = MOBA_HEADS // 2
    cols = 2 * qt

    @pl.when(cur == 0)
    def _():
        lane = _lane_iota((kt_ref.shape[1], LANES))
        km = jnp.zeros(lane.shape, F32)
        for n in range(n_b):
            km = jnp.where(lane == n, jnp.mean(kt_ref[0, :, n * qt:(n + 1) * qt], axis=-1, keepdims=True), km)
        kmean_ref[...] = km.T

    causal = _row_iota((qt, cols)) <= (_lane_iota((qt, cols)) & (qt - 1))
    for j in range(n_pair):
        tl = slice(j * LANES, (j + 1) * LANES)
        qj = q_ref[0, tl, :]
        zero = jnp.zeros((HEAD_DIM, qt), qj.dtype)
        qst = jnp.concatenate([jnp.concatenate([qj[:HEAD_DIM], zero], axis=0),
                               jnp.concatenate([zero, qj[HEAD_DIM:]], axis=0)], axis=1)
        km_hi, km_lo = _split(kmean_ref[:, tl])
        s_g = _dot(km_hi, qst) + _dot(km_lo, qst)
        blk = _row_iota(s_g.shape)
        past = blk < cur
        sel = _select_top(jnp.where(past, s_g, -jnp.inf), MOBA_TOPK, blk.astype(F32), 0)
        bias = jnp.where(jnp.logical_and(sel, past), 0.0, NEG).astype(BF16)
        qs = qst * ATT_SCALE
        qaug_ref[j] = jnp.concatenate([qs, bias], axis=0)
        s = jnp.where(causal, _dot(kb_ref[0, cur, :, tl], qs), NEG)
        _online_update(m_ref, l_ref, acc_ref, j, s, vb_ref[0, cur, tl, :], True)

    def body(n, carry):
        onehot = jnp.where(_lane_iota((qt, LANES)) == n, 1.0, 0.0).astype(BF16)
        for j in range(n_pair):
            tl = slice(j * LANES, (j + 1) * LANES)
            s = _dot(jnp.concatenate([kb_ref[0, n, :, tl], onehot], axis=1), qaug_ref[j])
            _online_update(m_ref, l_ref, acc_ref, j, s, vb_ref[0, n, tl, :], False)
        return carry

    lax.fori_loop(0, cur, body, 0)
    for j in range(n_pair):
        o = acc_ref[j] / jnp.maximum(l_ref[j], 1e-30)
        o = jnp.concatenate([o[:HEAD_DIM, :qt], o[HEAD_DIM:, qt:]], axis=0)
        o_ref[0, :, j * LANES:(j + 1) * LANES] = o.T.astype(BF16)


def _moba_prompt(q_b, kt, kb_rows, vb_blk):
    b, w, t = q_b.shape
    qt = MOBA_BLOCK
    assert kb_rows.shape[2] == qt and t // qt <= LANES
    per_b = lambda i, j: (i, 0, 0)
    per_b4 = lambda i, j: (i, 0, 0, 0)
    n_pair = MOBA_HEADS // 2
    return pl.pallas_call(
        _moba_prompt_kernel, grid=(b, t // qt),
        in_specs=[pl.BlockSpec((1, w, qt), lambda i, j: (i, 0, j)), pl.BlockSpec((1, w, t), per_b),
                  pl.BlockSpec((1,) + kb_rows.shape[1:], per_b4), pl.BlockSpec((1,) + vb_blk.shape[1:], per_b4)],
        out_specs=pl.BlockSpec((1, qt, w), lambda i, j: (i, j, 0)),
        out_shape=jax.ShapeDtypeStruct((b, t, w), BF16),
        scratch_shapes=[pltpu.VMEM((LANES, w), F32), pltpu.VMEM((n_pair, 2 * LANES, 2 * qt), BF16),
                        pltpu.VMEM((n_pair, 1, 2 * qt), F32), pltpu.VMEM((n_pair, 1, 2 * qt), F32),
                        pltpu.VMEM((n_pair, LANES, 2 * qt), F32)],
        compiler_params=_cparams("arbitrary", "arbitrary"), name="moba_prompt",
    )(q_b, kt, kb_rows, vb_blk)


def _merge_kernel(precise, x_ref, oa_ref, ob_ref, gm_ref, wa_ref, wb_ref, wo_ref, nf_ref, wr_hi, wr_lo, br_ref,
                  h_o, xn_o, lg_o):
    gm = gm_ref[...]
    u = (gm[:, :D_MODEL] * _mm(oa_ref[...], wa_ref[...], precise)
         + gm[:, D_MODEL:] * _mm(ob_ref[...], wb_ref[...], precise))
    h = x_ref[...] + _mm(u, wo_ref[...], precise)
    h_o[...] = h
    xn = h * lax.rsqrt(jnp.mean(h * h, axis=-1, keepdims=True) + EPS) * nf_ref[...]
    xn_o[...] = xn.astype(BF16)
    lg_o[...] = _dot3(xn, (wr_hi[...], wr_lo[...])) + br_ref[...]


def _merge(x2d, o_a, o_b, g_mrg, prm, tm, precise):
    n = x2d.shape[0]
    row = lambda i: (i, 0)
    const = lambda i: (0, 0)
    sfx = '_f32' if precise else ''
    ws = [prm[k] for k in ('w_up_a' + sfx, 'w_up_b' + sfx, 'w_out' + sfx, 'norm_ffn', 'wr_hi', 'wr_lo', 'b_router')]
    return pl.pallas_call(
        functools.partial(_merge_kernel, precise), grid=(n // tm,),
        in_specs=[pl.BlockSpec((tm, D_MODEL), row), pl.BlockSpec((tm, 512), row), pl.BlockSpec((tm, 512), row),
                  pl.BlockSpec((tm, 2 * D_MODEL), row)] + [pl.BlockSpec(w.shape, const) for w in ws],
        out_specs=[pl.BlockSpec((tm, D_MODEL), row), pl.BlockSpec((tm, D_MODEL), row), pl.BlockSpec((tm, LANES), row)],
        out_shape=[jax.ShapeDtypeStruct((n, D_MODEL), F32), jax.ShapeDtypeStruct((n, D_MODEL), BF16),
                   jax.ShapeDtypeStruct((n, LANES), F32)],
        compiler_params=_cparams("parallel"), name="merge_router_sample" if precise else "merge_router",
    )(x2d, o_a, o_b, g_mrg, *ws)


def _moe_kernel(be_ref, bv_ref, x_ref, g_ref, wgu_ref, bgu_ref, wd_ref, bd_ref, y_ref):
    i = pl.program_id(0)

    @pl.when(bv_ref[i] > 0)
    def _():
        h = _dot(x_ref[...], wgu_ref[0]) + bgu_ref[0]
        gate = jnp.minimum(h[:, :D_FF], SWIGLU_LIMIT)
        up = jnp.clip(h[:, D_FF:], -SWIGLU_LIMIT, SWIGLU_LIMIT)
        act = (up + 1.0) * gate * jax.nn.sigmoid(SWIGLU_ALPHA * gate)
        y = _dot(act.astype(BF16), wd_ref[0]) + bd_ref[0]
        y_ref[...] = y * g_ref[...]

    @pl.when(bv_ref[i] == 0)
    def _():
        y_ref[...] = jnp.zeros(y_ref.shape, F32)


def _moe_blocks(block_e, block_valid, x_sorted, slot_gate, w_gu, b_gu, w_down, b_down):
    cap = x_sorted.shape[0]
    tm = MOE_TILE
    row = lambda i, be, bv: (i, 0)
    exp = lambda i, be, bv: (be[i], 0, 0)
    grid_spec = pltpu.PrefetchScalarGridSpec(
        num_scalar_prefetch=2, grid=(cap // tm,),
        in_specs=[pl.BlockSpec((tm, D_MODEL), row), pl.BlockSpec((tm, 1), row),
                  pl.BlockSpec((1, D_MODEL, 2 * D_FF), exp), pl.BlockSpec((1, 1, 2 * D_FF), exp),
                  pl.BlockSpec((1, D_FF, D_MODEL), exp), pl.BlockSpec((1, 1, D_MODEL), exp)],
        out_specs=pl.BlockSpec((tm, D_MODEL), row))
    return pl.pallas_call(
        _moe_kernel, grid_spec=grid_spec, out_shape=jax.ShapeDtypeStruct((cap, D_MODEL), F32),
        compiler_params=_cparams("arbitrary"), name="moe_experts",
    )(block_e, block_valid, x_sorted, slot_gate, w_gu, b_gu, w_down, b_down)


def _route(logits):
    n = logits.shape[0]
    top_v, top_e = lax.top_k(logits, TOP_K)
    gates = jax.nn.softmax(top_v, axis=-1)
    onehot = jnp.sum((top_e[:, :, None] == jnp.arange(N_EXPERTS)[None, None, :]).astype(jnp.int32), axis=1)
    csum = jnp.cumsum(onehot, axis=0)
    counts = csum[-1]
    padded = (counts + MOE_TILE - 1) // MOE_TILE * MOE_TILE
    pend = jnp.cumsum(padded)
    rank = jnp.take_along_axis(csum - onehot, top_e, axis=1)
    slot = (pend - padded)[top_e] + rank
    n_blocks = -(-(n * TOP_K) // MOE_TILE) + N_EXPERTS
    cap = n_blocks * MOE_TILE
    starts = jnp.arange(n_blocks, dtype=jnp.int32) * MOE_TILE
    block_e = jnp.minimum(jnp.sum((pend[None, :] <= starts[:, None]).astype(jnp.int32), axis=1), N_EXPERTS - 1)
    block_valid = (starts < pend[-1]).astype(jnp.int32)
    tok = jnp.broadcast_to(jnp.arange(n, dtype=jnp.int32)[:, None], (n, TOP_K))
    slot_tok = jnp.zeros((cap,), jnp.int32).at[slot.reshape(-1)].set(tok.reshape(-1))
    slot_gate = jnp.zeros((cap,), F32).at[slot.reshape(-1)].set(gates.reshape(-1))
    return slot, slot_tok, slot_gate.reshape(cap, 1), block_e, block_valid


def _column(ref, b):
    x = ref[...]
    return jnp.sum(jnp.where(_lane_iota(x.shape) == b, x, 0.0), axis=-1, keepdims=True)


def _fold_lanes(x):
    return sum(x[..., i * LANES:(i + 1) * LANES] for i in range(x.shape[-1] // LANES))


def _gqa_scores(q3, kt):
    g = NSA_GROUP
    return jnp.concatenate([jnp.sum(q3[i * g:(i + 1) * g] * kt[i * HEAD_DIM:(i + 1) * HEAD_DIM][None], axis=1)
                            for i in range(NSA_KV_HEADS)], axis=0)


def _gqa_weighted(p, vt):
    g = NSA_GROUP
    return jnp.concatenate([p[i * g:(i + 1) * g][:, None, :] * vt[i * HEAD_DIM:(i + 1) * HEAD_DIM][None]
                            for i in range(NSA_KV_HEADS)], axis=0)


_CWS_NAMES = ('pka', 'pkb', 'k1a_hi', 'k1a_lo', 'k1b_hi', 'k1b_lo', 'k2t',
              'pva', 'pvb', 'v1a_hi', 'v1a_lo', 'v1b_hi', 'v1b_lo', 'v2t', 'gkc_col')


def _nsa_sample_kernel(pt_ref, qt_ref, g_ref, ksn_ref, vsn_ref, kwn_ref, vwn_ref, wk_ref, wv_ref, *rest):
    n_pg = (len(rest) - len(_CWS_NAMES) - 3) // 4
    ck_pages = rest[0:n_pg]
    cv_pages = rest[n_pg:2 * n_pg]
    sk_pages = rest[2 * n_pg:3 * n_pg]
    sv_pages = rest[3 * n_pg:4 * n_pg]
    cw = dict(zip(_CWS_NAMES, rest[4 * n_pg:4 * n_pg + len(_CWS_NAMES)]))
    ov_ref, e_ref, o_ref = rest[4 * n_pg + len(_CWS_NAMES):]
    b = pl.program_id(0)
    past = n_pg * PAGE_SIZE
    per_pg = PAGE_SIZE // CMP_STRIDE

    def compress_t(pages, p):
        xs = [jnp.concatenate([pg[0, pl.ds(l, per_pg, stride=CMP_STRIDE), :] for pg in pages], axis=0)
              for l in range(CMP_STRIDE)]
        h = _compress_hidden(xs, cw['p%sa' % p][...], cw['p%sb' % p][...],
                             (cw['%s1a_hi' % p][...], cw['%s1a_lo' % p][...]),
                             (cw['%s1b_hi' % p][...], cw['%s1b_lo' % p][...]), True)
        w2t = cw['%s2t' % p][...]
        return jnp.concatenate([_dot3(w2t, h[:LANES], nt=True), _dot3(w2t, h[LANES:], nt=True)], axis=0)

    kct = compress_t(ck_pages, 'k').reshape(NSA_KV_HEADS, HEAD_DIM, LANES)
    ss = jnp.sum(kct * kct, axis=1, keepdims=True)
    kct = (kct * lax.rsqrt(ss * (1.0 / HEAD_DIM) + EPS) * cw['gkc_col'][...][None]).reshape(LANES, LANES)
    vct = compress_t(cv_pages, 'v')

    q3 = (_column(qt_ref, b) * ATT_SCALE).reshape(NSA_HEADS, HEAD_DIM, 1)

    n_c = (past + 1 - CMP_BLOCK) // CMP_STRIDE + 1
    s_c = _gqa_scores(q3, kct)
    p_c = _masked_softmax(s_c, _lane_iota(s_c.shape) < n_c)
    o_c = jnp.sum(_gqa_weighted(p_c, vct), axis=-1, keepdims=True)
    row = _row_iota((NSA_HEADS, LANES))
    i0 = jnp.sum(p_c[:NSA_GROUP], axis=0, keepdims=True)
    i1 = jnp.sum(p_c[NSA_GROUP:], axis=0, keepdims=True)
    imp = _split_dot(jnp.where(row < NSA_GROUP, i0, i1), ov_ref[...])
    sel = _nsa_select(imp, _lane_iota(imp.shape), past // SEL_BLOCK, -1)
    picked = _dot(jnp.where(sel, 1.0, 0.0).astype(BF16), e_ref[...]) > 0.5

    def attend_with_new(s, valid, kn_ref, vn_ref, weighted):
        kn = _column(kn_ref, b)
        vn = _column(vn_ref, b)
        s = jnp.where(valid, s, NEG)
        s_n = _gqa_scores(q3, kn)
        m = jnp.maximum(jnp.max(s, axis=-1, keepdims=True), s_n)
        p = jnp.where(valid, jnp.exp(s - m), 0.0)
        p_n = jnp.exp(s_n - m)
        num = jnp.sum(weighted(p), axis=-1, keepdims=True) + _gqa_weighted(p_n, vn)
        den = jnp.maximum(jnp.sum(p, axis=-1, keepdims=True) + p_n, 1e-30)
        return num / den[:, :, None]

    s_s = jnp.concatenate([_gqa_scores(q3, pg[0]) for pg in sk_pages], axis=1)
    o_s = attend_with_new(
        s_s, picked, ksn_ref, vsn_ref,
        lambda p: sum(_gqa_weighted(p[:, i * PAGE_SIZE:(i + 1) * PAGE_SIZE], pg[0]) for i, pg in enumerate(sv_pages)))

    wb = wk_ref.shape[2]
    s_w = _gqa_scores(q3, wk_ref[0])
    inside = (past - wb + _lane_iota(s_w.shape)) > (past - WINDOW)
    o_w = attend_with_new(s_w, inside, kwn_ref, vwn_ref, lambda p: _fold_lanes(_gqa_weighted(p, wv_ref[0])))

    g = jnp.broadcast_to(g_ref[0], (NSA_HEADS, LANES))
    lane = _lane_iota(g.shape)
    gate = lambda j: jnp.sum(jnp.where(lane == j * NSA_HEADS + row, g, 0.0), axis=-1, keepdims=True)[:, :, None]
    o = (gate(0) * o_c + gate(1) * o_s + gate(2) * o_w).reshape(NSA_HEADS * HEAD_DIM, 1)

    @pl.when(b == 0)
    def _():
        o_ref[...] = jnp.zeros(o_ref.shape, F32)

    o_ref[...] = jnp.where(_lane_iota(o_ref.shape) == b, o, o_ref[...])


def _nsa_sample(page_table, qt, g_nsa, ksn, vsn, kwn, vwn, win_kt, win_vt, cache_ck, cache_cv, cache_skt, cache_svt,
                prm):
    db, n_pg = page_table.shape
    wb = win_kt.shape[2]
    per_b = lambda i, pt: (i, 0, 0)
    const2 = lambda i, pt: (0, 0)

    def page_spec(p):
        return pl.BlockSpec((1, PAGE_SIZE, LANES), lambda i, pt, p=p: (pt[i, p], 0, 0))

    cws = [prm['s_' + k] for k in _CWS_NAMES]
    full = lambda a: pl.BlockSpec(a.shape, const2)
    in_specs = ([full(qt), pl.BlockSpec((1, 1, LANES), per_b), full(ksn), full(vsn), full(kwn), full(vwn),
                 pl.BlockSpec((1, LANES, wb), per_b), pl.BlockSpec((1, LANES, wb), per_b)]
                + [page_spec(p) for p in range(n_pg)] * 4
                + [full(w) for w in cws] + [full(prm['ov']), full(prm['e_key'])])
    grid_spec = pltpu.PrefetchScalarGridSpec(num_scalar_prefetch=1, grid=(db,), in_specs=in_specs,
                                             out_specs=pl.BlockSpec((NSA_HEADS * HEAD_DIM, db), const2))
    pages = [cache_ck] * n_pg + [cache_cv] * n_pg + [cache_skt] * n_pg + [cache_svt] * n_pg
    return pl.pallas_call(
        _nsa_sample_kernel, grid_spec=grid_spec, out_shape=jax.ShapeDtypeStruct((NSA_HEADS * HEAD_DIM, db), F32),
        compiler_params=_cparams("arbitrary"), name="nsa_sample",
    )(page_table, qt, g_nsa, ksn, vsn, kwn, vwn, win_kt, win_vt, *pages, *cws, prm['ov'], prm['e_key'])


def _moba_sample_kernel(pt_ref, qt_ref, kn_ref, vn_ref, *rest):
    n_pg = (len(rest) - 1) // 2
    k_pages = rest[:n_pg]
    v_pages = rest[n_pg:2 * n_pg]
    o_ref = rest[2 * n_pg]
    b = pl.program_id(0)
    pages_per_blk = MOBA_BLOCK // PAGE_SIZE
    n_b = n_pg // pages_per_blk
    heads3 = lambda x: x.reshape(MOBA_HEADS, HEAD_DIM, x.shape[-1])

    q3 = heads3(_column(qt_ref, b) * ATT_SCALE)
    kn3 = heads3(_column(kn_ref, b))
    vn3 = heads3(_column(vn_ref, b))
    s_pages = [jnp.sum(heads3(pg[0]) * q3, axis=1) for pg in k_pages]
    s_new = jnp.sum(q3 * kn3, axis=1)

    lane = _lane_iota((MOBA_HEADS, LANES))
    s_g = jnp.full((MOBA_HEADS, LANES), -jnp.inf, F32)
    for n in range(n_b):
        tot = sum(s_pages[n * pages_per_blk + i] for i in range(pages_per_blk))
        s_g = jnp.where(lane == n, jnp.sum(tot, axis=-1, keepdims=True), s_g)
    sel = _select_top(s_g, min(MOBA_TOPK, n_b), lane.astype(F32), -1)
    sel_blk = [jnp.sum(jnp.where(jnp.logical_and(sel, lane == n), 1.0, 0.0), axis=-1, keepdims=True) > 0.5
               for n in range(n_b)]

    m = s_new
    for i in range(n_pg):
        m = jnp.maximum(m, jnp.max(jnp.where(sel_blk[i // pages_per_blk], s_pages[i], NEG), axis=-1, keepdims=True))
    p_new = jnp.exp(s_new - m)
    l = p_new
    acc = jnp.zeros((MOBA_HEADS, HEAD_DIM, PAGE_SIZE), F32)
    for i in range(n_pg):
        p = jnp.where(sel_blk[i // pages_per_blk], jnp.exp(s_pages[i] - m), 0.0)
        l = l + jnp.sum(p, axis=-1, keepdims=True)
        acc = acc + heads3(v_pages[i][0]) * p[:, None, :]
    o = (jnp.sum(acc, axis=-1, keepdims=True) + p_new[:, :, None] * vn3) / jnp.maximum(l, 1e-30)[:, :, None]

    @pl.when(b == 0)
    def _():
        o_ref[...] = jnp.zeros(o_ref.shape, F32)

    o_ref[...] = jnp.where(_lane_iota(o_ref.shape) == b, o.reshape(MOBA_HEADS * HEAD_DIM, 1), o_ref[...])


def _moba_sample(page_table, qt, knt, vnt, cache_kt, cache_vt):
    db, n_pg = page_table.shape
    w = qt.shape[0]
    assert (n_pg * PAGE_SIZE) % MOBA_BLOCK == 0
    const2 = lambda i, pt: (0, 0)

    def page_spec(p):
        return pl.BlockSpec((1, w, PAGE_SIZE), lambda i, pt, p=p: (pt[i, p], 0, 0))

    in_specs = [pl.BlockSpec((w, db), const2)] * 3 + [page_spec(p) for p in range(n_pg)] * 2
    grid_spec = pltpu.PrefetchScalarGridSpec(num_scalar_prefetch=1, grid=(db,), in_specs=in_specs,
                                             out_specs=pl.BlockSpec((w, db), const2))
    return pl.pallas_call(
        _moba_sample_kernel, grid_spec=grid_spec, out_shape=jax.ShapeDtypeStruct((w, db), F32),
        compiler_params=_cparams("arbitrary"), name="moba_sample",
    )(page_table, qt, knt, vnt, *([cache_kt] * n_pg), *([cache_vt] * n_pg))


def _split_weight(w):
    c = w * (2.0 ** 16 + 1.0)
    hi = c - (c - w)
    return hi.astype(BF16), (w - hi).astype(BF16)


def _prepare(norm_attn, w_in, g_qa, g_ka, g_kc, cmp_pos_k, cmp_w1_k, cmp_w2_k, cmp_pos_v, cmp_w1_v, cmp_w2_v,
             g_qb, g_kb, w_up_a, w_up_b, w_out, norm_ffn, w_router, b_router):
    qa = NSA_HEADS * HEAD_DIM
    kva = NSA_KV_HEADS * HEAD_DIM
    qb = MOBA_HEADS * HEAD_DIM
    widths = [qa, kva, kva, kva, kva, kva, kva, 3 * NSA_HEADS, qb, qb, qb, 2 * D_MODEL]
    cuts = np.cumsum([0] + widths)
    part = lambda i: w_in[:, cuts[i]:cuts[i + 1]]
    gcols = np.array([h * 3 + j for j in range(3) for h in range(NSA_HEADS)])
    wgnt = jnp.zeros((LANES, D_MODEL), F32).at[:3 * NSA_HEADS].set(part(7)[:, gcols].T)
    two = lambda g: jnp.concatenate([g, g]).reshape(1, LANES)
    lane = np.arange(LANES)
    c_idx = np.arange(LANES)[:, None] * CMP_STRIDE
    s_idx = np.arange(LANES)[None, :] * SEL_BLOCK
    ov = ((c_idx < s_idx + SEL_BLOCK) & (c_idx + CMP_BLOCK > s_idx) & (np.arange(LANES)[:, None] < LANES - 1))
    half_w = CMP_STRIDE * HEAD_DIM

    def w2wide(w2):
        z = jnp.zeros_like(w2)
        return jnp.concatenate([w2, z], axis=1).astype(BF16), jnp.concatenate([z, w2], axis=1).astype(BF16)

    k2g0, k2g1 = w2wide(cmp_w2_k)
    wr_hi, wr_lo = _split_weight(jnp.zeros((D_MODEL, LANES), F32).at[:, :N_EXPERTS].set(w_router))
    prm = {
        'norm_attn': norm_attn.reshape(1, D_MODEL),
        'wqat_f32': part(0).T, 'wkvat_f32': w_in[:, cuts[1]:cuts[7]].T, 'wgnt_f32': wgnt,
        'wqbt_f32': part(8).T, 'wkbt_f32': part(9).T, 'wvbt_f32': part(10).T, 'wgm_f32': part(11),
        'g_col': jnp.stack([g_qa, g_qb, g_ka[0], g_ka[1], g_ka[2], g_kb], axis=1),
        'gkc': two(g_kc),
        'bd': jnp.asarray((lane[:, None] // HEAD_DIM) == (lane[None, :] // HEAD_DIM), BF16),
        'pka': cmp_pos_k[:CMP_STRIDE].reshape(1, half_w), 'pkb': cmp_pos_k[CMP_STRIDE:].reshape(1, half_w),
        'pva': cmp_pos_v[:CMP_STRIDE].reshape(1, half_w), 'pvb': cmp_pos_v[CMP_STRIDE:].reshape(1, half_w),
        'k1a': cmp_w1_k[:half_w].astype(BF16), 'k1b': cmp_w1_k[half_w:].astype(BF16),
        'v1a': cmp_w1_v[:half_w].astype(BF16), 'v1b': cmp_w1_v[half_w:].astype(BF16),
        'k2g0': k2g0, 'k2g1': k2g1, 'v2t': cmp_w2_v.T.astype(BF16),
        'ov': jnp.asarray(ov, BF16), 'ovt': jnp.asarray(ov.T, BF16),
        'w_up_a_f32': w_up_a, 'w_up_b_f32': w_up_b, 'w_out_f32': w_out,
        'norm_ffn': norm_ffn.reshape(1, D_MODEL),
        'wr_hi': wr_hi, 'wr_lo': wr_lo,
        'b_router': jnp.zeros((1, LANES), F32).at[0, :N_EXPERTS].set(b_router),
    }
    for k in _INPROJ_W + ('w_up_a', 'w_up_b', 'w_out'):
        prm[k] = prm[k + '_f32'].astype(BF16)
    for p, w1, w2 in (('k', cmp_w1_k, cmp_w2_k), ('v', cmp_w1_v, cmp_w2_v)):
        for half, w in (('a', w1[:half_w]), ('b', w1[half_w:])):
            prm['s_%s1%s_hi' % (p, half)], prm['s_%s1%s_lo' % (p, half)] = _split_weight(w)
        prm['s_%s2t' % p] = w2.T
        prm['s_p%sa' % p] = prm['p%sa' % p]
        prm['s_p%sb' % p] = prm['p%sb' % p]
    prm['s_gkc_col'] = g_kc.reshape(HEAD_DIM, 1)
    return prm


def _rope_tables(pos):
    half = HEAD_DIM // 2
    inv = ROPE_THETA ** (-np.arange(half, dtype=np.float64) / half)
    ang = inv[:, None] * np.asarray(pos, np.float64)[None, :]
    return jnp.asarray(np.cos(ang), F32), jnp.asarray(np.sin(ang), F32)


def _expand_matrix(n_keys, block):
    e = (np.arange(n_keys)[None, :] // block) == np.arange(LANES)[:, None]
    return jnp.asarray(e, BF16)


def kernel(x_prompt, x_sample, cache_nsa_cmp_k, cache_nsa_cmp_v, cache_nsa_sel_k, cache_nsa_sel_v, cache_nsa_win_k, cache_nsa_win_v, cache_moba_k, cache_moba_v, page_table, norm_attn, w_in, g_qa, g_ka, g_kc, cmp_pos_k, cmp_w1_k, cmp_w2_k, cmp_pos_v, cmp_w1_v, cmp_w2_v, g_qb, g_kb, w_up_a, w_up_b, w_out, norm_ffn, w_router, b_router, w_gu, b_gu, w_down, b_down):
    B, T, _ = x_prompt.shape
    DB, DS, _ = x_sample.shape
    assert DS == 1
    n_pg = page_table.shape[1]
    past = n_pg * PAGE_SIZE
    wb = cache_nsa_win_k.shape[1]

    prm = _prepare(norm_attn, w_in, g_qa, g_ka, g_kc, cmp_pos_k, cmp_w1_k, cmp_w2_k, cmp_pos_v, cmp_w1_v, cmp_w2_v,
                   g_qb, g_kb, w_up_a, w_up_b, w_out, norm_ffn, w_router, b_router)
    prm['e_key'] = _expand_matrix(past, SEL_BLOCK)

    P = _in_proj(x_prompt.reshape(B * T, D_MODEL), B, _rope_tables(np.arange(T)), prm, KEY_BLOCK, False)
    r3 = lambda a: a.reshape(B, T, a.shape[-1])
    kc_cmp, vct_cmp = _compress_prompt(r3(P['kc_row']), r3(P['vc_row']), prm)
    oa_p = _nsa_prompt(P['qa'], P['gn'], kc_cmp, vct_cmp, P['ks_rows'], P['vs_blk'], P['kw_rows'], P['vw_blk'], prm)
    ob_p = _moba_prompt(P['qb'], P['kb'], P['kb_rows'], P['vb_blk'])
    h_p, xn_p, lg_p = _merge(x_prompt.reshape(B * T, D_MODEL), oa_p.reshape(B * T, 512), ob_p.reshape(B * T, 512),
                             P['gm'], prm, 256, False)

    S = _in_proj(x_sample.reshape(DB, D_MODEL), 1, _rope_tables(np.full((DB,), past)), prm, DB, True)
    rows2 = lambda c: c.reshape(c.shape[0], c.shape[1], c.shape[2] * c.shape[3])
    cols2 = lambda c: jnp.transpose(c, (0, 2, 3, 1)).reshape(c.shape[0], c.shape[2] * c.shape[3], c.shape[1])
    oa_t = _nsa_sample(page_table, S['qa'][0], S['gn'][0].T.reshape(DB, 1, LANES),
                       S['ks'][0], S['vs'][0], S['kw'][0], S['vw'][0],
                       cols2(cache_nsa_win_k), cols2(cache_nsa_win_v),
                       rows2(cache_nsa_cmp_k), rows2(cache_nsa_cmp_v), cols2(cache_nsa_sel_k), cols2(cache_nsa_sel_v),
                       prm)
    ob_t = _moba_sample(page_table, S['qb'][0], S['kb'][0], S['vb'][0], cols2(cache_moba_k), cols2(cache_moba_v))
    h_s, xn_s, lg_s = _merge(x_sample.reshape(DB, D_MODEL), oa_t.T, ob_t.T, S['gm'], prm, DB, True)

    h_all = jnp.concatenate([h_p, h_s], axis=0)
    xn_all = jnp.concatenate([xn_p, xn_s], axis=0)
    logits = jnp.concatenate([lg_p, lg_s], axis=0)[:, :N_EXPERTS]
    slot, slot_tok, slot_gate, block_e, block_valid = _route(logits)
    y_slot = _moe_blocks(block_e, block_valid, xn_all[slot_tok], slot_gate,
                         w_gu.astype(BF16), b_gu.reshape(N_EXPERTS, 1, 2 * D_FF),
                         w_down.astype(BF16), b_down.reshape(N_EXPERTS, 1, D_MODEL))
    y = h_all + jnp.sum(y_slot[slot], axis=1)
    y_prompt = y[:B * T].reshape(B, T, D_MODEL)
    y_sample = y[B * T:].reshape(DB, DS, D_MODEL)

    def kv4(a):
        n, w, t = a.shape
        return jnp.transpose(a.reshape(n, w // HEAD_DIM, HEAD_DIM, t), (0, 3, 1, 2))

    new4 = lambda a: kv4(a).reshape(DB, 1, a.shape[1] // HEAD_DIM, HEAD_DIM)
    wp = min(WINDOW, T)
    ws = min(WINDOW, wb + DS)
    kw_all = jnp.concatenate([cache_nsa_win_k, new4(S['kw'])], axis=1)[:, -ws:]
    vw_all = jnp.concatenate([cache_nsa_win_v, new4(S['vw'])], axis=1)[:, -ws:]
    return (y_prompt, y_sample,
            kv4(P['kc']), kv4(P['vc']), kv4(P['ks']), kv4(P['vs']),
            kv4(P['kw'][:, :, T - wp:]), kv4(P['vw'][:, :, T - wp:]), kv4(P['kb']), kv4(P['vb']),
            new4(S['kc']), new4(S['vc']), new4(S['ks']), new4(S['vs']), kw_all, vw_all,
            new4(S['kb']), new4(S['vb']))
```

```python
import functools

import numpy as np
import jax
import jax.numpy as jnp
from jax import lax
from jax.experimental import pallas as pl
from jax.experimental.pallas import tpu as pltpu

F32 = jnp.float32
BF16 = jnp.bfloat16

D_MODEL = 1024
PAGE_SIZE = 128
HEAD_DIM = 64
NSA_HEADS = 8
NSA_KV_HEADS = 2
NSA_GROUP = NSA_HEADS // NSA_KV_HEADS
CMP_BLOCK = 32
CMP_STRIDE = 16
CMP_HIDDEN = 256
SEL_BLOCK = 64
N_SEL = 8
WINDOW = 512
MOBA_HEADS = 8
MOBA_BLOCK = 256
MOBA_TOPK = 3
N_EXPERTS = 32
TOP_K = 4
D_FF = D_MODEL
SWIGLU_LIMIT = 7.0
SWIGLU_ALPHA = 1.702
ROPE_THETA = 10000.0
EPS = 1e-6
NEG = -1e30
POS_BIG = 1e30

LANES = 128
HALF = HEAD_DIM
ATT_SCALE = HEAD_DIM ** -0.5
Q_TILE = 128
KEY_BLOCK = MOBA_BLOCK
MOE_TILE = 256
VMEM_LIMIT = 56 * 1024 * 1024
FAR = -(1 << 20)


def _cparams(*sem):
    return pltpu.CompilerParams(dimension_semantics=sem, vmem_limit_bytes=VMEM_LIMIT)


def _lane_iota(shape):
    return lax.broadcasted_iota(jnp.int32, shape, len(shape) - 1)


def _row_iota(shape):
    return lax.broadcasted_iota(jnp.int32, shape, 0)


def _dot(a, b):
    return jnp.dot(a, b, preferred_element_type=F32)


def _dot_nt(a, b):
    return lax.dot_general(a, b, (((1,), (1,)), ((), ())), preferred_element_type=F32)


def _split(a):
    hi = a.astype(BF16)
    lo = (a - hi.astype(F32)).astype(BF16)
    return hi, lo


def _split_dot(a, b):
    hi, lo = _split(a)
    return _dot(hi, b) + _dot(lo, b)


def _dot_split(a, b):
    hi, lo = _split(b)
    return _dot(a, hi) + _dot(a, lo)


def _dot3(a, b, nt=False):
    mm = _dot_nt if nt else _dot
    a_hi, a_lo = _split(a)
    b_hi, b_lo = b if isinstance(b, tuple) else _split(b)
    return mm(a_hi, b_hi) + (mm(a_hi, b_lo) + mm(a_lo, b_hi))


def _mm(a, w, precise):
    return _dot3(a, w) if precise else _dot(a.astype(BF16), w)


def _masked_softmax(s, mask):
    s = jnp.where(mask, s, NEG)
    m = jnp.max(s, axis=-1, keepdims=True)
    e = jnp.where(mask, jnp.exp(s - m), 0.0)
    return e / jnp.maximum(jnp.sum(e, axis=-1, keepdims=True), 1e-30)


def _head_norm(y, gain, bd):
    ss = _split_dot(y * y, bd)
    return y * lax.rsqrt(ss * (1.0 / HEAD_DIM) + EPS) * gain


def _select_top(score, n_pick, idx_f, axis):
    sel = jnp.zeros(score.shape, jnp.bool_)
    for _ in range(n_pick):
        m = jnp.max(score, axis=axis, keepdims=True)
        first = jnp.min(jnp.where(score == m, idx_f, 1e9), axis=axis, keepdims=True)
        hit = idx_f == first
        sel = jnp.logical_or(sel, hit)
        score = jnp.where(hit, -jnp.inf, score)
    return sel


def _nsa_select(imp, blk, cur, axis):
    forced = (blk == 0) | (blk == cur) | (blk == cur - 1)
    valid = blk <= cur
    score = jnp.where(forced, POS_BIG, jnp.where(valid, imp, -jnp.inf))
    sel = _select_top(score, N_SEL, blk.astype(F32), axis)
    return jnp.logical_and(sel, valid)


def _online_update(m_ref, l_ref, acc_ref, i, s, vt, first):
    m_cur = jnp.max(s, axis=0, keepdims=True)
    if first:
        p = jnp.exp(s - m_cur)
        m_ref[i] = m_cur
        l_ref[i] = jnp.sum(p, axis=0, keepdims=True)
        acc_ref[i] = _dot(vt, p.astype(BF16))
    else:
        m_prev = m_ref[i]
        m_new = jnp.maximum(m_prev, m_cur)
        alpha = jnp.exp(m_prev - m_new)
        p = jnp.exp(s - m_new)
        m_ref[i] = m_new
        l_ref[i] = alpha * l_ref[i] + jnp.sum(p, axis=0, keepdims=True)
        acc_ref[i] = alpha * acc_ref[i] + _dot(vt, p.astype(BF16))


def _inproj_kernel(precise, x_ref, cos_ref, sin_ref, gn_ref, wqat, wkvat, wgnt, wqbt, wkbt, wvbt, wgm, g_col, *outs):
    (qa_o, kcr_o, vcr_o, kc_o, vc_o, ks_o, vs_o, kw_o, vw_o, ksr_o, vsb_o, kwr_o, vwb_o,
     gn_o, qb_o, kb_o, vb_o, kbr_o, vbb_o, gm_o) = outs
    x = x_ref[...]
    xn = x * lax.rsqrt(jnp.mean(x * x, axis=-1, keepdims=True) + EPS) * gn_ref[...]
    xa = xn if precise else xn.astype(BF16)
    mm_t = (lambda wt: _dot3(wt, xn, nt=True)) if precise else (lambda wt: _dot_nt(wt, xa))
    half = HEAD_DIM // 2
    cos = cos_ref[...][None]
    sin = sin_ref[...][None]

    def tile(j):
        return slice(j * LANES, (j + 1) * LANES)

    def qk_t(wt_ref, j, gain_col):
        y = mm_t(wt_ref[tile(j), :])
        y = y.reshape(2, HEAD_DIM, y.shape[-1])
        y = y * lax.rsqrt(jnp.mean(y * y, axis=1, keepdims=True) + EPS) * gain_col[None]
        x1, x2 = y[:, :half], y[:, half:]
        return jnp.concatenate([x1 * cos - x2 * sin, x2 * cos + x1 * sin], axis=1).reshape(LANES, -1)

    for j in range(4):
        qa_o[0, tile(j), :] = qk_t(wqat, j, g_col[:, 0:1]).astype(qa_o.dtype)
        qb_o[0, tile(j), :] = qk_t(wqbt, j, g_col[:, 1:2]).astype(qb_o.dtype)
    nsa_kv = ((kc_o, vc_o, kcr_o, vcr_o), (ks_o, vs_o, ksr_o, vsb_o), (kw_o, vw_o, kwr_o, vwb_o))
    for i, (k_o, v_o, k2_o, v2_o) in enumerate(nsa_kv):
        k = qk_t(wkvat, 2 * i, g_col[:, 2 + i:3 + i])
        v = mm_t(wkvat[tile(2 * i + 1), :])
        k_o[0] = k
        v_o[0] = v
        if i == 0:
            k2_o[...] = k.T
            v2_o[...] = v.T
        else:
            k2_o[0, 0] = k.T.astype(BF16)
            v2_o[0, 0] = v.astype(BF16)
    gn_o[0] = jax.nn.sigmoid(mm_t(wgnt[...]))
    for j in range(4):
        k = qk_t(wkbt, j, g_col[:, 5:6])
        v = mm_t(wvbt[tile(j), :])
        kb_o[0, tile(j), :] = k
        vb_o[0, tile(j), :] = v
        kbr_o[0, 0, :, tile(j)] = k.T.astype(BF16)
        vbb_o[0, 0, tile(j), :] = v.astype(BF16)
    for j in range(2 * D_MODEL // LANES):
        gm_o[:, tile(j)] = jax.nn.sigmoid(_mm(xa, wgm[:, tile(j)], precise))


_INPROJ_W = ('wqat', 'wkvat', 'wgnt', 'wqbt', 'wkbt', 'wvbt', 'wgm')
_INPROJ_OUT = ('qa', 'kc_row', 'vc_row', 'kc', 'vc', 'ks', 'vs', 'kw', 'vw', 'ks_rows', 'vs_blk', 'kw_rows', 'vw_blk',
               'gn', 'qb', 'kb', 'vb', 'kb_rows', 'vb_blk', 'gm')


def _in_proj(x2d, n_batch, rope, prm, tm, precise):
    n = x2d.shape[0]
    t = n // n_batch
    nb = t // tm
    cos_t, sin_t = rope
    row = lambda i: (i, 0)
    const = lambda i: (0, 0)
    pos_t = lambda i: (0, i % nb)
    tr = lambda i: (i // nb, 0, i % nb)
    blk = lambda i: (i // nb, i % nb, 0, 0)
    ws = [prm[k + ('_f32' if precise else '')] for k in _INPROJ_W] + [prm['g_col']]
    in_specs = ([pl.BlockSpec((tm, D_MODEL), row), pl.BlockSpec((HEAD_DIM // 2, tm), pos_t),
                 pl.BlockSpec((HEAD_DIM // 2, tm), pos_t), pl.BlockSpec((1, D_MODEL), const)]
                + [pl.BlockSpec(w.shape, const) for w in ws])
    qdt = F32 if precise else BF16
    rows_out = lambda w, dt: (jax.ShapeDtypeStruct((n, w), dt), pl.BlockSpec((tm, w), row))
    t_out = lambda w, dt=F32: (jax.ShapeDtypeStruct((n_batch, w, t), dt), pl.BlockSpec((1, w, tm), tr))
    r_out = lambda w: (jax.ShapeDtypeStruct((n_batch, nb, tm, w), BF16), pl.BlockSpec((1, 1, tm, w), blk))
    b_out = lambda w: (jax.ShapeDtypeStruct((n_batch, nb, w, tm), BF16), pl.BlockSpec((1, 1, w, tm), blk))
    outs = ([t_out(512, qdt), rows_out(LANES, F32), rows_out(LANES, F32)] + [t_out(LANES)] * 6
            + [r_out(LANES), b_out(LANES), r_out(LANES), b_out(LANES)]
            + [t_out(LANES), t_out(512, qdt), t_out(512), t_out(512), r_out(512), b_out(512),
               rows_out(2 * D_MODEL, F32)])
    res = pl.pallas_call(
        functools.partial(_inproj_kernel, precise), grid=(n // tm,), in_specs=in_specs,
        out_specs=[o[1] for o in outs], out_shape=[o[0] for o in outs], compiler_params=_cparams("parallel"),
        name="in_proj_sample" if precise else "in_proj",
    )(x2d, cos_t, sin_t, prm['norm_attn'], *ws)
    return dict(zip(_INPROJ_OUT, res))


def _compress_hidden(xs, pos_a, pos_b, w1a, w1b, precise):
    lo = _lane_iota((LANES, LANES)) < HALF
    t0, t1 = [], []
    for l in range(0, CMP_STRIDE, 2):
        a, b = xs[l], xs[l + 1]
        t0.append(jnp.where(lo, a, pltpu.roll(b, HALF, 1)))
        t1.append(jnp.where(lo, pltpu.roll(a, HALF, 1), b))
    x = jnp.concatenate([jnp.concatenate(t0, axis=1), jnp.concatenate(t1, axis=1)], axis=0)
    first = _mm(x + pos_a, w1a, precise)
    second = _mm(x + pos_b, w1b, precise)
    n = LANES
    shifted = jnp.concatenate([pltpu.roll(second[:n], n - 1, 0), pltpu.roll(second[n:], n - 1, 0)], axis=0)
    return jax.nn.gelu(first + shifted)


_CW_NAMES = ('pka', 'pkb', 'k1a', 'k1b', 'k2g0', 'k2g1', 'pva', 'pvb', 'v1a', 'v1b', 'v2t', 'gkc', 'bd')


def _compress_prompt_kernel(k_ref, v_ref, *rest):
    cw = dict(zip(_CW_NAMES, rest[:len(_CW_NAMES)]))
    kc_o, vct_o = rest[len(_CW_NAMES):]
    n = k_ref.shape[1] // CMP_STRIDE

    def hidden(ref, p):
        xs = [ref[0, pl.ds(l, n, stride=CMP_STRIDE), :] for l in range(CMP_STRIDE)]
        return _compress_hidden(xs, cw['p%sa' % p][...], cw['p%sb' % p][...], cw['%s1a' % p][...],
                                cw['%s1b' % p][...], False).astype(BF16)

    hk = hidden(k_ref, 'k')
    kc = _dot(hk[:LANES], cw['k2g0'][...]) + _dot(hk[LANES:], cw['k2g1'][...])
    kc_o[0] = _head_norm(kc, cw['gkc'][...], cw['bd'][...])
    hv = hidden(v_ref, 'v')
    w2t = cw['v2t'][...]
    vct_o[0] = jnp.concatenate([_dot_nt(w2t, hv[:LANES]), _dot_nt(w2t, hv[LANES:])], axis=0)


def _compress_prompt(k_c, v_c, prm):
    b, t, _ = k_c.shape
    assert t // CMP_STRIDE == LANES
    ws = [prm[k] for k in _CW_NAMES]
    per_b = lambda i: (i, 0, 0)
    const = lambda i: (0, 0)
    return pl.pallas_call(
        _compress_prompt_kernel, grid=(b,),
        in_specs=[pl.BlockSpec((1, t, LANES), per_b)] * 2 + [pl.BlockSpec(w.shape, const) for w in ws],
        out_specs=[pl.BlockSpec((1, LANES, LANES), per_b)] * 2,
        out_shape=[jax.ShapeDtypeStruct((b, LANES, LANES), F32)] * 2,
        compiler_params=_cparams("parallel"), name="compress_prompt",
    )(k_c, v_c, *ws)


def _nsa_prompt_kernel(q_ref, g_ref, kc_ref, vct_ref, ks_ref, vs_ref, kw_ref, vw_ref, ovt_ref,
                       o_ref, qaug_ref, m_ref, l_ref, acc_ref):
    qt = Q_TILE
    cols = NSA_HEADS * qt
    kb = ks_ref.shape[2]
    s0 = pl.program_id(1) * qt
    qx = q_ref[0] * ATT_SCALE
    zero = jnp.zeros((HEAD_DIM, qt), qx.dtype)
    blocks = []
    for h in range(NSA_HEADS):
        qh = qx[h * HEAD_DIM:(h + 1) * HEAD_DIM]
        blocks.append(jnp.concatenate([qh, zero] if h < NSA_GROUP else [zero, qh], axis=0))
    qst = jnp.concatenate(blocks, axis=1)
    t_col = s0 + (_lane_iota((1, cols)) & (qt - 1))

    n_c = kc_ref.shape[1] - 1
    s_c = _dot(kc_ref[0].astype(BF16), qst)
    c_end = _row_iota(s_c.shape) * CMP_STRIDE + (CMP_BLOCK - 1)
    ok = c_end <= jnp.minimum(t_col, (n_c - 1) * CMP_STRIDE + CMP_BLOCK - 1)
    s_c = jnp.where(ok, s_c, NEG)
    e = jnp.where(ok, jnp.exp(s_c - jnp.max(s_c, axis=0, keepdims=True)), 0.0)
    p_c = e / jnp.maximum(jnp.sum(e, axis=0, keepdims=True), 1e-30)
    o_c = _dot(vct_ref[0].astype(BF16), p_c.astype(BF16))
    g4 = NSA_GROUP * qt
    imp = jnp.concatenate([sum(p_c[:, g * g4 + r * qt: g * g4 + (r + 1) * qt] for r in range(NSA_GROUP))
                           for g in range(NSA_KV_HEADS)], axis=1)
    imp = _dot_split(ovt_ref[...], imp)
    cur = (s0 + (_lane_iota((1, 2 * qt)) & (qt - 1))) // SEL_BLOCK
    sel = _nsa_select(imp, _row_iota(imp.shape), cur, 0)

    bias = jnp.where(sel, 0.0, NEG).astype(BF16)
    bias = jnp.concatenate([bias[:, :qt]] * NSA_GROUP + [bias[:, qt:]] * NSA_GROUP, axis=1)
    qaug_ref[...] = jnp.concatenate([qst, bias], axis=0)
    per_kb = kb // SEL_BLOCK

    def sel_block(c, causal, first):
        key = _row_iota((kb, LANES))
        onehot = jnp.where(_lane_iota((kb, LANES)) - c * per_kb == key // SEL_BLOCK, 1.0, 0.0).astype(BF16)
        s = _dot(jnp.concatenate([ks_ref[0, c], onehot], axis=1), qaug_ref[...])
        if causal:
            s = jnp.where(c * kb + _row_iota(s.shape) <= t_col, s, NEG)
        _online_update(m_ref, l_ref, acc_ref, 0, s, vs_ref[0, c], first)

    c_last = s0 // kb
    sel_block(0, True, True)

    def body(c, carry):
        sel_block(c, False, False)
        return carry

    lax.fori_loop(1, c_last, body, 0)

    @pl.when(c_last > 0)
    def _():
        sel_block(c_last, True, False)

    o_s = acc_ref[0] / jnp.maximum(l_ref[0], 1e-30)

    w0 = s0 // kb - (WINDOW // kb)
    n_wb = WINDOW // kb + 1
    idx = [jnp.maximum(w0 + j, 0) for j in range(n_wb)]
    s_parts = []
    for j in range(n_wb):
        base = jnp.where(w0 + j < 0, FAR, (w0 + j) * kb)
        dist = t_col - (base + _row_iota((kb, cols)))
        s_j = _dot(kw_ref[0, idx[j]], qst)
        s_parts.append(jnp.where(dist >= 0, jnp.where(dist < WINDOW, s_j, NEG), NEG))
    m_w = functools.reduce(jnp.maximum, [jnp.max(s_j, axis=0, keepdims=True) for s_j in s_parts])
    p_parts = [jnp.exp(s_j - m_w) for s_j in s_parts]
    l_w = sum(jnp.sum(p_j, axis=0, keepdims=True) for p_j in p_parts)
    o_w = sum(_dot(vw_ref[0, idx[j]], p_parts[j].astype(BF16)) for j in range(n_wb)) / l_w

    g = g_ref[0]
    gate = lambda j: jnp.concatenate([g[j * NSA_HEADS + h: j * NSA_HEADS + h + 1] for h in range(NSA_HEADS)], axis=1)
    o = gate(0) * o_c + gate(1) * o_s + gate(2) * o_w
    heads = [o[(h // NSA_GROUP) * HEAD_DIM:(h // NSA_GROUP + 1) * HEAD_DIM, h * qt:(h + 1) * qt]
             for h in range(NSA_HEADS)]
    o_ref[0] = jnp.concatenate(heads, axis=0).T.astype(BF16)


def _nsa_prompt(q_a, g_nsa, kc, vct, ks_rows, vs_blk, kw_rows, vw_blk, prm):
    b, w, t = q_a.shape
    qt = Q_TILE
    kb = ks_rows.shape[2]
    assert kb % qt == 0 and WINDOW % kb == 0 and kb % SEL_BLOCK == 0 and t // SEL_BLOCK <= LANES
    cols = NSA_HEADS * qt
    per_q = lambda i, j: (i, 0, j)
    per_b = lambda i, j: (i, 0, 0)
    per_b4 = lambda i, j: (i, 0, 0, 0)
    return pl.pallas_call(
        _nsa_prompt_kernel, grid=(b, t // qt),
        in_specs=[pl.BlockSpec((1, w, qt), per_q), pl.BlockSpec((1, LANES, qt), per_q),
                  pl.BlockSpec((1, LANES, LANES), per_b), pl.BlockSpec((1, LANES, LANES), per_b),
                  pl.BlockSpec((1,) + ks_rows.shape[1:], per_b4), pl.BlockSpec((1,) + vs_blk.shape[1:], per_b4),
                  pl.BlockSpec((1,) + kw_rows.shape[1:], per_b4), pl.BlockSpec((1,) + vw_blk.shape[1:], per_b4),
                  pl.BlockSpec(prm['ovt'].shape, lambda i, j: (0, 0))],
        out_specs=pl.BlockSpec((1, qt, w), lambda i, j: (i, j, 0)),
        out_shape=jax.ShapeDtypeStruct((b, t, w), BF16),
        scratch_shapes=[pltpu.VMEM((2 * LANES, cols), BF16), pltpu.VMEM((1, 1, cols), F32),
                        pltpu.VMEM((1, 1, cols), F32), pltpu.VMEM((1, LANES, cols), F32)],
        compiler_params=_cparams("arbitrary", "arbitrary"), name="nsa_prompt",
    )(q_a, g_nsa, kc, vct, ks_rows, vs_blk, kw_rows, vw_blk, prm['ovt'])


def _moba_prompt_kernel(q_ref, kt_ref, kb_ref, vb_ref, o_ref, kmean_ref, qaug_ref, m_ref, l_ref, acc_ref):
    qt = MOBA_BLOCK
    cur = pl.program_id(1)
    n_b = kb_ref.shape[1]
    n_pair = MOBA_HEADS // 2
    cols = 2 * qt

    @pl.when(cur == 0)
    def _():
        lane = _lane_iota((kt_ref.shape[1], LANES))
        km = jnp.zeros(lane.shape, F32)
        for n in range(n_b):
            km = jnp.where(lane == n, jnp.mean(kt_ref[0, :, n * qt:(n + 1) * qt], axis=-1, keepdims=True), km)
        kmean_ref[...] = km.T

    causal = _row_iota((qt, cols)) <= (_lane_iota((qt, cols)) & (qt - 1))
    for j in range(n_pair):
        tl = slice(j * LANES, (j + 1) * LANES)
        qj = q_ref[0, tl, :]
        zero = jnp.zeros((HEAD_DIM, qt), qj.dtype)
        qst = jnp.concatenate([jnp.concatenate([qj[:HEAD_DIM], zero], axis=0),
                               jnp.concatenate([zero, qj[HEAD_DIM:]], axis=0)], axis=1)
        km_hi, km_lo = _split(kmean_ref[:, tl])
        s_g = _dot(km_hi, qst) + _dot(km_lo, qst)
        blk = _row_iota(s_g.shape)
        past = blk < cur
        sel = _select_top(jnp.where(past, s_g, -jnp.inf), MOBA_TOPK, blk.astype(F32), 0)
        bias = jnp.where(jnp.logical_and(sel, past), 0.0, NEG).astype(BF16)
        qs = qst * ATT_SCALE
        qaug_ref[j] = jnp.concatenate([qs, bias], axis=0)
        s = jnp.where(causal, _dot(kb_ref[0, cur, :, tl], qs), NEG)
        _online_update(m_ref, l_ref, acc_ref, j, s, vb_ref[0, cur, tl, :], True)

    def body(n, carry):
        onehot = jnp.where(_lane_iota((qt, LANES)) == n, 1.0, 0.0).astype(BF16)
        for j in range(n_pair):
            tl = slice(j * LANES, (j + 1) * LANES)
            s = _dot(jnp.concatenate([kb_ref[0, n, :, tl], onehot], axis=1), qaug_ref[j])
            _online_update(m_ref, l_ref, acc_ref, j, s, vb_ref[0, n, tl, :], False)
        return carry

    lax.fori_loop(0, cur, body, 0)
    for j in range(n_pair):
        o = acc_ref[j] / jnp.maximum(l_ref[j], 1e-30)
        o = jnp.concatenate([o[:HEAD_DIM, :qt], o[HEAD_DIM:, qt:]], axis=0)
        o_ref[0, :, j * LANES:(j + 1) * LANES] = o.T.astype(BF16)


def _moba_prompt(q_b, kt, kb_rows, vb_blk):
    b, w, t = q_b.shape
    qt = MOBA_BLOCK
    assert kb_rows.shape[2] == qt and t // qt <= LANES
    per_b = lambda i, j: (i, 0, 0)
    per_b4 = lambda i, j: (i, 0, 0, 0)
    n_pair = MOBA_HEADS // 2
    return pl.pallas_call(
        _moba_prompt_kernel, grid=(b, t // qt),
        in_specs=[pl.BlockSpec((1, w, qt), lambda i, j: (i, 0, j)), pl.BlockSpec((1, w, t), per_b),
                  pl.BlockSpec((1,) + kb_rows.shape[1:], per_b4), pl.BlockSpec((1,) + vb_blk.shape[1:], per_b4)],
        out_specs=pl.BlockSpec((1, qt, w), lambda i, j: (i, j, 0)),
        out_shape=jax.ShapeDtypeStruct((b, t, w), BF16),
        scratch_shapes=[pltpu.VMEM((LANES, w), F32), pltpu.VMEM((n_pair, 2 * LANES, 2 * qt), BF16),
                        pltpu.VMEM((n_pair, 1, 2 * qt), F32), pltpu.VMEM((n_pair, 1, 2 * qt), F32),
                        pltpu.VMEM((n_pair, LANES, 2 * qt), F32)],
        compiler_params=_cparams("arbitrary", "arbitrary"), name="moba_prompt",
    )(q_b, kt, kb_rows, vb_blk)


def _merge_kernel(precise, x_ref, oa_ref, ob_ref, gm_ref, wa_ref, wb_ref, wo_ref, nf_ref, wr_hi, wr_lo, br_ref,
                  h_o, xn_o, lg_o):
    gm = gm_ref[...]
    u = (gm[:, :D_MODEL] * _mm(oa_ref[...], wa_ref[...], precise)
         + gm[:, D_MODEL:] * _mm(ob_ref[...], wb_ref[...], precise))
    h = x_ref[...] + _mm(u, wo_ref[...], precise)
    h_o[...] = h
    xn = h * lax.rsqrt(jnp.mean(h * h, axis=-1, keepdims=True) + EPS) * nf_ref[...]
    xn_o[...] = xn
    lg_o[...] = _dot3(xn, (wr_hi[...], wr_lo[...])) + br_ref[...]


def _merge(x2d, o_a, o_b, g_mrg, prm, tm, precise):
    n = x2d.shape[0]
    row = lambda i: (i, 0)
    const = lambda i: (0, 0)
    sfx = '_f32' if precise else ''
    ws = [prm[k] for k in ('w_up_a' + sfx, 'w_up_b' + sfx, 'w_out' + sfx, 'norm_ffn', 'wr_hi', 'wr_lo', 'b_router')]
    return pl.pallas_call(
        functools.partial(_merge_kernel, precise), grid=(n // tm,),
        in_specs=[pl.BlockSpec((tm, D_MODEL), row), pl.BlockSpec((tm, 512), row), pl.BlockSpec((tm, 512), row),
                  pl.BlockSpec((tm, 2 * D_MODEL), row)] + [pl.BlockSpec(w.shape, const) for w in ws],
        out_specs=[pl.BlockSpec((tm, D_MODEL), row), pl.BlockSpec((tm, D_MODEL), row), pl.BlockSpec((tm, LANES), row)],
        out_shape=[jax.ShapeDtypeStruct((n, D_MODEL), F32), jax.ShapeDtypeStruct((n, D_MODEL), F32),
                   jax.ShapeDtypeStruct((n, LANES), F32)],
        compiler_params=_cparams("parallel"), name="merge_router_sample" if precise else "merge_router",
    )(x2d, o_a, o_b, g_mrg, *ws)


def _moe_kernel(be_ref, bv_ref, x_ref, g_ref, wgu_ref, bgu_ref, wd_ref, bd_ref, y_ref, wgu16_ref, wd16_ref):
    i = pl.program_id(0)

    @pl.when(jnp.logical_or(i == 0, be_ref[i] != be_ref[jnp.maximum(i - 1, 0)]))
    def _():
        wgu16_ref[...] = wgu_ref[0].astype(BF16)
        wd16_ref[...] = wd_ref[0].astype(BF16)

    @pl.when(bv_ref[i] > 0)
    def _():
        h = _dot(x_ref[...].astype(BF16), wgu16_ref[...]) + bgu_ref[0]
        gate = jnp.minimum(h[:, :D_FF], SWIGLU_LIMIT)
        up = jnp.clip(h[:, D_FF:], -SWIGLU_LIMIT, SWIGLU_LIMIT)
        act = (up + 1.0) * gate * jax.nn.sigmoid(SWIGLU_ALPHA * gate)
        y = _dot(act.astype(BF16), wd16_ref[...]) + bd_ref[0]
        y_ref[...] = y * g_ref[...]

    @pl.when(bv_ref[i] == 0)
    def _():
        y_ref[...] = jnp.zeros(y_ref.shape, F32)


def _moe_blocks(block_e, block_valid, x_sorted, slot_gate, w_gu, b_gu, w_down, b_down):
    cap = x_sorted.shape[0]
    tm = MOE_TILE
    row = lambda i, be, bv: (i, 0)
    exp = lambda i, be, bv: (be[i], 0, 0)
    grid_spec = pltpu.PrefetchScalarGridSpec(
        num_scalar_prefetch=2, grid=(cap // tm,),
        in_specs=[pl.BlockSpec((tm, D_MODEL), row), pl.BlockSpec((tm, 1), row),
                  pl.BlockSpec((1, D_MODEL, 2 * D_FF), exp), pl.BlockSpec((1, 1, 2 * D_FF), exp),
                  pl.BlockSpec((1, D_FF, D_MODEL), exp), pl.BlockSpec((1, 1, D_MODEL), exp)],
        out_specs=pl.BlockSpec((tm, D_MODEL), row),
        scratch_shapes=[pltpu.VMEM((D_MODEL, 2 * D_FF), BF16), pltpu.VMEM((D_FF, D_MODEL), BF16)])
    return pl.pallas_call(
        _moe_kernel, grid_spec=grid_spec, out_shape=jax.ShapeDtypeStruct((cap, D_MODEL), F32),
        compiler_params=_cparams("arbitrary"), name="moe_experts",
    )(block_e, block_valid, x_sorted, slot_gate, w_gu, b_gu, w_down, b_down)


def _route(logits):
    n = logits.shape[0]
    top_v, top_e = lax.top_k(logits, TOP_K)
    gates = jax.nn.softmax(top_v, axis=-1)
    onehot = jnp.sum((top_e[:, :, None] == jnp.arange(N_EXPERTS)[None, None, :]).astype(jnp.int32), axis=1)
    csum = jnp.cumsum(onehot, axis=0)
    counts = csum[-1]
    padded = (counts + MOE_TILE - 1) // MOE_TILE * MOE_TILE
    pend = jnp.cumsum(padded)
    rank = jnp.take_along_axis(csum - onehot, top_e, axis=1)
    slot = (pend - padded)[top_e] + rank
    n_blocks = -(-(n * TOP_K) // MOE_TILE) + N_EXPERTS
    cap = n_blocks * MOE_TILE
    starts = jnp.arange(n_blocks, dtype=jnp.int32) * MOE_TILE
    block_e = jnp.minimum(jnp.sum((pend[None, :] <= starts[:, None]).astype(jnp.int32), axis=1), N_EXPERTS - 1)
    block_valid = (starts < pend[-1]).astype(jnp.int32)
    tok = jnp.broadcast_to(jnp.arange(n, dtype=jnp.int32)[:, None], (n, TOP_K))
    _, s_tok, s_gate = lax.sort((slot.reshape(-1), tok.reshape(-1), gates.reshape(-1)), num_keys=1)
    per_slot = lambda v: jnp.broadcast_to(v[block_e][:, None], (n_blocks, MOE_TILE)).reshape(cap)
    rank_in_e = jnp.arange(cap, dtype=jnp.int32) - per_slot(pend - padded)
    used = rank_in_e < per_slot(counts)
    src = jnp.clip(per_slot(jnp.cumsum(counts) - counts) + rank_in_e, 0, n * TOP_K - 1)
    slot_tok = jnp.where(used, s_tok[src], 0)
    slot_gate = jnp.where(used, s_gate[src], 0.0)
    return slot, slot_tok, slot_gate.reshape(cap, 1), block_e, block_valid


def _column(ref, b):
    x = ref[...]
    return jnp.sum(jnp.where(_lane_iota(x.shape) == b, x, 0.0), axis=-1, keepdims=True)


def _fold_lanes(x):
    return sum(x[..., i * LANES:(i + 1) * LANES] for i in range(x.shape[-1] // LANES))


def _gqa_scores(q3, kt):
    g = NSA_GROUP
    return jnp.concatenate([jnp.sum(q3[i * g:(i + 1) * g] * kt[i * HEAD_DIM:(i + 1) * HEAD_DIM][None], axis=1)
                            for i in range(NSA_KV_HEADS)], axis=0)


def _gqa_weighted(p, vt):
    g = NSA_GROUP
    return jnp.concatenate([p[i * g:(i + 1) * g][:, None, :] * vt[i * HEAD_DIM:(i + 1) * HEAD_DIM][None]
                            for i in range(NSA_KV_HEADS)], axis=0)


_CWS_NAMES = ('pka', 'pkb', 'k1a_hi', 'k1a_lo', 'k1b_hi', 'k1b_lo', 'k2t',
              'pva', 'pvb', 'v1a_hi', 'v1a_lo', 'v1b_hi', 'v1b_lo', 'v2t', 'gkc_col')


def _nsa_sample_kernel(pt_ref, qt_ref, g_ref, ksn_ref, vsn_ref, kwn_ref, vwn_ref, wk_ref, wv_ref, *rest):
    n_pg = (len(rest) - len(_CWS_NAMES) - 3) // 4
    ck_pages = rest[0:n_pg]
    cv_pages = rest[n_pg:2 * n_pg]
    sk_pages = rest[2 * n_pg:3 * n_pg]
    sv_pages = rest[3 * n_pg:4 * n_pg]
    cw = dict(zip(_CWS_NAMES, rest[4 * n_pg:4 * n_pg + len(_CWS_NAMES)]))
    ov_ref, e_ref, o_ref, rows_ref = rest[4 * n_pg + len(_CWS_NAMES):]
    b = pl.program_id(0)
    past = n_pg * PAGE_SIZE

    def compress_t(pages, p):
        for i, pg in enumerate(pages):
            rows_ref[i * PAGE_SIZE:(i + 1) * PAGE_SIZE, :] = pg[0].T
        xs = [rows_ref[pl.ds(l, past // CMP_STRIDE, stride=CMP_STRIDE), :] for l in range(CMP_STRIDE)]
        h = _compress_hidden(xs, cw['p%sa' % p][...], cw['p%sb' % p][...],
                             (cw['%s1a_hi' % p][...], cw['%s1a_lo' % p][...]),
                             (cw['%s1b_hi' % p][...], cw['%s1b_lo' % p][...]), True)
        w2t = cw['%s2t' % p][...]
        return jnp.concatenate([_dot3(w2t, h[:LANES], nt=True), _dot3(w2t, h[LANES:], nt=True)], axis=0)

    kct = compress_t(ck_pages, 'k').reshape(NSA_KV_HEADS, HEAD_DIM, LANES)
    ss = jnp.sum(kct * kct, axis=1, keepdims=True)
    kct = (kct * lax.rsqrt(ss * (1.0 / HEAD_DIM) + EPS) * cw['gkc_col'][...][None]).reshape(LANES, LANES)
    vct = compress_t(cv_pages, 'v')

    q3 = (_column(qt_ref, b) * ATT_SCALE).reshape(NSA_HEADS, HEAD_DIM, 1)

    n_c = (past + 1 - CMP_BLOCK) // CMP_STRIDE + 1
    s_c = _gqa_scores(q3, kct)
    p_c = _masked_softmax(s_c, _lane_iota(s_c.shape) < n_c)
    o_c = jnp.sum(_gqa_weighted(p_c, vct), axis=-1, keepdims=True)
    row = _row_iota((NSA_HEADS, LANES))
    i0 = jnp.sum(p_c[:NSA_GROUP], axis=0, keepdims=True)
    i1 = jnp.sum(p_c[NSA_GROUP:], axis=0, keepdims=True)
    imp = _split_dot(jnp.where(row < NSA_GROUP, i0, i1), ov_ref[...])
    sel = _nsa_select(imp, _lane_iota(imp.shape), past // SEL_BLOCK, -1)
    picked = _dot(jnp.where(sel, 1.0, 0.0).astype(BF16), e_ref[...]) > 0.5

    def attend_with_new(s, valid, kn_ref, vn_ref, weighted):
        kn = _column(kn_ref, b)
        vn = _column(vn_ref, b)
        s = jnp.where(valid, s, NEG)
        s_n = _gqa_scores(q3, kn)
        m = jnp.maximum(jnp.max(s, axis=-1, keepdims=True), s_n)
        p = jnp.where(valid, jnp.exp(s - m), 0.0)
        p_n = jnp.exp(s_n - m)
        num = jnp.sum(weighted(p), axis=-1, keepdims=True) + _gqa_weighted(p_n, vn)
        den = jnp.maximum(jnp.sum(p, axis=-1, keepdims=True) + p_n, 1e-30)
        return num / den[:, :, None]

    s_s = jnp.concatenate([_gqa_scores(q3, pg[0]) for pg in sk_pages], axis=1)
    o_s = attend_with_new(
        s_s, picked, ksn_ref, vsn_ref,
        lambda p: sum(_gqa_weighted(p[:, i * PAGE_SIZE:(i + 1) * PAGE_SIZE], pg[0]) for i, pg in enumerate(sv_pages)))

    wb = wk_ref.shape[2]
    s_w = _gqa_scores(q3, wk_ref[0])
    inside = (past - wb + _lane_iota(s_w.shape)) > (past - WINDOW)
    o_w = attend_with_new(s_w, inside, kwn_ref, vwn_ref, lambda p: _fold_lanes(_gqa_weighted(p, wv_ref[0])))

    g = jnp.broadcast_to(g_ref[0], (NSA_HEADS, LANES))
    lane = _lane_iota(g.shape)
    gate = lambda j: jnp.sum(jnp.where(lane == j * NSA_HEADS + row, g, 0.0), axis=-1, keepdims=True)[:, :, None]
    o = (gate(0) * o_c + gate(1) * o_s + gate(2) * o_w).reshape(NSA_HEADS * HEAD_DIM, 1)

    @pl.when(b == 0)
    def _():
        o_ref[...] = jnp.zeros(o_ref.shape, F32)

    o_ref[...] = jnp.where(_lane_iota(o_ref.shape) == b, o, o_ref[...])


def _nsa_sample(page_table, qt, g_nsa, ksn, vsn, kwn, vwn, win_kt, win_vt, cache_ck, cache_cv, cache_skt, cache_svt,
                prm):
    db, n_pg = page_table.shape
    wb = win_kt.shape[2]
    per_b = lambda i, pt: (i, 0, 0)
    const2 = lambda i, pt: (0, 0)

    def page_spec(p):
        return pl.BlockSpec((1, PAGE_SIZE, LANES), lambda i, pt, p=p: (pt[i, p], 0, 0))

    cws = [prm['s_' + k] for k in _CWS_NAMES]
    full = lambda a: pl.BlockSpec(a.shape, const2)
    in_specs = ([full(qt), pl.BlockSpec((1, 1, LANES), per_b), full(ksn), full(vsn), full(kwn), full(vwn),
                 pl.BlockSpec((1, LANES, wb), per_b), pl.BlockSpec((1, LANES, wb), per_b)]
                + [page_spec(p) for p in range(n_pg)] * 4
                + [full(w) for w in cws] + [full(prm['ov']), full(prm['e_key'])])
    grid_spec = pltpu.PrefetchScalarGridSpec(num_scalar_prefetch=1, grid=(db,), in_specs=in_specs,
                                             out_specs=pl.BlockSpec((NSA_HEADS * HEAD_DIM, db), const2),
                                             scratch_shapes=[pltpu.VMEM((n_pg * PAGE_SIZE, LANES), F32)])
    pages = [cache_ck] * n_pg + [cache_cv] * n_pg + [cache_skt] * n_pg + [cache_svt] * n_pg
    return pl.pallas_call(
        _nsa_sample_kernel, grid_spec=grid_spec, out_shape=jax.ShapeDtypeStruct((NSA_HEADS * HEAD_DIM, db), F32),
        compiler_params=_cparams("arbitrary"), name="nsa_sample",
    )(page_table, qt, g_nsa, ksn, vsn, kwn, vwn, win_kt, win_vt, *pages, *cws, prm['ov'], prm['e_key'])


def _moba_sample_kernel(pt_ref, qt_ref, kn_ref, vn_ref, *rest):
    n_pg = (len(rest) - 1) // 2
    k_pages = rest[:n_pg]
    v_pages = rest[n_pg:2 * n_pg]
    o_ref = rest[2 * n_pg]
    b = pl.program_id(0)
    pages_per_blk = MOBA_BLOCK // PAGE_SIZE
    n_b = n_pg // pages_per_blk
    heads3 = lambda x: x.reshape(MOBA_HEADS, HEAD_DIM, x.shape[-1])

    q3 = heads3(_column(qt_ref, b) * ATT_SCALE)
    kn3 = heads3(_column(kn_ref, b))
    vn3 = heads3(_column(vn_ref, b))
    s_pages = [jnp.sum(heads3(pg[0]) * q3, axis=1) for pg in k_pages]
    s_new = jnp.sum(q3 * kn3, axis=1)

    lane = _lane_iota((MOBA_HEADS, LANES))
    s_g = jnp.full((MOBA_HEADS, LANES), -jnp.inf, F32)
    for n in range(n_b):
        tot = sum(s_pages[n * pages_per_blk + i] for i in range(pages_per_blk))
        s_g = jnp.where(lane == n, jnp.sum(tot, axis=-1, keepdims=True), s_g)
    sel = _select_top(s_g, min(MOBA_TOPK, n_b), lane.astype(F32), -1)
    sel_blk = [jnp.sum(jnp.where(jnp.logical_and(sel, lane == n), 1.0, 0.0), axis=-1, keepdims=True) > 0.5
               for n in range(n_b)]

    m = s_new
    for i in range(n_pg):
        m = jnp.maximum(m, jnp.max(jnp.where(sel_blk[i // pages_per_blk], s_pages[i], NEG), axis=-1, keepdims=True))
    p_new = jnp.exp(s_new - m)
    l = p_new
    acc = jnp.zeros((MOBA_HEADS, HEAD_DIM, PAGE_SIZE), F32)
    for i in range(n_pg):
        p = jnp.where(sel_blk[i // pages_per_blk], jnp.exp(s_pages[i] - m), 0.0)
        l = l + jnp.sum(p, axis=-1, keepdims=True)
        acc = acc + heads3(v_pages[i][0]) * p[:, None, :]
    o = (jnp.sum(acc, axis=-1, keepdims=True) + p_new[:, :, None] * vn3) / jnp.maximum(l, 1e-30)[:, :, None]

    @pl.when(b == 0)
    def _():
        o_ref[...] = jnp.zeros(o_ref.shape, F32)

    o_ref[...] = jnp.where(_lane_iota(o_ref.shape) == b, o.reshape(MOBA_HEADS * HEAD_DIM, 1), o_ref[...])


def _moba_sample(page_table, qt, knt, vnt, cache_kt, cache_vt):
    db, n_pg = page_table.shape
    w = qt.shape[0]
    assert (n_pg * PAGE_SIZE) % MOBA_BLOCK == 0
    const2 = lambda i, pt: (0, 0)

    def page_spec(p):
        return pl.BlockSpec((1, w, PAGE_SIZE), lambda i, pt, p=p: (pt[i, p], 0, 0))

    in_specs = [pl.BlockSpec((w, db), const2)] * 3 + [page_spec(p) for p in range(n_pg)] * 2
    grid_spec = pltpu.PrefetchScalarGridSpec(num_scalar_prefetch=1, grid=(db,), in_specs=in_specs,
                                             out_specs=pl.BlockSpec((w, db), const2))
    return pl.pallas_call(
        _moba_sample_kernel, grid_spec=grid_spec, out_shape=jax.ShapeDtypeStruct((w, db), F32),
        compiler_params=_cparams("arbitrary"), name="moba_sample",
    )(page_table, qt, knt, vnt, *([cache_kt] * n_pg), *([cache_vt] * n_pg))


def _split_weight(w):
    c = w * (2.0 ** 16 + 1.0)
    hi = c - (c - w)
    return hi.astype(BF16), (w - hi).astype(BF16)


def _prepare(norm_attn, w_in, g_qa, g_ka, g_kc, cmp_pos_k, cmp_w1_k, cmp_w2_k, cmp_pos_v, cmp_w1_v, cmp_w2_v,
             g_qb, g_kb, w_up_a, w_up_b, w_out, norm_ffn, w_router, b_router):
    qa = NSA_HEADS * HEAD_DIM
    kva = NSA_KV_HEADS * HEAD_DIM
    qb = MOBA_HEADS * HEAD_DIM
    widths = [qa, kva, kva, kva, kva, kva, kva, 3 * NSA_HEADS, qb, qb, qb, 2 * D_MODEL]
    cuts = np.cumsum([0] + widths)
    part = lambda i: w_in[:, cuts[i]:cuts[i + 1]]
    gcols = np.array([h * 3 + j for j in range(3) for h in range(NSA_HEADS)])
    wgnt = jnp.zeros((LANES, D_MODEL), F32).at[:3 * NSA_HEADS].set(part(7)[:, gcols].T)
    two = lambda g: jnp.concatenate([g, g]).reshape(1, LANES)
    lane = np.arange(LANES)
    c_idx = np.arange(LANES)[:, None] * CMP_STRIDE
    s_idx = np.arange(LANES)[None, :] * SEL_BLOCK
    ov = ((c_idx < s_idx + SEL_BLOCK) & (c_idx + CMP_BLOCK > s_idx) & (np.arange(LANES)[:, None] < LANES - 1))
    half_w = CMP_STRIDE * HEAD_DIM

    def w2wide(w2):
        z = jnp.zeros_like(w2)
        return jnp.concatenate([w2, z], axis=1).astype(BF16), jnp.concatenate([z, w2], axis=1).astype(BF16)

    k2g0, k2g1 = w2wide(cmp_w2_k)
    wr_hi, wr_lo = _split_weight(jnp.zeros((D_MODEL, LANES), F32).at[:, :N_EXPERTS].set(w_router))
    prm = {
        'norm_attn': norm_attn.reshape(1, D_MODEL),
        'wqat_f32': part(0).T, 'wkvat_f32': w_in[:, cuts[1]:cuts[7]].T, 'wgnt_f32': wgnt,
        'wqbt_f32': part(8).T, 'wkbt_f32': part(9).T, 'wvbt_f32': part(10).T, 'wgm_f32': part(11),
        'g_col': jnp.stack([g_qa, g_qb, g_ka[0], g_ka[1], g_ka[2], g_kb], axis=1),
        'gkc': two(g_kc),
        'bd': jnp.asarray((lane[:, None] // HEAD_DIM) == (lane[None, :] // HEAD_DIM), BF16),
        'pka': cmp_pos_k[:CMP_STRIDE].reshape(1, half_w), 'pkb': cmp_pos_k[CMP_STRIDE:].reshape(1, half_w),
        'pva': cmp_pos_v[:CMP_STRIDE].reshape(1, half_w), 'pvb': cmp_pos_v[CMP_STRIDE:].reshape(1, half_w),
        'k1a': cmp_w1_k[:half_w].astype(BF16), 'k1b': cmp_w1_k[half_w:].astype(BF16),
        'v1a': cmp_w1_v[:half_w].astype(BF16), 'v1b': cmp_w1_v[half_w:].astype(BF16),
        'k2g0': k2g0, 'k2g1': k2g1, 'v2t': cmp_w2_v.T.astype(BF16),
        'ov': jnp.asarray(ov, BF16), 'ovt': jnp.asarray(ov.T, BF16),
        'w_up_a_f32': w_up_a, 'w_up_b_f32': w_up_b, 'w_out_f32': w_out,
        'norm_ffn': norm_ffn.reshape(1, D_MODEL),
        'wr_hi': wr_hi, 'wr_lo': wr_lo,
        'b_router': jnp.zeros((1, LANES), F32).at[0, :N_EXPERTS].set(b_router),
    }
    for k in _INPROJ_W + ('w_up_a', 'w_up_b', 'w_out'):
        prm[k] = prm[k + '_f32'].astype(BF16)
    for p, w1, w2 in (('k', cmp_w1_k, cmp_w2_k), ('v', cmp_w1_v, cmp_w2_v)):
        for half, w in (('a', w1[:half_w]), ('b', w1[half_w:])):
            prm['s_%s1%s_hi' % (p, half)], prm['s_%s1%s_lo' % (p, half)] = _split_weight(w)
        prm['s_%s2t' % p] = w2.T
        prm['s_p%sa' % p] = prm['p%sa' % p]
        prm['s_p%sb' % p] = prm['p%sb' % p]
    prm['s_gkc_col'] = g_kc.reshape(HEAD_DIM, 1)
    return prm


def _rope_tables(pos):
    half = HEAD_DIM // 2
    inv = ROPE_THETA ** (-np.arange(half, dtype=np.float64) / half)
    ang = inv[:, None] * np.asarray(pos, np.float64)[None, :]
    return jnp.asarray(np.cos(ang), F32), jnp.asarray(np.sin(ang), F32)


def _expand_matrix(n_keys, block):
    e = (np.arange(n_keys)[None, :] // block) == np.arange(LANES)[:, None]
    return jnp.asarray(e, BF16)


def kernel(x_prompt, x_sample, cache_nsa_cmp_k, cache_nsa_cmp_v, cache_nsa_sel_k, cache_nsa_sel_v, cache_nsa_win_k, cache_nsa_win_v, cache_moba_k, cache_moba_v, page_table, norm_attn, w_in, g_qa, g_ka, g_kc, cmp_pos_k, cmp_w1_k, cmp_w2_k, cmp_pos_v, cmp_w1_v, cmp_w2_v, g_qb, g_kb, w_up_a, w_up_b, w_out, norm_ffn, w_router, b_router, w_gu, b_gu, w_down, b_down):
    B, T, _ = x_prompt.shape
    DB, DS, _ = x_sample.shape
    assert DS == 1
    n_pg = page_table.shape[1]
    past = n_pg * PAGE_SIZE
    wb = cache_nsa_win_k.shape[1]

    prm = _prepare(norm_attn, w_in, g_qa, g_ka, g_kc, cmp_pos_k, cmp_w1_k, cmp_w2_k, cmp_pos_v, cmp_w1_v, cmp_w2_v,
                   g_qb, g_kb, w_up_a, w_up_b, w_out, norm_ffn, w_router, b_router)
    prm['e_key'] = _expand_matrix(past, SEL_BLOCK)

    P = _in_proj(x_prompt.reshape(B * T, D_MODEL), B, _rope_tables(np.arange(T)), prm, KEY_BLOCK, False)
    r3 = lambda a: a.reshape(B, T, a.shape[-1])
    kc_cmp, vct_cmp = _compress_prompt(r3(P['kc_row']), r3(P['vc_row']), prm)
    oa_p = _nsa_prompt(P['qa'], P['gn'], kc_cmp, vct_cmp, P['ks_rows'], P['vs_blk'], P['kw_rows'], P['vw_blk'], prm)
    ob_p = _moba_prompt(P['qb'], P['kb'], P['kb_rows'], P['vb_blk'])
    h_p, xn_p, lg_p = _merge(x_prompt.reshape(B * T, D_MODEL), oa_p.reshape(B * T, 512), ob_p.reshape(B * T, 512),
                             P['gm'], prm, 256, False)

    S = _in_proj(x_sample.reshape(DB, D_MODEL), 1, _rope_tables(np.full((DB,), past)), prm, DB, True)
    cols2 = lambda c: jnp.transpose(c, (0, 2, 3, 1)).reshape(c.shape[0], c.shape[2] * c.shape[3], c.shape[1])
    oa_t = _nsa_sample(page_table, S['qa'][0], S['gn'][0].T.reshape(DB, 1, LANES),
                       S['ks'][0], S['vs'][0], S['kw'][0], S['vw'][0],
                       cols2(cache_nsa_win_k), cols2(cache_nsa_win_v),
                       cols2(cache_nsa_cmp_k), cols2(cache_nsa_cmp_v), cols2(cache_nsa_sel_k), cols2(cache_nsa_sel_v),
                       prm)
    ob_t = _moba_sample(page_table, S['qb'][0], S['kb'][0], S['vb'][0], cols2(cache_moba_k), cols2(cache_moba_v))
    h_s, xn_s, lg_s = _merge(x_sample.reshape(DB, D_MODEL), oa_t.T, ob_t.T, S['gm'], prm, DB, True)

    xn_all = jnp.concatenate([xn_p, xn_s], axis=0)
    logits = jnp.concatenate([lg_p, lg_s], axis=0)[:, :N_EXPERTS]
    slot, slot_tok, slot_gate, block_e, block_valid = _route(logits)
    y_slot = _moe_blocks(block_e, block_valid, xn_all[slot_tok], slot_gate,
                         w_gu, b_gu.reshape(N_EXPERTS, 1, 2 * D_FF), w_down, b_down.reshape(N_EXPERTS, 1, D_MODEL))
    n_tok = B * T + DB
    picked = y_slot[slot.T.reshape(-1)].reshape(TOP_K, n_tok, D_MODEL)
    y_prompt = (h_p + sum(picked[k, :B * T] for k in range(TOP_K))).reshape(B, T, D_MODEL)
    y_sample = (h_s + sum(picked[k, B * T:] for k in range(TOP_K))).reshape(DB, DS, D_MODEL)

    def kv4(a):
        n, w, t = a.shape
        return jnp.transpose(a.reshape(n, w // HEAD_DIM, HEAD_DIM, t), (0, 3, 1, 2))

    new4 = lambda a: kv4(a).reshape(DB, 1, a.shape[1] // HEAD_DIM, HEAD_DIM)
    wp = min(WINDOW, T)
    ws = min(WINDOW, wb + DS)
    kw_all = jnp.concatenate([cache_nsa_win_k, new4(S['kw'])], axis=1)[:, -ws:]
    vw_all = jnp.concatenate([cache_nsa_win_v, new4(S['vw'])], axis=1)[:, -ws:]
    return (y_prompt, y_sample,
            kv4(P['kc']), kv4(P['vc']), kv4(P['ks']), kv4(P['vs']),
            kv4(P['kw'][:, :, T - wp:]), kv4(P['vw'][:, :, T - wp:]), kv4(P['kb']), kv4(P['vb']),
            new4(S['kc']), new4(S['vc']), new4(S['ks']), new4(S['vs']), kw_all, vw_all,
            new4(S['kb']), new4(S['vb']))
```

```python
import functools

import numpy as np
import jax
import jax.numpy as jnp
from jax import lax
from jax.experimental import pallas as pl
from jax.experimental.pallas import tpu as pltpu

F32 = jnp.float32
BF16 = jnp.bfloat16

D_MODEL = 1024
PAGE_SIZE = 128
HEAD_DIM = 64
NSA_HEADS = 8
NSA_KV_HEADS = 2
NSA_GROUP = NSA_HEADS // NSA_KV_HEADS
CMP_BLOCK = 32
CMP_STRIDE = 16
CMP_HIDDEN = 256
SEL_BLOCK = 64
N_SEL = 8
WINDOW = 512
MOBA_HEADS = 8
MOBA_BLOCK = 256
MOBA_TOPK = 3
N_EXPERTS = 32
TOP_K = 4
D_FF = D_MODEL
SWIGLU_LIMIT = 7.0
SWIGLU_ALPHA = 1.702
ROPE_THETA = 10000.0
EPS = 1e-6
NEG = -1e30
POS_BIG = 1e30

LANES = 128
HALF = HEAD_DIM
ATT_SCALE = HEAD_DIM ** -0.5
Q_TILE = 128
KEY_BLOCK = MOBA_BLOCK
MOE_TILE = 256
VMEM_LIMIT = 56 * 1024 * 1024
FAR = -(1 << 20)


def _cparams(*sem):
    return pltpu.CompilerParams(dimension_semantics=sem, vmem_limit_bytes=VMEM_LIMIT)


def _lane_iota(shape):
    return lax.broadcasted_iota(jnp.int32, shape, len(shape) - 1)


def _row_iota(shape):
    return lax.broadcasted_iota(jnp.int32, shape, 0)


def _dot(a, b):
    return jnp.dot(a, b, preferred_element_type=F32)


def _dot_nt(a, b):
    return lax.dot_general(a, b, (((1,), (1,)), ((), ())), preferred_element_type=F32)


def _split(a):
    hi = a.astype(BF16)
    lo = (a - hi.astype(F32)).astype(BF16)
    return hi, lo


def _split_dot(a, b):
    hi, lo = _split(a)
    return _dot(hi, b) + _dot(lo, b)


def _dot_split(a, b):
    hi, lo = _split(b)
    return _dot(a, hi) + _dot(a, lo)


def _dot3(a, b, nt=False):
    mm = _dot_nt if nt else _dot
    a_hi, a_lo = _split(a)
    b_hi, b_lo = b if isinstance(b, tuple) else _split(b)
    return mm(a_hi, b_hi) + (mm(a_hi, b_lo) + mm(a_lo, b_hi))


def _mm(a, w, precise):
    return _dot3(a, w) if precise else _dot(a.astype(BF16), w)


def _masked_softmax(s, mask):
    s = jnp.where(mask, s, NEG)
    m = jnp.max(s, axis=-1, keepdims=True)
    e = jnp.where(mask, jnp.exp(s - m), 0.0)
    return e / jnp.maximum(jnp.sum(e, axis=-1, keepdims=True), 1e-30)


def _head_norm(y, gain, bd):
    ss = _split_dot(y * y, bd)
    return y * lax.rsqrt(ss * (1.0 / HEAD_DIM) + EPS) * gain


def _select_top(score, n_pick, idx_f, axis):
    sel = jnp.zeros(score.shape, jnp.bool_)
    for _ in range(n_pick):
        m = jnp.max(score, axis=axis, keepdims=True)
        first = jnp.min(jnp.where(score == m, idx_f, 1e9), axis=axis, keepdims=True)
        hit = idx_f == first
        sel = jnp.logical_or(sel, hit)
        score = jnp.where(hit, -jnp.inf, score)
    return sel


def _nsa_select(imp, blk, cur, axis):
    forced = (blk == 0) | (blk == cur) | (blk == cur - 1)
    valid = blk <= cur
    score = jnp.where(forced, POS_BIG, jnp.where(valid, imp, -jnp.inf))
    sel = _select_top(score, N_SEL, blk.astype(F32), axis)
    return jnp.logical_and(sel, valid)


def _attend_block(m_ref, l_ref, acc_ref, i, kmat, qmat, vt, first, bias=None):
    s = _dot(kmat, qmat)
    if bias is not None:
        s = s + bias
    m_cur = jnp.max(s, axis=0, keepdims=True)
    if first:
        p = jnp.exp(s - m_cur)
        m_ref[i] = m_cur
        l_ref[i] = jnp.sum(p, axis=0, keepdims=True)
        acc_ref[i] = _dot(vt, p.astype(BF16))
    else:
        m_prev = m_ref[i]
        m_new = jnp.maximum(m_prev, m_cur)
        alpha = jnp.exp(m_prev - m_new)
        p = jnp.exp(s - m_new)
        m_ref[i] = m_new
        l_ref[i] = alpha * l_ref[i] + jnp.sum(p, axis=0, keepdims=True)
        acc_ref[i] = alpha * acc_ref[i] + _dot(vt, p.astype(BF16))


def _inproj_kernel(precise, x_ref, cos_ref, sin_ref, gn_ref, wqat, wkvat, wgnt, wqbt, wkbt, wvbt, wgm, g_col, *outs):
    (qa_o, kcr_o, vcr_o, kc_o, vc_o, ks_o, vs_o, kw_o, vw_o, ksr_o, vsb_o, kwr_o, vwb_o,
     gn_o, qb_o, kb_o, vb_o, kbr_o, vbb_o, gm_o) = outs
    x = x_ref[...]
    xn = x * lax.rsqrt(jnp.mean(x * x, axis=-1, keepdims=True) + EPS) * gn_ref[...]
    xa = xn if precise else xn.astype(BF16)
    mm_t = (lambda wt: _dot3(wt, xn, nt=True)) if precise else (lambda wt: _dot_nt(wt, xa))
    half = HEAD_DIM // 2
    cos = cos_ref[...][None]
    sin = sin_ref[...][None]

    def tile(j):
        return slice(j * LANES, (j + 1) * LANES)

    def qk_t(y, gain_col):
        y = y.reshape(2, HEAD_DIM, y.shape[-1])
        y = y * lax.rsqrt(jnp.mean(y * y, axis=1, keepdims=True) + EPS) * gain_col[None]
        x1, x2 = y[:, :half], y[:, half:]
        return jnp.concatenate([x1 * cos - x2 * sin, x2 * cos + x1 * sin], axis=1).reshape(LANES, -1)

    y_qa = mm_t(wqat[...])
    y_qb = mm_t(wqbt[...])
    for j in range(4):
        qa_o[0, tile(j), :] = qk_t(y_qa[tile(j)], g_col[:, 0:1]).astype(qa_o.dtype)
        qb_o[0, tile(j), :] = qk_t(y_qb[tile(j)], g_col[:, 1:2]).astype(qb_o.dtype)
    y_kva = mm_t(wkvat[...])
    nsa_kv = ((kc_o, vc_o, kcr_o, vcr_o), (ks_o, vs_o, ksr_o, vsb_o), (kw_o, vw_o, kwr_o, vwb_o))
    for i, (k_o, v_o, k2_o, v2_o) in enumerate(nsa_kv):
        k = qk_t(y_kva[tile(2 * i)], g_col[:, 2 + i:3 + i])
        v = y_kva[tile(2 * i + 1)]
        k_o[0] = k
        v_o[0] = v
        if i == 0:
            k2_o[...] = k.T
            v2_o[...] = v.T
        else:
            k2_o[0, 0] = k.T.astype(BF16)
            v2_o[0, 0] = v.astype(BF16)
    gn_o[0] = jax.nn.sigmoid(mm_t(wgnt[...]))
    y_kb = mm_t(wkbt[...])
    y_vb = mm_t(wvbt[...])
    vb_o[0] = y_vb
    vbb_o[0, 0] = y_vb.astype(BF16)
    for j in range(4):
        k = qk_t(y_kb[tile(j)], g_col[:, 5:6])
        kb_o[0, tile(j), :] = k
        kbr_o[0, 0, :, tile(j)] = k.T.astype(BF16)
    gm_o[...] = jax.nn.sigmoid(_mm(xa, wgm[...], precise))


_INPROJ_W = ('wqat', 'wkvat', 'wgnt', 'wqbt', 'wkbt', 'wvbt', 'wgm')
_INPROJ_OUT = ('qa', 'kc_row', 'vc_row', 'kc', 'vc', 'ks', 'vs', 'kw', 'vw', 'ks_rows', 'vs_blk', 'kw_rows', 'vw_blk',
               'gn', 'qb', 'kb', 'vb', 'kb_rows', 'vb_blk', 'gm')


def _in_proj(x2d, n_batch, rope, prm, tm, precise):
    n = x2d.shape[0]
    t = n // n_batch
    nb = t // tm
    cos_t, sin_t = rope
    row = lambda i: (i, 0)
    const = lambda i: (0, 0)
    pos_t = lambda i: (0, i % nb)
    tr = lambda i: (i // nb, 0, i % nb)
    blk = lambda i: (i // nb, i % nb, 0, 0)
    ws = [prm[k + ('_f32' if precise else '')] for k in _INPROJ_W] + [prm['g_col']]
    in_specs = ([pl.BlockSpec((tm, D_MODEL), row), pl.BlockSpec((HEAD_DIM // 2, tm), pos_t),
                 pl.BlockSpec((HEAD_DIM // 2, tm), pos_t), pl.BlockSpec((1, D_MODEL), const)]
                + [pl.BlockSpec(w.shape, const) for w in ws])
    qdt = F32 if precise else BF16
    rows_out = lambda w, dt: (jax.ShapeDtypeStruct((n, w), dt), pl.BlockSpec((tm, w), row))
    t_out = lambda w, dt=F32: (jax.ShapeDtypeStruct((n_batch, w, t), dt), pl.BlockSpec((1, w, tm), tr))
    r_out = lambda w: (jax.ShapeDtypeStruct((n_batch, nb, tm, w), BF16), pl.BlockSpec((1, 1, tm, w), blk))
    b_out = lambda w: (jax.ShapeDtypeStruct((n_batch, nb, w, tm), BF16), pl.BlockSpec((1, 1, w, tm), blk))
    outs = ([t_out(512, qdt), rows_out(LANES, F32), rows_out(LANES, F32)] + [t_out(LANES)] * 6
            + [r_out(LANES), b_out(LANES), r_out(LANES), b_out(LANES)]
            + [t_out(LANES), t_out(512, qdt), t_out(512), t_out(512), r_out(512), b_out(512),
               rows_out(2 * D_MODEL, F32)])
    res = pl.pallas_call(
        functools.partial(_inproj_kernel, precise), grid=(n // tm,), in_specs=in_specs,
        out_specs=[o[1] for o in outs], out_shape=[o[0] for o in outs], compiler_params=_cparams("parallel"),
        name="in_proj_sample" if precise else "in_proj",
    )(x2d, cos_t, sin_t, prm['norm_attn'], *ws)
    return dict(zip(_INPROJ_OUT, res))


def _compress_hidden(xs, pos_a, pos_b, w1a, w1b, precise):
    lo = _lane_iota((LANES, LANES)) < HALF
    t0, t1 = [], []
    for l in range(0, CMP_STRIDE, 2):
        a, b = xs[l], xs[l + 1]
        t0.append(jnp.where(lo, a, pltpu.roll(b, HALF, 1)))
        t1.append(jnp.where(lo, pltpu.roll(a, HALF, 1), b))
    x = jnp.concatenate([jnp.concatenate(t0, axis=1), jnp.concatenate(t1, axis=1)], axis=0)
    first = _mm(x + pos_a, w1a, precise)
    second = _mm(x + pos_b, w1b, precise)
    n = LANES
    shifted = jnp.concatenate([pltpu.roll(second[:n], n - 1, 0), pltpu.roll(second[n:], n - 1, 0)], axis=0)
    return jax.nn.gelu(first + shifted)


_CW_NAMES = ('pka', 'pkb', 'k1a', 'k1b', 'k2g0', 'k2g1', 'pva', 'pvb', 'v1a', 'v1b', 'v2t', 'gkc', 'bd')


def _compress_prompt_kernel(k_ref, v_ref, *rest):
    cw = dict(zip(_CW_NAMES, rest[:len(_CW_NAMES)]))
    kc_o, vct_o = rest[len(_CW_NAMES):]
    n = k_ref.shape[1] // CMP_STRIDE

    def hidden(ref, p):
        xs = [ref[0, pl.ds(l, n, stride=CMP_STRIDE), :] for l in range(CMP_STRIDE)]
        return _compress_hidden(xs, cw['p%sa' % p][...], cw['p%sb' % p][...], cw['%s1a' % p][...],
                                cw['%s1b' % p][...], False).astype(BF16)

    hk = hidden(k_ref, 'k')
    kc = _dot(hk[:LANES], cw['k2g0'][...]) + _dot(hk[LANES:], cw['k2g1'][...])
    kc_o[0] = _head_norm(kc, cw['gkc'][...], cw['bd'][...])
    hv = hidden(v_ref, 'v')
    w2t = cw['v2t'][...]
    vct_o[0] = jnp.concatenate([_dot_nt(w2t, hv[:LANES]), _dot_nt(w2t, hv[LANES:])], axis=0)


def _compress_prompt(k_c, v_c, prm):
    b, t, _ = k_c.shape
    assert t // CMP_STRIDE == LANES
    ws = [prm[k] for k in _CW_NAMES]
    per_b = lambda i: (i, 0, 0)
    const = lambda i: (0, 0)
    return pl.pallas_call(
        _compress_prompt_kernel, grid=(b,),
        in_specs=[pl.BlockSpec((1, t, LANES), per_b)] * 2 + [pl.BlockSpec(w.shape, const) for w in ws],
        out_specs=[pl.BlockSpec((1, LANES, LANES), per_b)] * 2,
        out_shape=[jax.ShapeDtypeStruct((b, LANES, LANES), F32)] * 2,
        compiler_params=_cparams("parallel"), name="compress_prompt",
    )(k_c, v_c, *ws)


def _nsa_prompt_kernel(q_ref, g_ref, kc_ref, vct_ref, ks_ref, vs_ref, kw_ref, vw_ref, ovt_ref,
                       o_ref, qaug_ref, oc_ref, m_ref, l_ref, acc_ref):
    qt = Q_TILE
    kb = ks_ref.shape[2]
    s0 = pl.program_id(1) * qt
    t_q = s0 + _lane_iota((1, qt))
    head_cols = lambda h: slice(h * qt, (h + 1) * qt)
    zero = jnp.zeros((HEAD_DIM, qt), BF16)

    per_head = lambda x: jnp.concatenate([x] * NSA_HEADS, axis=1)
    blocks = []
    for h in range(NSA_HEADS):
        qh = q_ref[0, h * HEAD_DIM:(h + 1) * HEAD_DIM, :] * ATT_SCALE
        blocks.append(jnp.concatenate([qh, zero] if h < NSA_GROUP else [zero, qh], axis=0))
    qst = jnp.concatenate(blocks, axis=1)

    n_c = kc_ref.shape[1] - 1
    c_end = _row_iota((LANES, qt)) * CMP_STRIDE + (CMP_BLOCK - 1)
    ok = per_head(jnp.where(c_end <= jnp.minimum(t_q, (n_c - 1) * CMP_STRIDE + CMP_BLOCK - 1), 1.0, 0.0)) > 0.5
    s_c = jnp.where(ok, _dot(kc_ref[0].astype(BF16), qst), NEG)
    e = jnp.where(ok, jnp.exp(s_c - jnp.max(s_c, axis=0, keepdims=True)), 0.0)
    p_c = e / jnp.maximum(jnp.sum(e, axis=0, keepdims=True), 1e-30)
    oc_ref[...] = _dot(vct_ref[0].astype(BF16), p_c.astype(BF16))
    g4 = NSA_GROUP * qt
    imp = jnp.concatenate([sum(p_c[:, g * g4 + r * qt: g * g4 + (r + 1) * qt] for r in range(NSA_GROUP))
                           for g in range(NSA_KV_HEADS)], axis=1)
    imp = _dot_split(ovt_ref[...], imp)
    cur = (s0 + (_lane_iota((1, 2 * qt)) & (qt - 1))) // SEL_BLOCK
    sel = _nsa_select(imp, _row_iota(imp.shape), cur, 0)

    bias = jnp.where(sel, 0.0, NEG).astype(BF16)
    bias = jnp.concatenate([bias[:, :qt]] * NSA_GROUP + [bias[:, qt:]] * NSA_GROUP, axis=1)
    qaug_ref[...] = jnp.concatenate([qst, bias], axis=0)
    per_kb = kb // SEL_BLOCK

    def sel_block(c, causal, first):
        key = _row_iota((kb, LANES))
        onehot = jnp.where(_lane_iota((kb, LANES)) - c * per_kb == key // SEL_BLOCK, 1.0, 0.0).astype(BF16)
        kmat = jnp.concatenate([ks_ref[0, c], onehot], axis=1)
        cbias = per_head(jnp.where(c * kb + _row_iota((kb, qt)) <= t_q, 0.0, NEG)) if causal else None
        _attend_block(m_ref, l_ref, acc_ref, 0, kmat, qaug_ref[...], vs_ref[0, c], first, cbias)

    c_last = s0 // kb
    sel_block(0, True, True)

    def body(c, carry):
        sel_block(c, False, False)
        return carry

    lax.fori_loop(1, c_last, body, 0)

    @pl.when(c_last > 0)
    def _():
        sel_block(c_last, True, False)

    w0 = s0 // kb - (WINDOW // kb)
    n_wb = WINDOW // kb + 1
    for step, j in enumerate([n_wb - 1] + list(range(n_wb - 1))):
        blk = w0 + j
        base = jnp.where(blk < 0, FAR, blk * kb)
        dist = t_q - (base + _row_iota((kb, qt)))
        wbias = jnp.where(dist >= 0, jnp.where(dist < WINDOW, 0.0, NEG), NEG)
        idx = jnp.maximum(blk, 0)
        _attend_block(m_ref, l_ref, acc_ref, 1, kw_ref[0, idx], qaug_ref[:LANES, :], vw_ref[0, idx], step == 0,
                      per_head(wbias))

    g = g_ref[0]
    heads = []
    for h in range(NSA_HEADS):
        cs = head_cols(h)
        rows = slice((h // NSA_GROUP) * HEAD_DIM, (h // NSA_GROUP + 1) * HEAD_DIM)
        gate = lambda j: g[j * NSA_HEADS + h: j * NSA_HEADS + h + 1]
        o_s = acc_ref[0, rows, cs] / jnp.maximum(l_ref[0, :, cs], 1e-30)
        o_w = acc_ref[1, rows, cs] / jnp.maximum(l_ref[1, :, cs], 1e-30)
        heads.append(gate(0) * oc_ref[rows, cs] + gate(1) * o_s + gate(2) * o_w)
    o_ref[0] = jnp.concatenate(heads, axis=0).T.astype(BF16)


def _nsa_prompt(q_a, g_nsa, kc, vct, ks_rows, vs_blk, kw_rows, vw_blk, prm):
    b, w, t = q_a.shape
    qt = Q_TILE
    kb = ks_rows.shape[2]
    assert qt == LANES and kb % qt == 0 and WINDOW % kb == 0 and kb % SEL_BLOCK == 0 and t // SEL_BLOCK <= LANES
    cols = NSA_HEADS * qt
    per_q = lambda i, j: (i, 0, j)
    per_b = lambda i, j: (i, 0, 0)
    per_b4 = lambda i, j: (i, 0, 0, 0)
    return pl.pallas_call(
        _nsa_prompt_kernel, grid=(b, t // qt),
        in_specs=[pl.BlockSpec((1, w, qt), per_q), pl.BlockSpec((1, LANES, qt), per_q),
                  pl.BlockSpec((1, LANES, LANES), per_b), pl.BlockSpec((1, LANES, LANES), per_b),
                  pl.BlockSpec((1,) + ks_rows.shape[1:], per_b4), pl.BlockSpec((1,) + vs_blk.shape[1:], per_b4),
                  pl.BlockSpec((1,) + kw_rows.shape[1:], per_b4), pl.BlockSpec((1,) + vw_blk.shape[1:], per_b4),
                  pl.BlockSpec(prm['ovt'].shape, lambda i, j: (0, 0))],
        out_specs=pl.BlockSpec((1, qt, w), lambda i, j: (i, j, 0)),
        out_shape=jax.ShapeDtypeStruct((b, t, w), BF16),
        scratch_shapes=[pltpu.VMEM((2 * LANES, cols), BF16), pltpu.VMEM((LANES, cols), F32),
                        pltpu.VMEM((2, 1, cols), F32), pltpu.VMEM((2, 1, cols), F32),
                        pltpu.VMEM((2, LANES, cols), F32)],
        compiler_params=_cparams("arbitrary", "arbitrary"), name="nsa_prompt",
    )(q_a, g_nsa, kc, vct, ks_rows, vs_blk, kw_rows, vw_blk, prm['ovt'])


def _moba_prompt_kernel(q_ref, kt_ref, kb_ref, vb_ref, o_ref, kmean_ref, qaug_ref, m_ref, l_ref, acc_ref):
    qt = MOBA_BLOCK
    cur = pl.program_id(1)
    n_b = kb_ref.shape[1]
    n_pair = MOBA_HEADS // 2
    cols = 2 * qt

    @pl.when(cur == 0)
    def _():
        lane = _lane_iota((kt_ref.shape[1], LANES))
        km = jnp.zeros(lane.shape, F32)
        for n in range(n_b):
            km = jnp.where(lane == n, jnp.mean(kt_ref[0, :, n * qt:(n + 1) * qt], axis=-1, keepdims=True), km)
        kmean_ref[...] = km.T

    own_bias = jnp.where(_row_iota((qt, qt)) <= _lane_iota((qt, qt)), 0.0, NEG)
    own_bias = jnp.concatenate([own_bias, own_bias], axis=1)
    for j in range(n_pair):
        tl = slice(j * LANES, (j + 1) * LANES)
        qj = q_ref[0, tl, :]
        zero = jnp.zeros((HEAD_DIM, qt), qj.dtype)
        qst = jnp.concatenate([jnp.concatenate([qj[:HEAD_DIM], zero], axis=0),
                               jnp.concatenate([zero, qj[HEAD_DIM:]], axis=0)], axis=1)
        km_hi, km_lo = _split(kmean_ref[:, tl])
        s_g = _dot(km_hi, qst) + _dot(km_lo, qst)
        blk = _row_iota(s_g.shape)
        past = blk < cur
        sel = _select_top(jnp.where(past, s_g, -jnp.inf), MOBA_TOPK, blk.astype(F32), 0)
        bias = jnp.where(jnp.logical_and(sel, past), 0.0, NEG).astype(BF16)
        qs = qst * ATT_SCALE
        qaug_ref[j] = jnp.concatenate([qs, bias], axis=0)
        _attend_block(m_ref, l_ref, acc_ref, j, kb_ref[0, cur, :, tl], qs, vb_ref[0, cur, tl, :], True, own_bias)

    def body(n, carry):
        onehot = jnp.where(_lane_iota((qt, LANES)) == n, 1.0, 0.0).astype(BF16)
        for j in range(n_pair):
            tl = slice(j * LANES, (j + 1) * LANES)
            _attend_block(m_ref, l_ref, acc_ref, j, jnp.concatenate([kb_ref[0, n, :, tl], onehot], axis=1),
                          qaug_ref[j], vb_ref[0, n, tl, :], False)
        return carry

    lax.fori_loop(0, cur, body, 0)
    for j in range(n_pair):
        o = acc_ref[j] / jnp.maximum(l_ref[j], 1e-30)
        o = jnp.concatenate([o[:HEAD_DIM, :qt], o[HEAD_DIM:, qt:]], axis=0)
        o_ref[0, :, j * LANES:(j + 1) * LANES] = o.T.astype(BF16)


def _moba_prompt(q_b, kt, kb_rows, vb_blk):
    b, w, t = q_b.shape
    qt = MOBA_BLOCK
    assert kb_rows.shape[2] == qt and t // qt <= LANES
    per_b = lambda i, j: (i, 0, 0)
    per_b4 = lambda i, j: (i, 0, 0, 0)
    n_pair = MOBA_HEADS // 2
    return pl.pallas_call(
        _moba_prompt_kernel, grid=(b, t // qt),
        in_specs=[pl.BlockSpec((1, w, qt), lambda i, j: (i, 0, j)), pl.BlockSpec((1, w, t), per_b),
                  pl.BlockSpec((1,) + kb_rows.shape[1:], per_b4), pl.BlockSpec((1,) + vb_blk.shape[1:], per_b4)],
        out_specs=pl.BlockSpec((1, qt, w), lambda i, j: (i, j, 0)),
        out_shape=jax.ShapeDtypeStruct((b, t, w), BF16),
        scratch_shapes=[pltpu.VMEM((LANES, w), F32), pltpu.VMEM((n_pair, 2 * LANES, 2 * qt), BF16),
                        pltpu.VMEM((n_pair, 1, 2 * qt), F32), pltpu.VMEM((n_pair, 1, 2 * qt), F32),
                        pltpu.VMEM((n_pair, LANES, 2 * qt), F32)],
        compiler_params=_cparams("arbitrary", "arbitrary"), name="moba_prompt",
    )(q_b, kt, kb_rows, vb_blk)


def _merge_kernel(precise, x_ref, oa_ref, ob_ref, gm_ref, wa_ref, wb_ref, wo_ref, nf_ref, wr_hi, wr_lo, br_ref,
                  h_o, xn_o, lg_o):
    gm = gm_ref[...]
    u = (gm[:, :D_MODEL] * _mm(oa_ref[...], wa_ref[...], precise)
         + gm[:, D_MODEL:] * _mm(ob_ref[...], wb_ref[...], precise))
    h = x_ref[...] + _mm(u, wo_ref[...], precise)
    h_o[...] = h
    xn = h * lax.rsqrt(jnp.mean(h * h, axis=-1, keepdims=True) + EPS) * nf_ref[...]
    xn_o[...] = xn
    lg_o[...] = _dot3(xn, (wr_hi[...], wr_lo[...])) + br_ref[...]


def _merge(x2d, o_a, o_b, g_mrg, prm, tm, precise):
    n = x2d.shape[0]
    row = lambda i: (i, 0)
    const = lambda i: (0, 0)
    sfx = '_f32' if precise else ''
    ws = [prm[k] for k in ('w_up_a' + sfx, 'w_up_b' + sfx, 'w_out' + sfx, 'norm_ffn', 'wr_hi', 'wr_lo', 'b_router')]
    return pl.pallas_call(
        functools.partial(_merge_kernel, precise), grid=(n // tm,),
        in_specs=[pl.BlockSpec((tm, D_MODEL), row), pl.BlockSpec((tm, 512), row), pl.BlockSpec((tm, 512), row),
                  pl.BlockSpec((tm, 2 * D_MODEL), row)] + [pl.BlockSpec(w.shape, const) for w in ws],
        out_specs=[pl.BlockSpec((tm, D_MODEL), row), pl.BlockSpec((tm, D_MODEL), row), pl.BlockSpec((tm, LANES), row)],
        out_shape=[jax.ShapeDtypeStruct((n, D_MODEL), F32), jax.ShapeDtypeStruct((n, D_MODEL), F32),
                   jax.ShapeDtypeStruct((n, LANES), F32)],
        compiler_params=_cparams("parallel"), name="merge_router_sample" if precise else "merge_router",
    )(x2d, o_a, o_b, g_mrg, *ws)


def _moe_kernel(be_ref, bv_ref, x_ref, wgu_ref, bgu_ref, wd_ref, bd_ref, y_ref, wgu16_ref, wd16_ref):
    i = pl.program_id(0)

    @pl.when(jnp.logical_or(i == 0, be_ref[i] != be_ref[jnp.maximum(i - 1, 0)]))
    def _():
        wgu16_ref[...] = wgu_ref[0].astype(BF16)
        wd16_ref[...] = wd_ref[0].astype(BF16)

    @pl.when(bv_ref[i] > 0)
    def _():
        h = _dot(x_ref[...].astype(BF16), wgu16_ref[...]) + bgu_ref[0]
        gate = jnp.minimum(h[:, :D_FF], SWIGLU_LIMIT)
        up = jnp.clip(h[:, D_FF:], -SWIGLU_LIMIT, SWIGLU_LIMIT)
        act = (up + 1.0) * gate * jax.nn.sigmoid(SWIGLU_ALPHA * gate)
        y_ref[...] = _dot(act.astype(BF16), wd16_ref[...]) + bd_ref[0]

    @pl.when(bv_ref[i] == 0)
    def _():
        y_ref[...] = jnp.zeros(y_ref.shape, F32)


def _moe_blocks(block_e, block_valid, x_sorted, w_gu, b_gu, w_down, b_down):
    cap = x_sorted.shape[0]
    tm = MOE_TILE
    row = lambda i, be, bv: (i, 0)
    exp = lambda i, be, bv: (be[i], 0, 0)
    grid_spec = pltpu.PrefetchScalarGridSpec(
        num_scalar_prefetch=2, grid=(cap // tm,),
        in_specs=[pl.BlockSpec((tm, D_MODEL), row),
                  pl.BlockSpec((1, D_MODEL, 2 * D_FF), exp), pl.BlockSpec((1, 1, 2 * D_FF), exp),
                  pl.BlockSpec((1, D_FF, D_MODEL), exp), pl.BlockSpec((1, 1, D_MODEL), exp)],
        out_specs=pl.BlockSpec((tm, D_MODEL), row),
        scratch_shapes=[pltpu.VMEM((D_MODEL, 2 * D_FF), BF16), pltpu.VMEM((D_FF, D_MODEL), BF16)])
    return pl.pallas_call(
        _moe_kernel, grid_spec=grid_spec, out_shape=jax.ShapeDtypeStruct((cap, D_MODEL), F32),
        compiler_params=_cparams("arbitrary"), name="moe_experts",
    )(block_e, block_valid, x_sorted, w_gu, b_gu, w_down, b_down)


def _route(logits):
    n = logits.shape[0]
    top_v, top_e = lax.top_k(logits, TOP_K)
    gates = jax.nn.softmax(top_v, axis=-1)
    onehot = jnp.sum((top_e[:, :, None] == jnp.arange(N_EXPERTS)[None, None, :]).astype(jnp.int32), axis=1)
    csum = jnp.cumsum(onehot, axis=0)
    counts = csum[-1]
    padded = (counts + MOE_TILE - 1) // MOE_TILE * MOE_TILE
    pend = jnp.cumsum(padded)
    rank = jnp.take_along_axis(csum - onehot, top_e, axis=1)
    slot = (pend - padded)[top_e] + rank
    n_blocks = -(-(n * TOP_K) // MOE_TILE) + N_EXPERTS
    cap = n_blocks * MOE_TILE
    starts = jnp.arange(n_blocks, dtype=jnp.int32) * MOE_TILE
    block_e = jnp.minimum(jnp.sum((pend[None, :] <= starts[:, None]).astype(jnp.int32), axis=1), N_EXPERTS - 1)
    block_valid = (starts < pend[-1]).astype(jnp.int32)
    assert cap * n < 2 ** 31
    tok = jnp.broadcast_to(jnp.arange(n, dtype=jnp.int32)[:, None], (n, TOP_K))
    s_tok = lax.sort((slot * n + tok).reshape(-1)) % n
    per_slot = lambda v: jnp.broadcast_to(v[block_e][:, None], (n_blocks, MOE_TILE)).reshape(cap)
    rank_in_e = jnp.arange(cap, dtype=jnp.int32) - per_slot(pend - padded)
    used = rank_in_e < per_slot(counts)
    src = jnp.clip(per_slot(jnp.cumsum(counts) - counts) + rank_in_e, 0, n * TOP_K - 1)
    slot_tok = jnp.where(used, s_tok[src], 0)
    return slot, gates, slot_tok, block_e, block_valid


def _column(ref, b):
    x = ref[...]
    return jnp.sum(jnp.where(_lane_iota(x.shape) == b, x, 0.0), axis=-1, keepdims=True)


def _fold_lanes(x):
    return sum(x[..., i * LANES:(i + 1) * LANES] for i in range(x.shape[-1] // LANES))


def _gqa_scores(q3, kt):
    g = NSA_GROUP
    return jnp.concatenate([jnp.sum(q3[i * g:(i + 1) * g] * kt[i * HEAD_DIM:(i + 1) * HEAD_DIM][None], axis=1)
                            for i in range(NSA_KV_HEADS)], axis=0)


def _gqa_weighted(p, vt):
    g = NSA_GROUP
    return jnp.concatenate([p[i * g:(i + 1) * g][:, None, :] * vt[i * HEAD_DIM:(i + 1) * HEAD_DIM][None]
                            for i in range(NSA_KV_HEADS)], axis=0)


_CWS_NAMES = ('pka', 'pkb', 'k1a_hi', 'k1a_lo', 'k1b_hi', 'k1b_lo', 'k2t',
              'pva', 'pvb', 'v1a_hi', 'v1a_lo', 'v1b_hi', 'v1b_lo', 'v2t', 'gkc_col')


def _nsa_sample_kernel(pt_ref, qt_ref, g_ref, ksn_ref, vsn_ref, kwn_ref, vwn_ref, wk_ref, wv_ref, *rest):
    n_pg = (len(rest) - len(_CWS_NAMES) - 3) // 4
    ck_pages = rest[0:n_pg]
    cv_pages = rest[n_pg:2 * n_pg]
    sk_pages = rest[2 * n_pg:3 * n_pg]
    sv_pages = rest[3 * n_pg:4 * n_pg]
    cw = dict(zip(_CWS_NAMES, rest[4 * n_pg:4 * n_pg + len(_CWS_NAMES)]))
    ov_ref, e_ref, o_ref, rows_ref = rest[4 * n_pg + len(_CWS_NAMES):]
    b = pl.program_id(0)
    past = n_pg * PAGE_SIZE

    def compress_t(pages, p):
        for i, pg in enumerate(pages):
            rows_ref[i * PAGE_SIZE:(i + 1) * PAGE_SIZE, :] = pg[0].T
        xs = [rows_ref[pl.ds(l, past // CMP_STRIDE, stride=CMP_STRIDE), :] for l in range(CMP_STRIDE)]
        h = _compress_hidden(xs, cw['p%sa' % p][...], cw['p%sb' % p][...],
                             (cw['%s1a_hi' % p][...], cw['%s1a_lo' % p][...]),
                             (cw['%s1b_hi' % p][...], cw['%s1b_lo' % p][...]), True)
        w2t = cw['%s2t' % p][...]
        return jnp.concatenate([_dot3(w2t, h[:LANES], nt=True), _dot3(w2t, h[LANES:], nt=True)], axis=0)

    kct = compress_t(ck_pages, 'k').reshape(NSA_KV_HEADS, HEAD_DIM, LANES)
    ss = jnp.sum(kct * kct, axis=1, keepdims=True)
    kct = (kct * lax.rsqrt(ss * (1.0 / HEAD_DIM) + EPS) * cw['gkc_col'][...][None]).reshape(LANES, LANES)
    vct = compress_t(cv_pages, 'v')

    q3 = (_column(qt_ref, b) * ATT_SCALE).reshape(NSA_HEADS, HEAD_DIM, 1)

    n_c = (past + 1 - CMP_BLOCK) // CMP_STRIDE + 1
    s_c = _gqa_scores(q3, kct)
    p_c = _masked_softmax(s_c, _lane_iota(s_c.shape) < n_c)
    o_c = jnp.sum(_gqa_weighted(p_c, vct), axis=-1, keepdims=True)
    row = _row_iota((NSA_HEADS, LANES))
    i0 = jnp.sum(p_c[:NSA_GROUP], axis=0, keepdims=True)
    i1 = jnp.sum(p_c[NSA_GROUP:], axis=0, keepdims=True)
    imp = _split_dot(jnp.where(row < NSA_GROUP, i0, i1), ov_ref[...])
    sel = _nsa_select(imp, _lane_iota(imp.shape), past // SEL_BLOCK, -1)
    picked = _dot(jnp.where(sel, 1.0, 0.0).astype(BF16), e_ref[...]) > 0.5

    def attend_with_new(s, valid, kn_ref, vn_ref, weighted):
        kn = _column(kn_ref, b)
        vn = _column(vn_ref, b)
        s = jnp.where(valid, s, NEG)
        s_n = _gqa_scores(q3, kn)
        m = jnp.maximum(jnp.max(s, axis=-1, keepdims=True), s_n)
        p = jnp.where(valid, jnp.exp(s - m), 0.0)
        p_n = jnp.exp(s_n - m)
        num = jnp.sum(weighted(p), axis=-1, keepdims=True) + _gqa_weighted(p_n, vn)
        den = jnp.maximum(jnp.sum(p, axis=-1, keepdims=True) + p_n, 1e-30)
        return num / den[:, :, None]

    s_s = jnp.concatenate([_gqa_scores(q3, pg[0]) for pg in sk_pages], axis=1)
    o_s = attend_with_new(
        s_s, picked, ksn_ref, vsn_ref,
        lambda p: sum(_gqa_weighted(p[:, i * PAGE_SIZE:(i + 1) * PAGE_SIZE], pg[0]) for i, pg in enumerate(sv_pages)))

    wb = wk_ref.shape[2]
    s_w = _gqa_scores(q3, wk_ref[0])
    inside = (past - wb + _lane_iota(s_w.shape)) > (past - WINDOW)
    o_w = attend_with_new(s_w, inside, kwn_ref, vwn_ref, lambda p: _fold_lanes(_gqa_weighted(p, wv_ref[0])))

    g = jnp.broadcast_to(g_ref[0], (NSA_HEADS, LANES))
    lane = _lane_iota(g.shape)
    gate = lambda j: jnp.sum(jnp.where(lane == j * NSA_HEADS + row, g, 0.0), axis=-1, keepdims=True)[:, :, None]
    o = (gate(0) * o_c + gate(1) * o_s + gate(2) * o_w).reshape(NSA_HEADS * HEAD_DIM, 1)

    @pl.when(b == 0)
    def _():
        o_ref[...] = jnp.zeros(o_ref.shape, F32)

    o_ref[...] = jnp.where(_lane_iota(o_ref.shape) == b, o, o_ref[...])


def _nsa_sample(page_table, qt, g_nsa, ksn, vsn, kwn, vwn, win_kt, win_vt, cache_ck, cache_cv, cache_skt, cache_svt,
                prm):
    db, n_pg = page_table.shape
    wb = win_kt.shape[2]
    per_b = lambda i, pt: (i, 0, 0)
    const2 = lambda i, pt: (0, 0)

    def page_spec(p):
        return pl.BlockSpec((1, PAGE_SIZE, LANES), lambda i, pt, p=p: (pt[i, p], 0, 0))

    cws = [prm['s_' + k] for k in _CWS_NAMES]
    full = lambda a: pl.BlockSpec(a.shape, const2)
    in_specs = ([full(qt), pl.BlockSpec((1, 1, LANES), per_b), full(ksn), full(vsn), full(kwn), full(vwn),
                 pl.BlockSpec((1, LANES, wb), per_b), pl.BlockSpec((1, LANES, wb), per_b)]
                + [page_spec(p) for p in range(n_pg)] * 4
                + [full(w) for w in cws] + [full(prm['ov']), full(prm['e_key'])])
    grid_spec = pltpu.PrefetchScalarGridSpec(num_scalar_prefetch=1, grid=(db,), in_specs=in_specs,
                                             out_specs=pl.BlockSpec((NSA_HEADS * HEAD_DIM, db), const2),
                                             scratch_shapes=[pltpu.VMEM((n_pg * PAGE_SIZE, LANES), F32)])
    pages = [cache_ck] * n_pg + [cache_cv] * n_pg + [cache_skt] * n_pg + [cache_svt] * n_pg
    return pl.pallas_call(
        _nsa_sample_kernel, grid_spec=grid_spec, out_shape=jax.ShapeDtypeStruct((NSA_HEADS * HEAD_DIM, db), F32),
        compiler_params=_cparams("arbitrary"), name="nsa_sample",
    )(page_table, qt, g_nsa, ksn, vsn, kwn, vwn, win_kt, win_vt, *pages, *cws, prm['ov'], prm['e_key'])


def _moba_sample_kernel(pt_ref, qt_ref, kn_ref, vn_ref, *rest):
    n_pg = (len(rest) - 1) // 2
    k_pages = rest[:n_pg]
    v_pages = rest[n_pg:2 * n_pg]
    o_ref = rest[2 * n_pg]
    b = pl.program_id(0)
    pages_per_blk = MOBA_BLOCK // PAGE_SIZE
    n_b = n_pg // pages_per_blk
    heads3 = lambda x: x.reshape(MOBA_HEADS, HEAD_DIM, x.shape[-1])

    q3 = heads3(_column(qt_ref, b) * ATT_SCALE)
    kn3 = heads3(_column(kn_ref, b))
    vn3 = heads3(_column(vn_ref, b))
    s_pages = [jnp.sum(heads3(pg[0]) * q3, axis=1) for pg in k_pages]
    s_new = jnp.sum(q3 * kn3, axis=1)

    lane = _lane_iota((MOBA_HEADS, LANES))
    s_g = jnp.full((MOBA_HEADS, LANES), -jnp.inf, F32)
    for n in range(n_b):
        tot = sum(s_pages[n * pages_per_blk + i] for i in range(pages_per_blk))
        s_g = jnp.where(lane == n, jnp.sum(tot, axis=-1, keepdims=True), s_g)
    sel = _select_top(s_g, min(MOBA_TOPK, n_b), lane.astype(F32), -1)
    sel_blk = [jnp.sum(jnp.where(jnp.logical_and(sel, lane == n), 1.0, 0.0), axis=-1, keepdims=True) > 0.5
               for n in range(n_b)]

    m = s_new
    for i in range(n_pg):
        m = jnp.maximum(m, jnp.max(jnp.where(sel_blk[i // pages_per_blk], s_pages[i], NEG), axis=-1, keepdims=True))
    p_new = jnp.exp(s_new - m)
    l = p_new
    acc = jnp.zeros((MOBA_HEADS, HEAD_DIM, PAGE_SIZE), F32)
    for i in range(n_pg):
        p = jnp.where(sel_blk[i // pages_per_blk], jnp.exp(s_pages[i] - m), 0.0)
        l = l + jnp.sum(p, axis=-1, keepdims=True)
        acc = acc + heads3(v_pages[i][0]) * p[:, None, :]
    o = (jnp.sum(acc, axis=-1, keepdims=True) + p_new[:, :, None] * vn3) / jnp.maximum(l, 1e-30)[:, :, None]

    @pl.when(b == 0)
    def _():
        o_ref[...] = jnp.zeros(o_ref.shape, F32)

    o_ref[...] = jnp.where(_lane_iota(o_ref.shape) == b, o.reshape(MOBA_HEADS * HEAD_DIM, 1), o_ref[...])


def _moba_sample(page_table, qt, knt, vnt, cache_kt, cache_vt):
    db, n_pg = page_table.shape
    w = qt.shape[0]
    assert (n_pg * PAGE_SIZE) % MOBA_BLOCK == 0
    const2 = lambda i, pt: (0, 0)

    def page_spec(p):
        return pl.BlockSpec((1, w, PAGE_SIZE), lambda i, pt, p=p: (pt[i, p], 0, 0))

    in_specs = [pl.BlockSpec((w, db), const2)] * 3 + [page_spec(p) for p in range(n_pg)] * 2
    grid_spec = pltpu.PrefetchScalarGridSpec(num_scalar_prefetch=1, grid=(db,), in_specs=in_specs,
                                             out_specs=pl.BlockSpec((w, db), const2))
    return pl.pallas_call(
        _moba_sample_kernel, grid_spec=grid_spec, out_shape=jax.ShapeDtypeStruct((w, db), F32),
        compiler_params=_cparams("arbitrary"), name="moba_sample",
    )(page_table, qt, knt, vnt, *([cache_kt] * n_pg), *([cache_vt] * n_pg))


def _split_weight(w):
    c = w * (2.0 ** 16 + 1.0)
    hi = c - (c - w)
    return hi.astype(BF16), (w - hi).astype(BF16)


def _prepare(norm_attn, w_in, g_qa, g_ka, g_kc, cmp_pos_k, cmp_w1_k, cmp_w2_k, cmp_pos_v, cmp_w1_v, cmp_w2_v,
             g_qb, g_kb, w_up_a, w_up_b, w_out, norm_ffn, w_router, b_router):
    qa = NSA_HEADS * HEAD_DIM
    kva = NSA_KV_HEADS * HEAD_DIM
    qb = MOBA_HEADS * HEAD_DIM
    widths = [qa, kva, kva, kva, kva, kva, kva, 3 * NSA_HEADS, qb, qb, qb, 2 * D_MODEL]
    cuts = np.cumsum([0] + widths)
    part = lambda i: w_in[:, cuts[i]:cuts[i + 1]]
    gcols = np.array([h * 3 + j for j in range(3) for h in range(NSA_HEADS)])
    wgnt = jnp.zeros((LANES, D_MODEL), F32).at[:3 * NSA_HEADS].set(part(7)[:, gcols].T)
    two = lambda g: jnp.concatenate([g, g]).reshape(1, LANES)
    lane = np.arange(LANES)
    c_idx = np.arange(LANES)[:, None] * CMP_STRIDE
    s_idx = np.arange(LANES)[None, :] * SEL_BLOCK
    ov = ((c_idx < s_idx + SEL_BLOCK) & (c_idx + CMP_BLOCK > s_idx) & (np.arange(LANES)[:, None] < LANES - 1))
    half_w = CMP_STRIDE * HEAD_DIM

    def w2wide(w2):
        z = jnp.zeros_like(w2)
        return jnp.concatenate([w2, z], axis=1).astype(BF16), jnp.concatenate([z, w2], axis=1).astype(BF16)

    k2g0, k2g1 = w2wide(cmp_w2_k)
    wr_hi, wr_lo = _split_weight(jnp.zeros((D_MODEL, LANES), F32).at[:, :N_EXPERTS].set(w_router))
    prm = {
        'norm_attn': norm_attn.reshape(1, D_MODEL),
        'wqat_f32': part(0).T, 'wkvat_f32': w_in[:, cuts[1]:cuts[7]].T, 'wgnt_f32': wgnt,
        'wqbt_f32': part(8).T, 'wkbt_f32': part(9).T, 'wvbt_f32': part(10).T, 'wgm_f32': part(11),
        'g_col': jnp.stack([g_qa, g_qb, g_ka[0], g_ka[1], g_ka[2], g_kb], axis=1),
        'gkc': two(g_kc),
        'bd': jnp.asarray((lane[:, None] // HEAD_DIM) == (lane[None, :] // HEAD_DIM), BF16),
        'pka': cmp_pos_k[:CMP_STRIDE].reshape(1, half_w), 'pkb': cmp_pos_k[CMP_STRIDE:].reshape(1, half_w),
        'pva': cmp_pos_v[:CMP_STRIDE].reshape(1, half_w), 'pvb': cmp_pos_v[CMP_STRIDE:].reshape(1, half_w),
        'k1a': cmp_w1_k[:half_w].astype(BF16), 'k1b': cmp_w1_k[half_w:].astype(BF16),
        'v1a': cmp_w1_v[:half_w].astype(BF16), 'v1b': cmp_w1_v[half_w:].astype(BF16),
        'k2g0': k2g0, 'k2g1': k2g1, 'v2t': cmp_w2_v.T.astype(BF16),
        'ov': jnp.asarray(ov, BF16), 'ovt': jnp.asarray(ov.T, BF16),
        'w_up_a_f32': w_up_a, 'w_up_b_f32': w_up_b, 'w_out_f32': w_out,
        'norm_ffn': norm_ffn.reshape(1, D_MODEL),
        'wr_hi': wr_hi, 'wr_lo': wr_lo,
        'b_router': jnp.zeros((1, LANES), F32).at[0, :N_EXPERTS].set(b_router),
    }
    for k in _INPROJ_W + ('w_up_a', 'w_up_b', 'w_out'):
        prm[k] = prm[k + '_f32'].astype(BF16)
    for p, w1, w2 in (('k', cmp_w1_k, cmp_w2_k), ('v', cmp_w1_v, cmp_w2_v)):
        for half, w in (('a', w1[:half_w]), ('b', w1[half_w:])):
            prm['s_%s1%s_hi' % (p, half)], prm['s_%s1%s_lo' % (p, half)] = _split_weight(w)
        prm['s_%s2t' % p] = w2.T
        prm['s_p%sa' % p] = prm['p%sa' % p]
        prm['s_p%sb' % p] = prm['p%sb' % p]
    prm['s_gkc_col'] = g_kc.reshape(HEAD_DIM, 1)
    return prm


def _rope_tables(pos):
    half = HEAD_DIM // 2
    inv = ROPE_THETA ** (-np.arange(half, dtype=np.float64) / half)
    ang = inv[:, None] * np.asarray(pos, np.float64)[None, :]
    return jnp.asarray(np.cos(ang), F32), jnp.asarray(np.sin(ang), F32)


def _expand_matrix(n_keys, block):
    e = (np.arange(n_keys)[None, :] // block) == np.arange(LANES)[:, None]
    return jnp.asarray(e, BF16)


def kernel(x_prompt, x_sample, cache_nsa_cmp_k, cache_nsa_cmp_v, cache_nsa_sel_k, cache_nsa_sel_v, cache_nsa_win_k, cache_nsa_win_v, cache_moba_k, cache_moba_v, page_table, norm_attn, w_in, g_qa, g_ka, g_kc, cmp_pos_k, cmp_w1_k, cmp_w2_k, cmp_pos_v, cmp_w1_v, cmp_w2_v, g_qb, g_kb, w_up_a, w_up_b, w_out, norm_ffn, w_router, b_router, w_gu, b_gu, w_down, b_down):
    B, T, _ = x_prompt.shape
    DB, DS, _ = x_sample.shape
    assert DS == 1
    n_pg = page_table.shape[1]
    past = n_pg * PAGE_SIZE
    wb = cache_nsa_win_k.shape[1]

    prm = _prepare(norm_attn, w_in, g_qa, g_ka, g_kc, cmp_pos_k, cmp_w1_k, cmp_w2_k, cmp_pos_v, cmp_w1_v, cmp_w2_v,
                   g_qb, g_kb, w_up_a, w_up_b, w_out, norm_ffn, w_router, b_router)
    prm['e_key'] = _expand_matrix(past, SEL_BLOCK)

    P = _in_proj(x_prompt.reshape(B * T, D_MODEL), B, _rope_tables(np.arange(T)), prm, KEY_BLOCK, False)
    r3 = lambda a: a.reshape(B, T, a.shape[-1])
    kc_cmp, vct_cmp = _compress_prompt(r3(P['kc_row']), r3(P['vc_row']), prm)
    oa_p = _nsa_prompt(P['qa'], P['gn'], kc_cmp, vct_cmp, P['ks_rows'], P['vs_blk'], P['kw_rows'], P['vw_blk'], prm)
    ob_p = _moba_prompt(P['qb'], P['kb'], P['kb_rows'], P['vb_blk'])
    h_p, xn_p, lg_p = _merge(x_prompt.reshape(B * T, D_MODEL), oa_p.reshape(B * T, 512), ob_p.reshape(B * T, 512),
                             P['gm'], prm, 256, False)

    S = _in_proj(x_sample.reshape(DB, D_MODEL), 1, _rope_tables(np.full((DB,), past)), prm, DB, True)
    cols2 = lambda c: jnp.transpose(c, (0, 2, 3, 1)).reshape(c.shape[0], c.shape[2] * c.shape[3], c.shape[1])
    oa_t = _nsa_sample(page_table, S['qa'][0], S['gn'][0].T.reshape(DB, 1, LANES),
                       S['ks'][0], S['vs'][0], S['kw'][0], S['vw'][0],
                       cols2(cache_nsa_win_k), cols2(cache_nsa_win_v),
                       cols2(cache_nsa_cmp_k), cols2(cache_nsa_cmp_v), cols2(cache_nsa_sel_k), cols2(cache_nsa_sel_v),
                       prm)
    ob_t = _moba_sample(page_table, S['qb'][0], S['kb'][0], S['vb'][0], cols2(cache_moba_k), cols2(cache_moba_v))
    h_s, xn_s, lg_s = _merge(x_sample.reshape(DB, D_MODEL), oa_t.T, ob_t.T, S['gm'], prm, DB, True)

    xn_all = jnp.concatenate([xn_p, xn_s], axis=0)
    logits = jnp.concatenate([lg_p, lg_s], axis=0)[:, :N_EXPERTS]
    slot, gates, slot_tok, block_e, block_valid = _route(logits)
    y_slot = _moe_blocks(block_e, block_valid, xn_all[slot_tok],
                         w_gu, b_gu.reshape(N_EXPERTS, 1, 2 * D_FF), w_down, b_down.reshape(N_EXPERTS, 1, D_MODEL))
    n_tok = B * T + DB
    picked = y_slot[slot.T.reshape(-1)].reshape(TOP_K, n_tok, D_MODEL)
    mix = lambda lo, hi: sum(gates[lo:hi, k:k + 1] * picked[k, lo:hi] for k in range(TOP_K))
    y_prompt = (h_p + mix(0, B * T)).reshape(B, T, D_MODEL)
    y_sample = (h_s + mix(B * T, n_tok)).reshape(DB, DS, D_MODEL)

    def kv4(a):
        n, w, t = a.shape
        return jnp.transpose(a.reshape(n, w // HEAD_DIM, HEAD_DIM, t), (0, 3, 1, 2))

    new4 = lambda a: kv4(a).reshape(DB, 1, a.shape[1] // HEAD_DIM, HEAD_DIM)
    wp = min(WINDOW, T)
    ws = min(WINDOW, wb + DS)
    kw_all = jnp.concatenate([cache_nsa_win_k, new4(S['kw'])], axis=1)[:, -ws:]
    vw_all = jnp.concatenate([cache_nsa_win_v, new4(S['vw'])], axis=1)[:, -ws:]
    return (y_prompt, y_sample,
            kv4(P['kc']), kv4(P['vc']), kv4(P['ks']), kv4(P['vs']),
            kv4(P['kw'][:, :, T - wp:]), kv4(P['vw'][:, :, T - wp:]), kv4(P['kb']), kv4(P['vb']),
            new4(S['kc']), new4(S['vc']), new4(S['ks']), new4(S['vs']), kw_all, vw_all,
            new4(S['kb']), new4(S['vb']))
```

```python
import functools

import numpy as np
import jax
import jax.numpy as jnp
from jax import lax
from jax.experimental import pallas as pl
from jax.experimental.pallas import tpu as pltpu

F32 = jnp.float32
BF16 = jnp.bfloat16

D_MODEL = 1024
PAGE_SIZE = 128
HEAD_DIM = 64
NSA_HEADS = 8
NSA_KV_HEADS = 2
NSA_GROUP = NSA_HEADS // NSA_KV_HEADS
CMP_BLOCK = 32
CMP_STRIDE = 16
CMP_HIDDEN = 256
SEL_BLOCK = 64
N_SEL = 8
WINDOW = 512
MOBA_HEADS = 8
MOBA_BLOCK = 256
MOBA_TOPK = 3
N_EXPERTS = 32
TOP_K = 4
D_FF = D_MODEL
SWIGLU_LIMIT = 7.0
SWIGLU_ALPHA = 1.702
ROPE_THETA = 10000.0
EPS = 1e-6
NEG = -1e30
POS_BIG = 1e30

LANES = 128
HALF = HEAD_DIM
ATT_SCALE = HEAD_DIM ** -0.5
Q_TILE = 128
KEY_BLOCK = MOBA_BLOCK
MOE_TILE = 512
MERGE_TILE = 512
NSA_SAMPLE_SEQS = 1
VMEM_LIMIT = 56 * 1024 * 1024
FAR = -(1 << 20)


def _cparams(*sem):
    return pltpu.CompilerParams(dimension_semantics=sem, vmem_limit_bytes=VMEM_LIMIT)


def _lane_iota(shape):
    return lax.broadcasted_iota(jnp.int32, shape, len(shape) - 1)


def _row_iota(shape):
    return lax.broadcasted_iota(jnp.int32, shape, 0)


def _dot(a, b):
    return jnp.dot(a, b, preferred_element_type=F32)


def _dot_nt(a, b):
    return lax.dot_general(a, b, (((1,), (1,)), ((), ())), preferred_element_type=F32)


def _split(a):
    hi = a.astype(BF16)
    lo = (a - hi.astype(F32)).astype(BF16)
    return hi, lo


def _split_dot(a, b):
    hi, lo = _split(a)
    return _dot(hi, b) + _dot(lo, b)


def _dot_split(a, b):
    hi, lo = _split(b)
    return _dot(a, hi) + _dot(a, lo)


def _dot3(a, b, nt=False):
    mm = _dot_nt if nt else _dot
    a_hi, a_lo = _split(a)
    b_hi, b_lo = b if isinstance(b, tuple) else _split(b)
    return mm(a_hi, b_hi) + (mm(a_hi, b_lo) + mm(a_lo, b_hi))


def _mm(a, w, precise):
    return _dot3(a, w) if precise else _dot(a.astype(BF16), w)


def _masked_softmax(s, mask):
    s = jnp.where(mask, s, NEG)
    m = jnp.max(s, axis=-1, keepdims=True)
    e = jnp.where(mask, jnp.exp(s - m), 0.0)
    return e / jnp.maximum(jnp.sum(e, axis=-1, keepdims=True), 1e-30)


def _head_norm(y, gain, bd):
    ss = _split_dot(y * y, bd)
    return y * lax.rsqrt(ss * (1.0 / HEAD_DIM) + EPS) * gain


def _select_top(score, n_pick, idx_f, axis):
    sel = jnp.zeros(score.shape, jnp.bool_)
    for _ in range(n_pick):
        m = jnp.max(score, axis=axis, keepdims=True)
        first = jnp.min(jnp.where(score == m, idx_f, 1e9), axis=axis, keepdims=True)
        hit = idx_f == first
        sel = jnp.logical_or(sel, hit)
        score = jnp.where(hit, -jnp.inf, score)
    return sel


def _nsa_select(imp, blk, cur, axis):
    forced = (blk == 0) | (blk == cur) | (blk == cur - 1)
    valid = blk <= cur
    score = jnp.where(forced, POS_BIG, jnp.where(valid, imp, -jnp.inf))
    sel = _select_top(score, N_SEL, blk.astype(F32), axis)
    return jnp.logical_and(sel, valid)


def _attend_block(m_ref, l_ref, acc_ref, i, kmat, qmat, vt, first, bias=None):
    s = _dot(kmat, qmat)
    if bias is not None:
        s = s + bias
    m_cur = jnp.max(s, axis=0, keepdims=True)
    if first:
        p = jnp.exp(s - m_cur)
        m_ref[i] = m_cur
        l_ref[i] = jnp.sum(p, axis=0, keepdims=True)
        acc_ref[i] = _dot(vt, p.astype(BF16))
    else:
        m_prev = m_ref[i]
        m_new = jnp.maximum(m_prev, m_cur)
        alpha = jnp.exp(m_prev - m_new)
        p = jnp.exp(s - m_new)
        m_ref[i] = m_new
        l_ref[i] = alpha * l_ref[i] + jnp.sum(p, axis=0, keepdims=True)
        acc_ref[i] = alpha * acc_ref[i] + _dot(vt, p.astype(BF16))


def _inproj_kernel(precise, x_ref, cos_ref, sin_ref, gn_ref, wqat, wkvat, wgnt, wqbt, wkbt, wvbt, wgm, g_col, *outs):
    (qa_o, kcr_o, vcr_o, kc_o, vc_o, ks_o, vs_o, kw_o, vw_o, ksr_o, vsb_o, kwr_o, vwb_o,
     gn_o, qb_o, kb_o, vb_o, kbr_o, vbb_o, gm_o) = outs
    x = x_ref[...]
    xn = x * lax.rsqrt(jnp.mean(x * x, axis=-1, keepdims=True) + EPS) * gn_ref[...]
    xa = xn if precise else xn.astype(BF16)
    mm_t = (lambda wt: _dot3(wt, xn, nt=True)) if precise else (lambda wt: _dot_nt(wt, xa))
    half = HEAD_DIM // 2
    cos = cos_ref[...][None]
    sin = sin_ref[...][None]

    def tile(j):
        return slice(j * LANES, (j + 1) * LANES)

    def qk_t(y, gain_col):
        y = y.reshape(2, HEAD_DIM, y.shape[-1])
        y = y * lax.rsqrt(jnp.mean(y * y, axis=1, keepdims=True) + EPS) * gain_col[None]
        x1, x2 = y[:, :half], y[:, half:]
        return jnp.concatenate([x1 * cos - x2 * sin, x2 * cos + x1 * sin], axis=1).reshape(LANES, -1)

    y_qa = mm_t(wqat[...])
    y_qb = mm_t(wqbt[...])
    for j in range(4):
        qa_o[0, tile(j), :] = qk_t(y_qa[tile(j)], g_col[:, 0:1]).astype(qa_o.dtype)
        qb_o[0, tile(j), :] = qk_t(y_qb[tile(j)], g_col[:, 1:2]).astype(qb_o.dtype)
    y_kva = mm_t(wkvat[...])
    nsa_kv = ((kc_o, vc_o, kcr_o, vcr_o), (ks_o, vs_o, ksr_o, vsb_o), (kw_o, vw_o, kwr_o, vwb_o))
    for i, (k_o, v_o, k2_o, v2_o) in enumerate(nsa_kv):
        k = qk_t(y_kva[tile(2 * i)], g_col[:, 2 + i:3 + i])
        v = y_kva[tile(2 * i + 1)]
        k_o[0] = k
        v_o[0] = v
        if i == 0:
            k2_o[...] = k.T
            v2_o[...] = v.T
        else:
            k2_o[0, 0] = k.T.astype(BF16)
            v2_o[0, 0] = v.astype(BF16)
    gn_o[0] = jax.nn.sigmoid(mm_t(wgnt[...]))
    y_kb = mm_t(wkbt[...])
    y_vb = mm_t(wvbt[...])
    vb_o[0] = y_vb
    vbb_o[0, 0] = y_vb.astype(BF16)
    for j in range(4):
        k = qk_t(y_kb[tile(j)], g_col[:, 5:6])
        kb_o[0, tile(j), :] = k
        kbr_o[0, 0, :, tile(j)] = k.T.astype(BF16)
    gm_o[...] = jax.nn.sigmoid(_mm(xa, wgm[...], precise))


_INPROJ_W = ('wqat', 'wkvat', 'wgnt', 'wqbt', 'wkbt', 'wvbt', 'wgm')
_INPROJ_OUT = ('qa', 'kc_row', 'vc_row', 'kc', 'vc', 'ks', 'vs', 'kw', 'vw', 'ks_rows', 'vs_blk', 'kw_rows', 'vw_blk',
               'gn', 'qb', 'kb', 'vb', 'kb_rows', 'vb_blk', 'gm')


def _in_proj(x2d, n_batch, rope, prm, tm, precise):
    n = x2d.shape[0]
    t = n // n_batch
    nb = t // tm
    cos_t, sin_t = rope
    row = lambda i: (i, 0)
    const = lambda i: (0, 0)
    pos_t = lambda i: (0, i % nb)
    tr = lambda i: (i // nb, 0, i % nb)
    blk = lambda i: (i // nb, i % nb, 0, 0)
    ws = [prm[k + ('_f32' if precise else '')] for k in _INPROJ_W] + [prm['g_col']]
    in_specs = ([pl.BlockSpec((tm, D_MODEL), row), pl.BlockSpec((HEAD_DIM // 2, tm), pos_t),
                 pl.BlockSpec((HEAD_DIM // 2, tm), pos_t), pl.BlockSpec((1, D_MODEL), const)]
                + [pl.BlockSpec(w.shape, const) for w in ws])
    qdt = F32 if precise else BF16
    rows_out = lambda w, dt: (jax.ShapeDtypeStruct((n, w), dt), pl.BlockSpec((tm, w), row))
    t_out = lambda w, dt=F32: (jax.ShapeDtypeStruct((n_batch, w, t), dt), pl.BlockSpec((1, w, tm), tr))
    r_out = lambda w: (jax.ShapeDtypeStruct((n_batch, nb, tm, w), BF16), pl.BlockSpec((1, 1, tm, w), blk))
    b_out = lambda w: (jax.ShapeDtypeStruct((n_batch, nb, w, tm), BF16), pl.BlockSpec((1, 1, w, tm), blk))
    outs = ([t_out(512, qdt), rows_out(LANES, F32), rows_out(LANES, F32)] + [t_out(LANES)] * 6
            + [r_out(LANES), b_out(LANES), r_out(LANES), b_out(LANES)]
            + [t_out(LANES), t_out(512, qdt), t_out(512), t_out(512), r_out(512), b_out(512),
               rows_out(2 * D_MODEL, F32)])
    res = pl.pallas_call(
        functools.partial(_inproj_kernel, precise), grid=(n // tm,), in_specs=in_specs,
        out_specs=[o[1] for o in outs], out_shape=[o[0] for o in outs], compiler_params=_cparams("parallel"),
        name="in_proj_sample" if precise else "in_proj",
    )(x2d, cos_t, sin_t, prm['norm_attn'], *ws)
    return dict(zip(_INPROJ_OUT, res))


def _compress_hidden(xs, pos_a, pos_b, w1a, w1b, precise):
    lo = _lane_iota((LANES, LANES)) < HALF
    t0, t1 = [], []
    for l in range(0, CMP_STRIDE, 2):
        a, b = xs[l], xs[l + 1]
        t0.append(jnp.where(lo, a, pltpu.roll(b, HALF, 1)))
        t1.append(jnp.where(lo, pltpu.roll(a, HALF, 1), b))
    x = jnp.concatenate([jnp.concatenate(t0, axis=1), jnp.concatenate(t1, axis=1)], axis=0)
    first = _mm(x + pos_a, w1a, precise)
    second = _mm(x + pos_b, w1b, precise)
    n = LANES
    shifted = jnp.concatenate([pltpu.roll(second[:n], n - 1, 0), pltpu.roll(second[n:], n - 1, 0)], axis=0)
    return jax.nn.gelu(first + shifted)


_CW_NAMES = ('pka', 'pkb', 'k1a', 'k1b', 'k2g0', 'k2g1', 'pva', 'pvb', 'v1a', 'v1b', 'v2t', 'gkc', 'bd')


def _compress_prompt_kernel(k_ref, v_ref, *rest):
    cw = dict(zip(_CW_NAMES, rest[:len(_CW_NAMES)]))
    kc_o, vct_o = rest[len(_CW_NAMES):]
    n = k_ref.shape[1] // CMP_STRIDE

    def hidden(ref, p):
        xs = [ref[0, pl.ds(l, n, stride=CMP_STRIDE), :] for l in range(CMP_STRIDE)]
        return _compress_hidden(xs, cw['p%sa' % p][...], cw['p%sb' % p][...], cw['%s1a' % p][...],
                                cw['%s1b' % p][...], False).astype(BF16)

    hk = hidden(k_ref, 'k')
    kc = _dot(hk[:LANES], cw['k2g0'][...]) + _dot(hk[LANES:], cw['k2g1'][...])
    kc_o[0] = _head_norm(kc, cw['gkc'][...], cw['bd'][...])
    hv = hidden(v_ref, 'v')
    w2t = cw['v2t'][...]
    vct_o[0] = jnp.concatenate([_dot_nt(w2t, hv[:LANES]), _dot_nt(w2t, hv[LANES:])], axis=0)


def _compress_prompt(k_c, v_c, prm):
    b, t, _ = k_c.shape
    assert t // CMP_STRIDE == LANES
    ws = [prm[k] for k in _CW_NAMES]
    per_b = lambda i: (i, 0, 0)
    const = lambda i: (0, 0)
    return pl.pallas_call(
        _compress_prompt_kernel, grid=(b,),
        in_specs=[pl.BlockSpec((1, t, LANES), per_b)] * 2 + [pl.BlockSpec(w.shape, const) for w in ws],
        out_specs=[pl.BlockSpec((1, LANES, LANES), per_b)] * 2,
        out_shape=[jax.ShapeDtypeStruct((b, LANES, LANES), F32)] * 2,
        compiler_params=_cparams("parallel"), name="compress_prompt",
    )(k_c, v_c, *ws)


def _nsa_prompt_kernel(q_ref, g_ref, kc_ref, vct_ref, ks_ref, vs_ref, kw_ref, vw_ref, ovt_ref,
                       o_ref, qaug_ref, oc_ref, m_ref, l_ref, acc_ref):
    qt = Q_TILE
    kb = ks_ref.shape[2]
    s0 = pl.program_id(1) * qt
    t_q = s0 + _lane_iota((1, qt))
    head_cols = lambda h: slice(h * qt, (h + 1) * qt)
    zero = jnp.zeros((HEAD_DIM, qt), BF16)

    per_head = lambda x: jnp.concatenate([x] * NSA_HEADS, axis=1)
    blocks = []
    for h in range(NSA_HEADS):
        qh = q_ref[0, h * HEAD_DIM:(h + 1) * HEAD_DIM, :] * ATT_SCALE
        blocks.append(jnp.concatenate([qh, zero] if h < NSA_GROUP else [zero, qh], axis=0))
    qst = jnp.concatenate(blocks, axis=1)

    n_c = kc_ref.shape[1] - 1
    c_end = _row_iota((LANES, qt)) * CMP_STRIDE + (CMP_BLOCK - 1)
    ok = per_head(jnp.where(c_end <= jnp.minimum(t_q, (n_c - 1) * CMP_STRIDE + CMP_BLOCK - 1), 1.0, 0.0)) > 0.5
    s_c = jnp.where(ok, _dot(kc_ref[0].astype(BF16), qst), NEG)
    e = jnp.where(ok, jnp.exp(s_c - jnp.max(s_c, axis=0, keepdims=True)), 0.0)
    p_c = e / jnp.maximum(jnp.sum(e, axis=0, keepdims=True), 1e-30)
    oc_ref[...] = _dot(vct_ref[0].astype(BF16), p_c.astype(BF16))
    g4 = NSA_GROUP * qt
    imp = jnp.concatenate([sum(p_c[:, g * g4 + r * qt: g * g4 + (r + 1) * qt] for r in range(NSA_GROUP))
                           for g in range(NSA_KV_HEADS)], axis=1)
    imp = _dot_split(ovt_ref[...], imp)
    cur = (s0 + (_lane_iota((1, 2 * qt)) & (qt - 1))) // SEL_BLOCK
    sel = _nsa_select(imp, _row_iota(imp.shape), cur, 0)

    bias = jnp.where(sel, 0.0, NEG).astype(BF16)
    bias = jnp.concatenate([bias[:, :qt]] * NSA_GROUP + [bias[:, qt:]] * NSA_GROUP, axis=1)
    qaug_ref[...] = jnp.concatenate([qst, bias], axis=0)
    per_kb = kb // SEL_BLOCK

    def sel_block(c, causal, first):
        key = _row_iota((kb, LANES))
        onehot = jnp.where(_lane_iota((kb, LANES)) - c * per_kb == key // SEL_BLOCK, 1.0, 0.0).astype(BF16)
        kmat = jnp.concatenate([ks_ref[0, c], onehot], axis=1)
        cbias = per_head(jnp.where(c * kb + _row_iota((kb, qt)) <= t_q, 0.0, NEG)) if causal else None
        _attend_block(m_ref, l_ref, acc_ref, 0, kmat, qaug_ref[...], vs_ref[0, c], first, cbias)

    c_last = s0 // kb
    sel_block(0, True, True)

    def body(c, carry):
        sel_block(c, False, False)
        return carry

    lax.fori_loop(1, c_last, body, 0)

    @pl.when(c_last > 0)
    def _():
        sel_block(c_last, True, False)

    w0 = s0 // kb - (WINDOW // kb)
    n_wb = WINDOW // kb + 1
    for step, j in enumerate([n_wb - 1] + list(range(n_wb - 1))):
        blk = w0 + j
        base = jnp.where(blk < 0, FAR, blk * kb)
        dist = t_q - (base + _row_iota((kb, qt)))
        wbias = jnp.where(dist >= 0, jnp.where(dist < WINDOW, 0.0, NEG), NEG)
        idx = jnp.maximum(blk, 0)
        _attend_block(m_ref, l_ref, acc_ref, 1, kw_ref[0, idx], qaug_ref[:LANES, :], vw_ref[0, idx], step == 0,
                      per_head(wbias))

    g = g_ref[0]
    heads = []
    for h in range(NSA_HEADS):
        cs = head_cols(h)
        rows = slice((h // NSA_GROUP) * HEAD_DIM, (h // NSA_GROUP + 1) * HEAD_DIM)
        gate = lambda j: g[j * NSA_HEADS + h: j * NSA_HEADS + h + 1]
        o_s = acc_ref[0, rows, cs] / jnp.maximum(l_ref[0, :, cs], 1e-30)
        o_w = acc_ref[1, rows, cs] / jnp.maximum(l_ref[1, :, cs], 1e-30)
        heads.append(gate(0) * oc_ref[rows, cs] + gate(1) * o_s + gate(2) * o_w)
    o_ref[0] = jnp.concatenate(heads, axis=0).T.astype(BF16)


def _nsa_prompt(q_a, g_nsa, kc, vct, ks_rows, vs_blk, kw_rows, vw_blk, prm):
    b, w, t = q_a.shape
    qt = Q_TILE
    kb = ks_rows.shape[2]
    assert qt == LANES and kb % qt == 0 and WINDOW % kb == 0 and kb % SEL_BLOCK == 0 and t // SEL_BLOCK <= LANES
    cols = NSA_HEADS * qt
    per_q = lambda i, j: (i, 0, j)
    per_b = lambda i, j: (i, 0, 0)
    per_b4 = lambda i, j: (i, 0, 0, 0)
    return pl.pallas_call(
        _nsa_prompt_kernel, grid=(b, t // qt),
        in_specs=[pl.BlockSpec((1, w, qt), per_q), pl.BlockSpec((1, LANES, qt), per_q),
                  pl.BlockSpec((1, LANES, LANES), per_b), pl.BlockSpec((1, LANES, LANES), per_b),
                  pl.BlockSpec((1,) + ks_rows.shape[1:], per_b4), pl.BlockSpec((1,) + vs_blk.shape[1:], per_b4),
                  pl.BlockSpec((1,) + kw_rows.shape[1:], per_b4), pl.BlockSpec((1,) + vw_blk.shape[1:], per_b4),
                  pl.BlockSpec(prm['ovt'].shape, lambda i, j: (0, 0))],
        out_specs=pl.BlockSpec((1, qt, w), lambda i, j: (i, j, 0)),
        out_shape=jax.ShapeDtypeStruct((b, t, w), BF16),
        scratch_shapes=[pltpu.VMEM((2 * LANES, cols), BF16), pltpu.VMEM((LANES, cols), F32),
                        pltpu.VMEM((2, 1, cols), F32), pltpu.VMEM((2, 1, cols), F32),
                        pltpu.VMEM((2, LANES, cols), F32)],
        compiler_params=_cparams("arbitrary", "arbitrary"), name="nsa_prompt",
    )(q_a, g_nsa, kc, vct, ks_rows, vs_blk, kw_rows, vw_blk, prm['ovt'])


def _moba_prompt_kernel(q_ref, kt_ref, kb_ref, vb_ref, o_ref, kmean_ref, qaug_ref, m_ref, l_ref, acc_ref):
    qt = MOBA_BLOCK
    cur = pl.program_id(1)
    n_b = kb_ref.shape[1]
    n_pair = MOBA_HEADS // 2
    cols = 2 * qt

    @pl.when(cur == 0)
    def _():
        lane = _lane_iota((kt_ref.shape[1], LANES))
        km = jnp.zeros(lane.shape, F32)
        for n in range(n_b):
            km = jnp.where(lane == n, jnp.mean(kt_ref[0, :, n * qt:(n + 1) * qt], axis=-1, keepdims=True), km)
        kmean_ref[...] = km.T

    own_bias = jnp.where(_row_iota((qt, qt)) <= _lane_iota((qt, qt)), 0.0, NEG)
    own_bias = jnp.concatenate([own_bias, own_bias], axis=1)
    for j in range(n_pair):
        tl = slice(j * LANES, (j + 1) * LANES)
        qj = q_ref[0, tl, :]
        zero = jnp.zeros((HEAD_DIM, qt), qj.dtype)
        qst = jnp.concatenate([jnp.concatenate([qj[:HEAD_DIM], zero], axis=0),
                               jnp.concatenate([zero, qj[HEAD_DIM:]], axis=0)], axis=1)
        km_hi, km_lo = _split(kmean_ref[:, tl])
        s_g = _dot(km_hi, qst) + _dot(km_lo, qst)
        blk = _row_iota(s_g.shape)
        past = blk < cur
        sel = _select_top(jnp.where(past, s_g, -jnp.inf), MOBA_TOPK, blk.astype(F32), 0)
        bias = jnp.where(jnp.logical_and(sel, past), 0.0, NEG).astype(BF16)
        qs = qst * ATT_SCALE
        qaug_ref[j] = jnp.concatenate([qs, bias], axis=0)
        _attend_block(m_ref, l_ref, acc_ref, j, kb_ref[0, cur, :, tl], qs, vb_ref[0, cur, tl, :], True, own_bias)

    def body(n, carry):
        onehot = jnp.where(_lane_iota((qt, LANES)) == n, 1.0, 0.0).astype(BF16)
        for j in range(n_pair):
            tl = slice(j * LANES, (j + 1) * LANES)
            _attend_block(m_ref, l_ref, acc_ref, j, jnp.concatenate([kb_ref[0, n, :, tl], onehot], axis=1),
                          qaug_ref[j], vb_ref[0, n, tl, :], False)
        return carry

    lax.fori_loop(0, cur, body, 0)
    for j in range(n_pair):
        o = acc_ref[j] / jnp.maximum(l_ref[j], 1e-30)
        o = jnp.concatenate([o[:HEAD_DIM, :qt], o[HEAD_DIM:, qt:]], axis=0)
        o_ref[0, :, j * LANES:(j + 1) * LANES] = o.T.astype(BF16)


def _moba_prompt(q_b, kt, kb_rows, vb_blk):
    b, w, t = q_b.shape
    qt = MOBA_BLOCK
    assert kb_rows.shape[2] == qt and t // qt <= LANES
    per_b = lambda i, j: (i, 0, 0)
    per_b4 = lambda i, j: (i, 0, 0, 0)
    n_pair = MOBA_HEADS // 2
    return pl.pallas_call(
        _moba_prompt_kernel, grid=(b, t // qt),
        in_specs=[pl.BlockSpec((1, w, qt), lambda i, j: (i, 0, j)), pl.BlockSpec((1, w, t), per_b),
                  pl.BlockSpec((1,) + kb_rows.shape[1:], per_b4), pl.BlockSpec((1,) + vb_blk.shape[1:], per_b4)],
        out_specs=pl.BlockSpec((1, qt, w), lambda i, j: (i, j, 0)),
        out_shape=jax.ShapeDtypeStruct((b, t, w), BF16),
        scratch_shapes=[pltpu.VMEM((LANES, w), F32), pltpu.VMEM((n_pair, 2 * LANES, 2 * qt), BF16),
                        pltpu.VMEM((n_pair, 1, 2 * qt), F32), pltpu.VMEM((n_pair, 1, 2 * qt), F32),
                        pltpu.VMEM((n_pair, LANES, 2 * qt), F32)],
        compiler_params=_cparams("arbitrary", "arbitrary"), name="moba_prompt",
    )(q_b, kt, kb_rows, vb_blk)


def _merge_kernel(precise, x_ref, oa_ref, ob_ref, gm_ref, wa_ref, wb_ref, wo_ref, nf_ref, wr_hi, wr_lo, br_ref,
                  h_o, xn_o, lg_o):
    gm = gm_ref[...]
    u = (gm[:, :D_MODEL] * _mm(oa_ref[...], wa_ref[...], precise)
         + gm[:, D_MODEL:] * _mm(ob_ref[...], wb_ref[...], precise))
    h = x_ref[...] + _mm(u, wo_ref[...], precise)
    h_o[...] = h
    xn = h * lax.rsqrt(jnp.mean(h * h, axis=-1, keepdims=True) + EPS) * nf_ref[...]
    xn_o[...] = xn
    lg_o[...] = _dot3(xn, (wr_hi[...], wr_lo[...])) + br_ref[...]


def _merge(x2d, o_a, o_b, g_mrg, prm, tm, precise):
    n = x2d.shape[0]
    row = lambda i: (i, 0)
    const = lambda i: (0, 0)
    sfx = '_f32' if precise else ''
    ws = [prm[k] for k in ('w_up_a' + sfx, 'w_up_b' + sfx, 'w_out' + sfx, 'norm_ffn', 'wr_hi', 'wr_lo', 'b_router')]
    return pl.pallas_call(
        functools.partial(_merge_kernel, precise), grid=(n // tm,),
        in_specs=[pl.BlockSpec((tm, D_MODEL), row), pl.BlockSpec((tm, 512), row), pl.BlockSpec((tm, 512), row),
                  pl.BlockSpec((tm, 2 * D_MODEL), row)] + [pl.BlockSpec(w.shape, const) for w in ws],
        out_specs=[pl.BlockSpec((tm, D_MODEL), row), pl.BlockSpec((tm, D_MODEL), row), pl.BlockSpec((tm, LANES), row)],
        out_shape=[jax.ShapeDtypeStruct((n, D_MODEL), F32), jax.ShapeDtypeStruct((n, D_MODEL), F32),
                   jax.ShapeDtypeStruct((n, LANES), F32)],
        compiler_params=_cparams("parallel"), name="merge_router_sample" if precise else "merge_router",
    )(x2d, o_a, o_b, g_mrg, *ws)


def _moe_kernel(be_ref, bv_ref, x_ref, wgu_ref, bgu_ref, wd_ref, bd_ref, y_ref, wgu16_ref, wd16_ref):
    i = pl.program_id(0)

    @pl.when(jnp.logical_or(i == 0, be_ref[i] != be_ref[jnp.maximum(i - 1, 0)]))
    def _():
        wgu16_ref[...] = wgu_ref[0].astype(BF16)
        wd16_ref[...] = wd_ref[0].astype(BF16)

    @pl.when(bv_ref[i] > 0)
    def _():
        h = _dot(x_ref[...].astype(BF16), wgu16_ref[...]) + bgu_ref[0]
        gate = jnp.minimum(h[:, :D_FF], SWIGLU_LIMIT)
        up = jnp.clip(h[:, D_FF:], -SWIGLU_LIMIT, SWIGLU_LIMIT)
        act = (up + 1.0) * gate * jax.nn.sigmoid(SWIGLU_ALPHA * gate)
        y_ref[...] = _dot(act.astype(BF16), wd16_ref[...]) + bd_ref[0]

    @pl.when(bv_ref[i] == 0)
    def _():
        y_ref[...] = jnp.zeros(y_ref.shape, F32)


def _moe_blocks(block_e, block_valid, x_sorted, w_gu, b_gu, w_down, b_down):
    cap = x_sorted.shape[0]
    tm = MOE_TILE
    row = lambda i, be, bv: (i, 0)
    exp = lambda i, be, bv: (be[i], 0, 0)
    grid_spec = pltpu.PrefetchScalarGridSpec(
        num_scalar_prefetch=2, grid=(cap // tm,),
        in_specs=[pl.BlockSpec((tm, D_MODEL), row),
                  pl.BlockSpec((1, D_MODEL, 2 * D_FF), exp), pl.BlockSpec((1, 1, 2 * D_FF), exp),
                  pl.BlockSpec((1, D_FF, D_MODEL), exp), pl.BlockSpec((1, 1, D_MODEL), exp)],
        out_specs=pl.BlockSpec((tm, D_MODEL), row),
        scratch_shapes=[pltpu.VMEM((D_MODEL, 2 * D_FF), BF16), pltpu.VMEM((D_FF, D_MODEL), BF16)])
    return pl.pallas_call(
        _moe_kernel, grid_spec=grid_spec, out_shape=jax.ShapeDtypeStruct((cap, D_MODEL), F32),
        compiler_params=_cparams("arbitrary"), name="moe_experts",
    )(block_e, block_valid, x_sorted, w_gu, b_gu, w_down, b_down)


def _route(logits):
    n = logits.shape[0]
    top_v, top_e = lax.top_k(logits, TOP_K)
    gates = jax.nn.softmax(top_v, axis=-1)
    onehot = jnp.sum((top_e[:, :, None] == jnp.arange(N_EXPERTS)[None, None, :]).astype(jnp.int32), axis=1)
    csum = jnp.cumsum(onehot, axis=0)
    counts = csum[-1]
    padded = (counts + MOE_TILE - 1) // MOE_TILE * MOE_TILE
    pend = jnp.cumsum(padded)
    rank = jnp.take_along_axis(csum - onehot, top_e, axis=1)
    slot = (pend - padded)[top_e] + rank
    n_blocks = -(-(n * TOP_K) // MOE_TILE) + N_EXPERTS
    cap = n_blocks * MOE_TILE
    starts = jnp.arange(n_blocks, dtype=jnp.int32) * MOE_TILE
    block_e = jnp.minimum(jnp.sum((pend[None, :] <= starts[:, None]).astype(jnp.int32), axis=1), N_EXPERTS - 1)
    block_valid = (starts < pend[-1]).astype(jnp.int32)
    assert cap * n < 2 ** 31
    tok = jnp.broadcast_to(jnp.arange(n, dtype=jnp.int32)[:, None], (n, TOP_K))
    s_tok = lax.sort((slot * n + tok).reshape(-1)) % n
    per_slot = lambda v: jnp.broadcast_to(v[block_e][:, None], (n_blocks, MOE_TILE)).reshape(cap)
    rank_in_e = jnp.arange(cap, dtype=jnp.int32) - per_slot(pend - padded)
    used = rank_in_e < per_slot(counts)
    src = jnp.clip(per_slot(jnp.cumsum(counts) - counts) + rank_in_e, 0, n * TOP_K - 1)
    slot_tok = jnp.where(used, s_tok[src], 0)
    return slot, gates, slot_tok, block_e, block_valid


def _column(ref, b):
    x = ref[...]
    return jnp.sum(jnp.where(_lane_iota(x.shape) == b, x, 0.0), axis=-1, keepdims=True)


def _fold_lanes(x):
    return sum(x[..., i * LANES:(i + 1) * LANES] for i in range(x.shape[-1] // LANES))


def _gqa_scores(q3, kt):
    g = NSA_GROUP
    return jnp.concatenate([jnp.sum(q3[i * g:(i + 1) * g] * kt[i * HEAD_DIM:(i + 1) * HEAD_DIM][None], axis=1)
                            for i in range(NSA_KV_HEADS)], axis=0)


def _gqa_weighted(p, vt):
    g = NSA_GROUP
    return jnp.concatenate([p[i * g:(i + 1) * g][:, None, :] * vt[i * HEAD_DIM:(i + 1) * HEAD_DIM][None]
                            for i in range(NSA_KV_HEADS)], axis=0)


_CWS_NAMES = ('pka', 'pkb', 'k1a_hi', 'k1a_lo', 'k1b_hi', 'k1b_lo', 'k2t',
              'pva', 'pvb', 'v1a_hi', 'v1a_lo', 'v1b_hi', 'v1b_lo', 'v2t', 'gkc_col')


def _nsa_sample_kernel(pt_ref, qt_ref, g_ref, ksn_ref, vsn_ref, kwn_ref, vwn_ref, wk_ref, wv_ref, *rest):
    n_seq = g_ref.shape[0]
    n_pg = (len(rest) - len(_CWS_NAMES) - 4) // (4 * n_seq)
    n_pages = 4 * n_seq * n_pg
    cw = dict(zip(_CWS_NAMES, rest[n_pages:n_pages + len(_CWS_NAMES)]))
    ov_ref, e_ref, o_ref, rows_ref = rest[n_pages + len(_CWS_NAMES):]
    step = pl.program_id(0)
    fixed = (cw, ov_ref, e_ref, rows_ref, qt_ref, g_ref, ksn_ref, vsn_ref, kwn_ref, vwn_ref, wk_ref, wv_ref)
    outs = [_nsa_sample_one(step * n_seq + j, j, rest[4 * j * n_pg:4 * (j + 1) * n_pg], *fixed) for j in range(n_seq)]

    @pl.when(step == 0)
    def _():
        o_ref[...] = jnp.zeros(o_ref.shape, F32)

    for j, o in enumerate(outs):
        o_ref[...] = jnp.where(_lane_iota(o_ref.shape) == step * n_seq + j, o, o_ref[...])


def _nsa_sample_one(b, j, pages, cw, ov_ref, e_ref, rows_ref, qt_ref, g_ref, ksn_ref, vsn_ref, kwn_ref, vwn_ref,
                    wk_ref, wv_ref):
    n_pg = len(pages) // 4
    ck_pages, cv_pages, sk_pages, sv_pages = (pages[k * n_pg:(k + 1) * n_pg] for k in range(4))
    past = n_pg * PAGE_SIZE

    def compress_t(pages, p):
        for i, pg in enumerate(pages):
            rows_ref[j, i * PAGE_SIZE:(i + 1) * PAGE_SIZE, :] = pg[0].T
        xs = [rows_ref[j, pl.ds(l, past // CMP_STRIDE, stride=CMP_STRIDE), :] for l in range(CMP_STRIDE)]
        h = _compress_hidden(xs, cw['p%sa' % p][...], cw['p%sb' % p][...],
                             (cw['%s1a_hi' % p][...], cw['%s1a_lo' % p][...]),
                             (cw['%s1b_hi' % p][...], cw['%s1b_lo' % p][...]), True)
        w2t = cw['%s2t' % p][...]
        return jnp.concatenate([_dot3(w2t, h[:LANES], nt=True), _dot3(w2t, h[LANES:], nt=True)], axis=0)

    kct = compress_t(ck_pages, 'k').reshape(NSA_KV_HEADS, HEAD_DIM, LANES)
    ss = jnp.sum(kct * kct, axis=1, keepdims=True)
    kct = (kct * lax.rsqrt(ss * (1.0 / HEAD_DIM) + EPS) * cw['gkc_col'][...][None]).reshape(LANES, LANES)
    vct = compress_t(cv_pages, 'v')

    q3 = (_column(qt_ref, b) * ATT_SCALE).reshape(NSA_HEADS, HEAD_DIM, 1)

    n_c = (past + 1 - CMP_BLOCK) // CMP_STRIDE + 1
    s_c = _gqa_scores(q3, kct)
    p_c = _masked_softmax(s_c, _lane_iota(s_c.shape) < n_c)
    o_c = jnp.sum(_gqa_weighted(p_c, vct), axis=-1, keepdims=True)
    row = _row_iota((NSA_HEADS, LANES))
    i0 = jnp.sum(p_c[:NSA_GROUP], axis=0, keepdims=True)
    i1 = jnp.sum(p_c[NSA_GROUP:], axis=0, keepdims=True)
    imp = _split_dot(jnp.where(row < NSA_GROUP, i0, i1), ov_ref[...])
    sel = _nsa_select(imp, _lane_iota(imp.shape), past // SEL_BLOCK, -1)
    picked = _dot(jnp.where(sel, 1.0, 0.0).astype(BF16), e_ref[...]) > 0.5

    def attend_with_new(s, valid, kn_ref, vn_ref, weighted):
        kn = _column(kn_ref, b)
        vn = _column(vn_ref, b)
        s = jnp.where(valid, s, NEG)
        s_n = _gqa_scores(q3, kn)
        m = jnp.maximum(jnp.max(s, axis=-1, keepdims=True), s_n)
        p = jnp.where(valid, jnp.exp(s - m), 0.0)
        p_n = jnp.exp(s_n - m)
        num = jnp.sum(weighted(p), axis=-1, keepdims=True) + _gqa_weighted(p_n, vn)
        den = jnp.maximum(jnp.sum(p, axis=-1, keepdims=True) + p_n, 1e-30)
        return num / den[:, :, None]

    s_s = jnp.concatenate([_gqa_scores(q3, pg[0]) for pg in sk_pages], axis=1)
    o_s = attend_with_new(
        s_s, picked, ksn_ref, vsn_ref,
        lambda p: sum(_gqa_weighted(p[:, i * PAGE_SIZE:(i + 1) * PAGE_SIZE], pg[0]) for i, pg in enumerate(sv_pages)))

    wb = wk_ref.shape[2]
    s_w = _gqa_scores(q3, wk_ref[j])
    inside = (past - wb + _lane_iota(s_w.shape)) > (past - WINDOW)
    o_w = attend_with_new(s_w, inside, kwn_ref, vwn_ref, lambda p: _fold_lanes(_gqa_weighted(p, wv_ref[j])))

    g = jnp.broadcast_to(g_ref[j], (NSA_HEADS, LANES))
    lane = _lane_iota(g.shape)
    gate = lambda br: jnp.sum(jnp.where(lane == br * NSA_HEADS + row, g, 0.0), axis=-1, keepdims=True)[:, :, None]
    return (gate(0) * o_c + gate(1) * o_s + gate(2) * o_w).reshape(NSA_HEADS * HEAD_DIM, 1)


def _nsa_sample(page_table, qt, g_nsa, ksn, vsn, kwn, vwn, win_kt, win_vt, cache_ck, cache_cv, cache_skt, cache_svt,
                prm):
    db, n_pg = page_table.shape
    wb = win_kt.shape[2]
    n_seq = NSA_SAMPLE_SEQS
    assert db % n_seq == 0
    per_b = lambda i, pt: (i, 0, 0)
    const2 = lambda i, pt: (0, 0)

    def page_spec(j, p):
        return pl.BlockSpec((1, PAGE_SIZE, LANES), lambda i, pt, j=j, p=p: (pt[i * n_seq + j, p], 0, 0))

    cws = [prm['s_' + k] for k in _CWS_NAMES]
    full = lambda a: pl.BlockSpec(a.shape, const2)
    in_specs = ([full(qt), pl.BlockSpec((n_seq, 1, LANES), per_b), full(ksn), full(vsn), full(kwn), full(vwn),
                 pl.BlockSpec((n_seq, LANES, wb), per_b), pl.BlockSpec((n_seq, LANES, wb), per_b)]
                + [page_spec(j, p) for j in range(n_seq) for _ in range(4) for p in range(n_pg)]
                + [full(w) for w in cws] + [full(prm['ov']), full(prm['e_key'])])
    grid_spec = pltpu.PrefetchScalarGridSpec(num_scalar_prefetch=1, grid=(db // n_seq,), in_specs=in_specs,
                                             out_specs=pl.BlockSpec((NSA_HEADS * HEAD_DIM, db), const2),
                                             scratch_shapes=[pltpu.VMEM((n_seq, n_pg * PAGE_SIZE, LANES), F32)])
    pages = ([cache_ck] * n_pg + [cache_cv] * n_pg + [cache_skt] * n_pg + [cache_svt] * n_pg) * n_seq
    return pl.pallas_call(
        _nsa_sample_kernel, grid_spec=grid_spec, out_shape=jax.ShapeDtypeStruct((NSA_HEADS * HEAD_DIM, db), F32),
        compiler_params=_cparams("arbitrary"), name="nsa_sample",
    )(page_table, qt, g_nsa, ksn, vsn, kwn, vwn, win_kt, win_vt, *pages, *cws, prm['ov'], prm['e_key'])


def _moba_sample_kernel(pt_ref, qt_ref, kn_ref, vn_ref, *rest):
    n_pg = (len(rest) - 1) // 2
    k_pages = rest[:n_pg]
    v_pages = rest[n_pg:2 * n_pg]
    o_ref = rest[2 * n_pg]
    b = pl.program_id(0)
    pages_per_blk = MOBA_BLOCK // PAGE_SIZE
    n_b = n_pg // pages_per_blk
    heads3 = lambda x: x.reshape(MOBA_HEADS, HEAD_DIM, x.shape[-1])

    q3 = heads3(_column(qt_ref, b) * ATT_SCALE)
    kn3 = heads3(_column(kn_ref, b))
    vn3 = heads3(_column(vn_ref, b))
    s_pages = [jnp.sum(heads3(pg[0]) * q3, axis=1) for pg in k_pages]
    s_new = jnp.sum(q3 * kn3, axis=1)

    lane = _lane_iota((MOBA_HEADS, LANES))
    s_g = jnp.full((MOBA_HEADS, LANES), -jnp.inf, F32)
    for n in range(n_b):
        tot = sum(s_pages[n * pages_per_blk + i] for i in range(pages_per_blk))
        s_g = jnp.where(lane == n, jnp.sum(tot, axis=-1, keepdims=True), s_g)
    sel = _select_top(s_g, min(MOBA_TOPK, n_b), lane.astype(F32), -1)
    sel_blk = [jnp.sum(jnp.where(jnp.logical_and(sel, lane == n), 1.0, 0.0), axis=-1, keepdims=True) > 0.5
               for n in range(n_b)]

    m = s_new
    for i in range(n_pg):
        m = jnp.maximum(m, jnp.max(jnp.where(sel_blk[i // pages_per_blk], s_pages[i], NEG), axis=-1, keepdims=True))
    p_new = jnp.exp(s_new - m)
    l = p_new
    acc = jnp.zeros((MOBA_HEADS, HEAD_DIM, PAGE_SIZE), F32)
    for i in range(n_pg):
        p = jnp.where(sel_blk[i // pages_per_blk], jnp.exp(s_pages[i] - m), 0.0)
        l = l + jnp.sum(p, axis=-1, keepdims=True)
        acc = acc + heads3(v_pages[i][0]) * p[:, None, :]
    o = (jnp.sum(acc, axis=-1, keepdims=True) + p_new[:, :, None] * vn3) / jnp.maximum(l, 1e-30)[:, :, None]

    @pl.when(b == 0)
    def _():
        o_ref[...] = jnp.zeros(o_ref.shape, F32)

    o_ref[...] = jnp.where(_lane_iota(o_ref.shape) == b, o.reshape(MOBA_HEADS * HEAD_DIM, 1), o_ref[...])


def _moba_sample(page_table, qt, knt, vnt, cache_kt, cache_vt):
    db, n_pg = page_table.shape
    w = qt.shape[0]
    assert (n_pg * PAGE_SIZE) % MOBA_BLOCK == 0
    const2 = lambda i, pt: (0, 0)

    def page_spec(p):
        return pl.BlockSpec((1, w, PAGE_SIZE), lambda i, pt, p=p: (pt[i, p], 0, 0))

    in_specs = [pl.BlockSpec((w, db), const2)] * 3 + [page_spec(p) for p in range(n_pg)] * 2
    grid_spec = pltpu.PrefetchScalarGridSpec(num_scalar_prefetch=1, grid=(db,), in_specs=in_specs,
                                             out_specs=pl.BlockSpec((w, db), const2))
    return pl.pallas_call(
        _moba_sample_kernel, grid_spec=grid_spec, out_shape=jax.ShapeDtypeStruct((w, db), F32),
        compiler_params=_cparams("arbitrary"), name="moba_sample",
    )(page_table, qt, knt, vnt, *([cache_kt] * n_pg), *([cache_vt] * n_pg))


def _split_weight(w):
    c = w * (2.0 ** 16 + 1.0)
    hi = c - (c - w)
    return hi.astype(BF16), (w - hi).astype(BF16)


def _prepare(norm_attn, w_in, g_qa, g_ka, g_kc, cmp_pos_k, cmp_w1_k, cmp_w2_k, cmp_pos_v, cmp_w1_v, cmp_w2_v,
             g_qb, g_kb, w_up_a, w_up_b, w_out, norm_ffn, w_router, b_router):
    qa = NSA_HEADS * HEAD_DIM
    kva = NSA_KV_HEADS * HEAD_DIM
    qb = MOBA_HEADS * HEAD_DIM
    widths = [qa, kva, kva, kva, kva, kva, kva, 3 * NSA_HEADS, qb, qb, qb, 2 * D_MODEL]
    cuts = np.cumsum([0] + widths)
    part = lambda i: w_in[:, cuts[i]:cuts[i + 1]]
    gcols = np.array([h * 3 + j for j in range(3) for h in range(NSA_HEADS)])
    wgnt = jnp.zeros((LANES, D_MODEL), F32).at[:3 * NSA_HEADS].set(part(7)[:, gcols].T)
    two = lambda g: jnp.concatenate([g, g]).reshape(1, LANES)
    lane = np.arange(LANES)
    c_idx = np.arange(LANES)[:, None] * CMP_STRIDE
    s_idx = np.arange(LANES)[None, :] * SEL_BLOCK
    ov = ((c_idx < s_idx + SEL_BLOCK) & (c_idx + CMP_BLOCK > s_idx) & (np.arange(LANES)[:, None] < LANES - 1))
    half_w = CMP_STRIDE * HEAD_DIM

    def w2wide(w2):
        z = jnp.zeros_like(w2)
        return jnp.concatenate([w2, z], axis=1).astype(BF16), jnp.concatenate([z, w2], axis=1).astype(BF16)

    k2g0, k2g1 = w2wide(cmp_w2_k)
    wr_hi, wr_lo = _split_weight(jnp.zeros((D_MODEL, LANES), F32).at[:, :N_EXPERTS].set(w_router))
    prm = {
        'norm_attn': norm_attn.reshape(1, D_MODEL),
        'wqat_f32': part(0).T, 'wkvat_f32': w_in[:, cuts[1]:cuts[7]].T, 'wgnt_f32': wgnt,
        'wqbt_f32': part(8).T, 'wkbt_f32': part(9).T, 'wvbt_f32': part(10).T, 'wgm_f32': part(11),
        'g_col': jnp.stack([g_qa, g_qb, g_ka[0], g_ka[1], g_ka[2], g_kb], axis=1),
        'gkc': two(g_kc),
        'bd': jnp.asarray((lane[:, None] // HEAD_DIM) == (lane[None, :] // HEAD_DIM), BF16),
        'pka': cmp_pos_k[:CMP_STRIDE].reshape(1, half_w), 'pkb': cmp_pos_k[CMP_STRIDE:].reshape(1, half_w),
        'pva': cmp_pos_v[:CMP_STRIDE].reshape(1, half_w), 'pvb': cmp_pos_v[CMP_STRIDE:].reshape(1, half_w),
        'k1a': cmp_w1_k[:half_w].astype(BF16), 'k1b': cmp_w1_k[half_w:].astype(BF16),
        'v1a': cmp_w1_v[:half_w].astype(BF16), 'v1b': cmp_w1_v[half_w:].astype(BF16),
        'k2g0': k2g0, 'k2g1': k2g1, 'v2t': cmp_w2_v.T.astype(BF16),
        'ov': jnp.asarray(ov, BF16), 'ovt': jnp.asarray(ov.T, BF16),
        'w_up_a_f32': w_up_a, 'w_up_b_f32': w_up_b, 'w_out_f32': w_out,
        'norm_ffn': norm_ffn.reshape(1, D_MODEL),
        'wr_hi': wr_hi, 'wr_lo': wr_lo,
        'b_router': jnp.zeros((1, LANES), F32).at[0, :N_EXPERTS].set(b_router),
    }
    for k in _INPROJ_W + ('w_up_a', 'w_up_b', 'w_out'):
        prm[k] = prm[k + '_f32'].astype(BF16)
    for p, w1, w2 in (('k', cmp_w1_k, cmp_w2_k), ('v', cmp_w1_v, cmp_w2_v)):
        for half, w in (('a', w1[:half_w]), ('b', w1[half_w:])):
            prm['s_%s1%s_hi' % (p, half)], prm['s_%s1%s_lo' % (p, half)] = _split_weight(w)
        prm['s_%s2t' % p] = w2.T
        prm['s_p%sa' % p] = prm['p%sa' % p]
        prm['s_p%sb' % p] = prm['p%sb' % p]
    prm['s_gkc_col'] = g_kc.reshape(HEAD_DIM, 1)
    return prm


def _rope_tables(pos):
    half = HEAD_DIM // 2
    inv = ROPE_THETA ** (-np.arange(half, dtype=np.float64) / half)
    ang = inv[:, None] * np.asarray(pos, np.float64)[None, :]
    return jnp.asarray(np.cos(ang), F32), jnp.asarray(np.sin(ang), F32)


def _expand_matrix(n_keys, block):
    e = (np.arange(n_keys)[None, :] // block) == np.arange(LANES)[:, None]
    return jnp.asarray(e, BF16)


def kernel(x_prompt, x_sample, cache_nsa_cmp_k, cache_nsa_cmp_v, cache_nsa_sel_k, cache_nsa_sel_v, cache_nsa_win_k, cache_nsa_win_v, cache_moba_k, cache_moba_v, page_table, norm_attn, w_in, g_qa, g_ka, g_kc, cmp_pos_k, cmp_w1_k, cmp_w2_k, cmp_pos_v, cmp_w1_v, cmp_w2_v, g_qb, g_kb, w_up_a, w_up_b, w_out, norm_ffn, w_router, b_router, w_gu, b_gu, w_down, b_down):
    B, T, _ = x_prompt.shape
    DB, DS, _ = x_sample.shape
    assert DS == 1
    n_pg = page_table.shape[1]
    past = n_pg * PAGE_SIZE
    wb = cache_nsa_win_k.shape[1]

    prm = _prepare(norm_attn, w_in, g_qa, g_ka, g_kc, cmp_pos_k, cmp_w1_k, cmp_w2_k, cmp_pos_v, cmp_w1_v, cmp_w2_v,
                   g_qb, g_kb, w_up_a, w_up_b, w_out, norm_ffn, w_router, b_router)
    prm['e_key'] = _expand_matrix(past, SEL_BLOCK)

    P = _in_proj(x_prompt.reshape(B * T, D_MODEL), B, _rope_tables(np.arange(T)), prm, KEY_BLOCK, False)
    r3 = lambda a: a.reshape(B, T, a.shape[-1])
    kc_cmp, vct_cmp = _compress_prompt(r3(P['kc_row']), r3(P['vc_row']), prm)
    oa_p = _nsa_prompt(P['qa'], P['gn'], kc_cmp, vct_cmp, P['ks_rows'], P['vs_blk'], P['kw_rows'], P['vw_blk'], prm)
    ob_p = _moba_prompt(P['qb'], P['kb'], P['kb_rows'], P['vb_blk'])
    h_p, xn_p, lg_p = _merge(x_prompt.reshape(B * T, D_MODEL), oa_p.reshape(B * T, 512), ob_p.reshape(B * T, 512),
                             P['gm'], prm, MERGE_TILE, False)

    S = _in_proj(x_sample.reshape(DB, D_MODEL), 1, _rope_tables(np.full((DB,), past)), prm, DB, True)
    cols2 = lambda c: jnp.transpose(c, (0, 2, 3, 1)).reshape(c.shape[0], c.shape[2] * c.shape[3], c.shape[1])
    oa_t = _nsa_sample(page_table, S['qa'][0], S['gn'][0].T.reshape(DB, 1, LANES),
                       S['ks'][0], S['vs'][0], S['kw'][0], S['vw'][0],
                       cols2(cache_nsa_win_k), cols2(cache_nsa_win_v),
                       cols2(cache_nsa_cmp_k), cols2(cache_nsa_cmp_v), cols2(cache_nsa_sel_k), cols2(cache_nsa_sel_v),
                       prm)
    ob_t = _moba_sample(page_table, S['qb'][0], S['kb'][0], S['vb'][0], cols2(cache_moba_k), cols2(cache_moba_v))
    h_s, xn_s, lg_s = _merge(x_sample.reshape(DB, D_MODEL), oa_t.T, ob_t.T, S['gm'], prm, DB, True)

    xn_all = jnp.concatenate([xn_p, xn_s], axis=0)
    logits = jnp.concatenate([lg_p, lg_s], axis=0)[:, :N_EXPERTS]
    slot, gates, slot_tok, block_e, block_valid = _route(logits)
    y_slot = _moe_blocks(block_e, block_valid, xn_all[slot_tok],
                         w_gu, b_gu.reshape(N_EXPERTS, 1, 2 * D_FF), w_down, b_down.reshape(N_EXPERTS, 1, D_MODEL))
    n_tok = B * T + DB
    picked = y_slot[slot.T.reshape(-1)].reshape(TOP_K, n_tok, D_MODEL)
    mix = lambda lo, hi: sum(gates[lo:hi, k:k + 1] * picked[k, lo:hi] for k in range(TOP_K))
    y_prompt = (h_p + mix(0, B * T)).reshape(B, T, D_MODEL)
    y_sample = (h_s + mix(B * T, n_tok)).reshape(DB, DS, D_MODEL)

    def kv4(a):
        n, w, t = a.shape
        return jnp.transpose(a.reshape(n, w // HEAD_DIM, HEAD_DIM, t), (0, 3, 1, 2))

    new4 = lambda a: kv4(a).reshape(DB, 1, a.shape[1] // HEAD_DIM, HEAD_DIM)
    wp = min(WINDOW, T)
    ws = min(WINDOW, wb + DS)
    kw_all = jnp.concatenate([cache_nsa_win_k, new4(S['kw'])], axis=1)[:, -ws:]
    vw_all = jnp.concatenate([cache_nsa_win_v, new4(S['vw'])], axis=1)[:, -ws:]
    return (y_prompt, y_sample,
            kv4(P['kc']), kv4(P['vc']), kv4(P['ks']), kv4(P['vs']),
            kv4(P['kw'][:, :, T - wp:]), kv4(P['vw'][:, :, T - wp:]), kv4(P['kb']), kv4(P['vb']),
            new4(S['kc']), new4(S['vc']), new4(S['ks']), new4(S['vs']), kw_all, vw_all,
            new4(S['kb']), new4(S['vb']))
```

```python
import functools

import numpy as np
import jax
import jax.numpy as jnp
from jax import lax
from jax.experimental import pallas as pl
from jax.experimental.pallas import tpu as pltpu

F32 = jnp.float32
BF16 = jnp.bfloat16

D_MODEL = 1024
PAGE_SIZE = 128
HEAD_DIM = 64
NSA_HEADS = 8
NSA_KV_HEADS = 2
NSA_GROUP = NSA_HEADS // NSA_KV_HEADS
CMP_BLOCK = 32
CMP_STRIDE = 16
CMP_HIDDEN = 256
SEL_BLOCK = 64
N_SEL = 8
WINDOW = 512
MOBA_HEADS = 8
MOBA_BLOCK = 256
MOBA_TOPK = 3
N_EXPERTS = 32
TOP_K = 4
D_FF = D_MODEL
SWIGLU_LIMIT = 7.0
SWIGLU_ALPHA = 1.702
ROPE_THETA = 10000.0
EPS = 1e-6
NEG = -1e30
POS_BIG = 1e30

LANES = 128
HALF = HEAD_DIM
ATT_SCALE = HEAD_DIM ** -0.5
Q_TILE = 128
KEY_BLOCK = MOBA_BLOCK
MOE_TILE = 512
MERGE_TILE = 512
NSA_SAMPLE_SEQS = 1
VMEM_LIMIT = 56 * 1024 * 1024
FAR = -(1 << 20)


def _cparams(*sem):
    return pltpu.CompilerParams(dimension_semantics=sem, vmem_limit_bytes=VMEM_LIMIT)


def _lane_iota(shape):
    return lax.broadcasted_iota(jnp.int32, shape, len(shape) - 1)


def _row_iota(shape):
    return lax.broadcasted_iota(jnp.int32, shape, 0)


def _dot(a, b):
    return jnp.dot(a, b, preferred_element_type=F32)


def _dot_nt(a, b):
    return lax.dot_general(a, b, (((1,), (1,)), ((), ())), preferred_element_type=F32)


def _split(a):
    hi = a.astype(BF16)
    lo = (a - hi.astype(F32)).astype(BF16)
    return hi, lo


def _split_dot(a, b):
    hi, lo = _split(a)
    return _dot(hi, b) + _dot(lo, b)


def _dot_split(a, b):
    hi, lo = _split(b)
    return _dot(a, hi) + _dot(a, lo)


def _dot3(a, b, nt=False):
    mm = _dot_nt if nt else _dot
    a_hi, a_lo = _split(a)
    b_hi, b_lo = b if isinstance(b, tuple) else _split(b)
    return mm(a_hi, b_hi) + (mm(a_hi, b_lo) + mm(a_lo, b_hi))


def _mm(a, w, precise):
    return _dot3(a, w) if precise else _dot(a.astype(BF16), w)


def _masked_softmax(s, mask):
    s = jnp.where(mask, s, NEG)
    m = jnp.max(s, axis=-1, keepdims=True)
    e = jnp.where(mask, jnp.exp(s - m), 0.0)
    return e / jnp.maximum(jnp.sum(e, axis=-1, keepdims=True), 1e-30)


def _head_norm(y, gain, bd):
    ss = _split_dot(y * y, bd)
    return y * lax.rsqrt(ss * (1.0 / HEAD_DIM) + EPS) * gain


def _select_top(score, n_pick, idx_f, axis):
    sel = jnp.zeros(score.shape, jnp.bool_)
    for _ in range(n_pick):
        m = jnp.max(score, axis=axis, keepdims=True)
        first = jnp.min(jnp.where(score == m, idx_f, 1e9), axis=axis, keepdims=True)
        hit = idx_f == first
        sel = jnp.logical_or(sel, hit)
        score = jnp.where(hit, -jnp.inf, score)
    return sel


def _rank_select(score_row, n_pick):
    a = jnp.broadcast_to(score_row, (LANES, LANES))
    b = a.T
    ahead = jnp.where(b > a, 1.0, jnp.where(b == a, jnp.where(_row_iota(a.shape) < _lane_iota(a.shape), 1.0, 0.0),
                                            0.0))
    return jnp.sum(ahead, axis=0, keepdims=True) < n_pick


def _nsa_select(imp, blk, cur, axis):
    forced = (blk == 0) | (blk == cur) | (blk == cur - 1)
    valid = blk <= cur
    score = jnp.where(forced, POS_BIG, jnp.where(valid, imp, -jnp.inf))
    sel = _select_top(score, N_SEL, blk.astype(F32), axis)
    return jnp.logical_and(sel, valid)


def _attend_block(m_ref, l_ref, acc_ref, i, kmat, qmat, vt, first, bias=None):
    s = _dot(kmat, qmat)
    if bias is not None:
        s = s + bias
    m_cur = jnp.max(s, axis=0, keepdims=True)
    if first:
        p = jnp.exp(s - m_cur)
        m_ref[i] = m_cur
        l_ref[i] = jnp.sum(p, axis=0, keepdims=True)
        acc_ref[i] = _dot(vt, p.astype(BF16))
    else:
        m_prev = m_ref[i]
        m_new = jnp.maximum(m_prev, m_cur)
        alpha = jnp.exp(m_prev - m_new)
        p = jnp.exp(s - m_new)
        m_ref[i] = m_new
        l_ref[i] = alpha * l_ref[i] + jnp.sum(p, axis=0, keepdims=True)
        acc_ref[i] = alpha * acc_ref[i] + _dot(vt, p.astype(BF16))


def _inproj_kernel(precise, x_ref, cos_ref, sin_ref, gn_ref, wqat, wkvat, wgnt, wqbt, wkbt, wvbt, wgm, g_col, *outs):
    (qa_o, kcr_o, vcr_o, kc_o, vc_o, ks_o, vs_o, kw_o, vw_o, ksr_o, vsb_o, kwr_o, vwb_o,
     gn_o, qb_o, kb_o, vb_o, kbr_o, vbb_o, gm_o) = outs
    x = x_ref[...]
    xn = x * lax.rsqrt(jnp.mean(x * x, axis=-1, keepdims=True) + EPS) * gn_ref[...]
    xa = xn if precise else xn.astype(BF16)
    mm_t = (lambda wt: _dot3(wt, xn, nt=True)) if precise else (lambda wt: _dot_nt(wt, xa))
    half = HEAD_DIM // 2
    cos = cos_ref[...][None]
    sin = sin_ref[...][None]

    def tile(j):
        return slice(j * LANES, (j + 1) * LANES)

    def qk_t(y, gain_col):
        y = y.reshape(2, HEAD_DIM, y.shape[-1])
        y = y * lax.rsqrt(jnp.mean(y * y, axis=1, keepdims=True) + EPS) * gain_col[None]
        x1, x2 = y[:, :half], y[:, half:]
        return jnp.concatenate([x1 * cos - x2 * sin, x2 * cos + x1 * sin], axis=1).reshape(LANES, -1)

    y_qa = mm_t(wqat[...])
    y_qb = mm_t(wqbt[...])
    for j in range(4):
        qa_o[0, tile(j), :] = qk_t(y_qa[tile(j)], g_col[:, 0:1]).astype(qa_o.dtype)
        qb_o[0, tile(j), :] = qk_t(y_qb[tile(j)], g_col[:, 1:2]).astype(qb_o.dtype)
    y_kva = mm_t(wkvat[...])
    nsa_kv = ((kc_o, vc_o, kcr_o, vcr_o), (ks_o, vs_o, ksr_o, vsb_o), (kw_o, vw_o, kwr_o, vwb_o))
    for i, (k_o, v_o, k2_o, v2_o) in enumerate(nsa_kv):
        k = qk_t(y_kva[tile(2 * i)], g_col[:, 2 + i:3 + i])
        v = y_kva[tile(2 * i + 1)]
        k_o[0] = k
        v_o[0] = v
        if i == 0:
            k2_o[...] = k.T
            v2_o[...] = v.T
        else:
            k2_o[0, 0] = k.T.astype(BF16)
            v2_o[0, 0] = v.astype(BF16)
    gn_o[0] = jax.nn.sigmoid(mm_t(wgnt[...]))
    y_kb = mm_t(wkbt[...])
    y_vb = mm_t(wvbt[...])
    vb_o[0] = y_vb
    vbb_o[0, 0] = y_vb.astype(BF16)
    for j in range(4):
        k = qk_t(y_kb[tile(j)], g_col[:, 5:6])
        kb_o[0, tile(j), :] = k
        kbr_o[0, 0, :, tile(j)] = k.T.astype(BF16)
    gm_o[...] = jax.nn.sigmoid(_mm(xa, wgm[...], precise))


_INPROJ_W = ('wqat', 'wkvat', 'wgnt', 'wqbt', 'wkbt', 'wvbt', 'wgm')
_INPROJ_OUT = ('qa', 'kc_row', 'vc_row', 'kc', 'vc', 'ks', 'vs', 'kw', 'vw', 'ks_rows', 'vs_blk', 'kw_rows', 'vw_blk',
               'gn', 'qb', 'kb', 'vb', 'kb_rows', 'vb_blk', 'gm')


def _in_proj(x2d, n_batch, rope, prm, tm, precise):
    n = x2d.shape[0]
    t = n // n_batch
    nb = t // tm
    cos_t, sin_t = rope
    row = lambda i: (i, 0)
    const = lambda i: (0, 0)
    pos_t = lambda i: (0, i % nb)
    tr = lambda i: (i // nb, 0, i % nb)
    blk = lambda i: (i // nb, i % nb, 0, 0)
    ws = [prm[k + ('_f32' if precise else '')] for k in _INPROJ_W] + [prm['g_col']]
    in_specs = ([pl.BlockSpec((tm, D_MODEL), row), pl.BlockSpec((HEAD_DIM // 2, tm), pos_t),
                 pl.BlockSpec((HEAD_DIM // 2, tm), pos_t), pl.BlockSpec((1, D_MODEL), const)]
                + [pl.BlockSpec(w.shape, const) for w in ws])
    qdt = F32 if precise else BF16
    rows_out = lambda w, dt: (jax.ShapeDtypeStruct((n, w), dt), pl.BlockSpec((tm, w), row))
    t_out = lambda w, dt=F32: (jax.ShapeDtypeStruct((n_batch, w, t), dt), pl.BlockSpec((1, w, tm), tr))
    r_out = lambda w: (jax.ShapeDtypeStruct((n_batch, nb, tm, w), BF16), pl.BlockSpec((1, 1, tm, w), blk))
    b_out = lambda w: (jax.ShapeDtypeStruct((n_batch, nb, w, tm), BF16), pl.BlockSpec((1, 1, w, tm), blk))
    outs = ([t_out(512, qdt), rows_out(LANES, F32), rows_out(LANES, F32)] + [t_out(LANES)] * 6
            + [r_out(LANES), b_out(LANES), r_out(LANES), b_out(LANES)]
            + [t_out(LANES), t_out(512, qdt), t_out(512), t_out(512), r_out(512), b_out(512),
               rows_out(2 * D_MODEL, F32)])
    res = pl.pallas_call(
        functools.partial(_inproj_kernel, precise), grid=(n // tm,), in_specs=in_specs,
        out_specs=[o[1] for o in outs], out_shape=[o[0] for o in outs], compiler_params=_cparams("parallel"),
        name="in_proj_sample" if precise else "in_proj",
    )(x2d, cos_t, sin_t, prm['norm_attn'], *ws)
    return dict(zip(_INPROJ_OUT, res))


def _compress_hidden(xs, pos_a, pos_b, w1a, w1b, precise):
    lo = _lane_iota((LANES, LANES)) < HALF
    t0, t1 = [], []
    for l in range(0, CMP_STRIDE, 2):
        a, b = xs[l], xs[l + 1]
        t0.append(jnp.where(lo, a, pltpu.roll(b, HALF, 1)))
        t1.append(jnp.where(lo, pltpu.roll(a, HALF, 1), b))
    x = jnp.concatenate([jnp.concatenate(t0, axis=1), jnp.concatenate(t1, axis=1)], axis=0)
    first = _mm(x + pos_a, w1a, precise)
    second = _mm(x + pos_b, w1b, precise)
    n = LANES
    shifted = jnp.concatenate([pltpu.roll(second[:n], n - 1, 0), pltpu.roll(second[n:], n - 1, 0)], axis=0)
    return jax.nn.gelu(first + shifted)


_CW_NAMES = ('pka', 'pkb', 'k1a', 'k1b', 'k2g0', 'k2g1', 'pva', 'pvb', 'v1a', 'v1b', 'v2t', 'gkc', 'bd')


def _compress_prompt_kernel(k_ref, v_ref, *rest):
    cw = dict(zip(_CW_NAMES, rest[:len(_CW_NAMES)]))
    kc_o, vct_o = rest[len(_CW_NAMES):]
    n = k_ref.shape[1] // CMP_STRIDE

    def hidden(ref, p):
        xs = [ref[0, pl.ds(l, n, stride=CMP_STRIDE), :] for l in range(CMP_STRIDE)]
        return _compress_hidden(xs, cw['p%sa' % p][...], cw['p%sb' % p][...], cw['%s1a' % p][...],
                                cw['%s1b' % p][...], False).astype(BF16)

    hk = hidden(k_ref, 'k')
    kc = _dot(hk[:LANES], cw['k2g0'][...]) + _dot(hk[LANES:], cw['k2g1'][...])
    kc_o[0] = _head_norm(kc, cw['gkc'][...], cw['bd'][...])
    hv = hidden(v_ref, 'v')
    w2t = cw['v2t'][...]
    vct_o[0] = jnp.concatenate([_dot_nt(w2t, hv[:LANES]), _dot_nt(w2t, hv[LANES:])], axis=0)


def _compress_prompt(k_c, v_c, prm):
    b, t, _ = k_c.shape
    assert t // CMP_STRIDE == LANES
    ws = [prm[k] for k in _CW_NAMES]
    per_b = lambda i: (i, 0, 0)
    const = lambda i: (0, 0)
    return pl.pallas_call(
        _compress_prompt_kernel, grid=(b,),
        in_specs=[pl.BlockSpec((1, t, LANES), per_b)] * 2 + [pl.BlockSpec(w.shape, const) for w in ws],
        out_specs=[pl.BlockSpec((1, LANES, LANES), per_b)] * 2,
        out_shape=[jax.ShapeDtypeStruct((b, LANES, LANES), F32)] * 2,
        compiler_params=_cparams("parallel"), name="compress_prompt",
    )(k_c, v_c, *ws)


def _nsa_prompt_kernel(q_ref, g_ref, kc_ref, vct_ref, ks_ref, vs_ref, kw_ref, vw_ref, ovt_ref,
                       o_ref, qaug_ref, oc_ref, m_ref, l_ref, acc_ref):
    qt = Q_TILE
    kb = ks_ref.shape[2]
    s0 = pl.program_id(1) * qt
    t_q = s0 + _lane_iota((1, qt))
    head_cols = lambda h: slice(h * qt, (h + 1) * qt)
    zero = jnp.zeros((HEAD_DIM, qt), BF16)

    per_head = lambda x: jnp.concatenate([x] * NSA_HEADS, axis=1)
    blocks = []
    for h in range(NSA_HEADS):
        qh = q_ref[0, h * HEAD_DIM:(h + 1) * HEAD_DIM, :] * ATT_SCALE
        blocks.append(jnp.concatenate([qh, zero] if h < NSA_GROUP else [zero, qh], axis=0))
    qst = jnp.concatenate(blocks, axis=1)

    n_c = kc_ref.shape[1] - 1
    c_end = _row_iota((LANES, qt)) * CMP_STRIDE + (CMP_BLOCK - 1)
    ok = per_head(jnp.where(c_end <= jnp.minimum(t_q, (n_c - 1) * CMP_STRIDE + CMP_BLOCK - 1), 1.0, 0.0)) > 0.5
    s_c = jnp.where(ok, _dot(kc_ref[0].astype(BF16), qst), NEG)
    e = jnp.where(ok, jnp.exp(s_c - jnp.max(s_c, axis=0, keepdims=True)), 0.0)
    p_c = e / jnp.maximum(jnp.sum(e, axis=0, keepdims=True), 1e-30)
    oc_ref[...] = _dot(vct_ref[0].astype(BF16), p_c.astype(BF16))
    g4 = NSA_GROUP * qt
    imp = jnp.concatenate([sum(p_c[:, g * g4 + r * qt: g * g4 + (r + 1) * qt] for r in range(NSA_GROUP))
                           for g in range(NSA_KV_HEADS)], axis=1)
    imp = _dot_split(ovt_ref[...], imp)
    cur = (s0 + (_lane_iota((1, 2 * qt)) & (qt - 1))) // SEL_BLOCK
    sel = _nsa_select(imp, _row_iota(imp.shape), cur, 0)

    bias = jnp.where(sel, 0.0, NEG).astype(BF16)
    bias = jnp.concatenate([bias[:, :qt]] * NSA_GROUP + [bias[:, qt:]] * NSA_GROUP, axis=1)
    qaug_ref[...] = jnp.concatenate([qst, bias], axis=0)
    per_kb = kb // SEL_BLOCK

    def sel_block(c, causal, first):
        key = _row_iota((kb, LANES))
        onehot = jnp.where(_lane_iota((kb, LANES)) - c * per_kb == key // SEL_BLOCK, 1.0, 0.0).astype(BF16)
        kmat = jnp.concatenate([ks_ref[0, c], onehot], axis=1)
        cbias = per_head(jnp.where(c * kb + _row_iota((kb, qt)) <= t_q, 0.0, NEG)) if causal else None
        _attend_block(m_ref, l_ref, acc_ref, 0, kmat, qaug_ref[...], vs_ref[0, c], first, cbias)

    c_last = s0 // kb
    sel_block(0, True, True)

    def body(c, carry):
        sel_block(c, False, False)
        return carry

    lax.fori_loop(1, c_last, body, 0)

    @pl.when(c_last > 0)
    def _():
        sel_block(c_last, True, False)

    w0 = s0 // kb - (WINDOW // kb)
    n_wb = WINDOW // kb + 1
    for step, j in enumerate([n_wb - 1] + list(range(n_wb - 1))):
        blk = w0 + j
        base = jnp.where(blk < 0, FAR, blk * kb)
        dist = t_q - (base + _row_iota((kb, qt)))
        wbias = jnp.where(dist >= 0, jnp.where(dist < WINDOW, 0.0, NEG), NEG)
        idx = jnp.maximum(blk, 0)
        _attend_block(m_ref, l_ref, acc_ref, 1, kw_ref[0, idx], qaug_ref[:LANES, :], vw_ref[0, idx], step == 0,
                      per_head(wbias))

    g = g_ref[0]
    heads = []
    for h in range(NSA_HEADS):
        cs = head_cols(h)
        rows = slice((h // NSA_GROUP) * HEAD_DIM, (h // NSA_GROUP + 1) * HEAD_DIM)
        gate = lambda j: g[j * NSA_HEADS + h: j * NSA_HEADS + h + 1]
        o_s = acc_ref[0, rows, cs] / jnp.maximum(l_ref[0, :, cs], 1e-30)
        o_w = acc_ref[1, rows, cs] / jnp.maximum(l_ref[1, :, cs], 1e-30)
        heads.append(gate(0) * oc_ref[rows, cs] + gate(1) * o_s + gate(2) * o_w)
    o_ref[0] = jnp.concatenate(heads, axis=0).T.astype(BF16)


def _nsa_prompt(q_a, g_nsa, kc, vct, ks_rows, vs_blk, kw_rows, vw_blk, prm):
    b, w, t = q_a.shape
    qt = Q_TILE
    kb = ks_rows.shape[2]
    assert qt == LANES and kb % qt == 0 and WINDOW % kb == 0 and kb % SEL_BLOCK == 0 and t // SEL_BLOCK <= LANES
    cols = NSA_HEADS * qt
    per_q = lambda i, j: (i, 0, j)
    per_b = lambda i, j: (i, 0, 0)
    per_b4 = lambda i, j: (i, 0, 0, 0)
    return pl.pallas_call(
        _nsa_prompt_kernel, grid=(b, t // qt),
        in_specs=[pl.BlockSpec((1, w, qt), per_q), pl.BlockSpec((1, LANES, qt), per_q),
                  pl.BlockSpec((1, LANES, LANES), per_b), pl.BlockSpec((1, LANES, LANES), per_b),
                  pl.BlockSpec((1,) + ks_rows.shape[1:], per_b4), pl.BlockSpec((1,) + vs_blk.shape[1:], per_b4),
                  pl.BlockSpec((1,) + kw_rows.shape[1:], per_b4), pl.BlockSpec((1,) + vw_blk.shape[1:], per_b4),
                  pl.BlockSpec(prm['ovt'].shape, lambda i, j: (0, 0))],
        out_specs=pl.BlockSpec((1, qt, w), lambda i, j: (i, j, 0)),
        out_shape=jax.ShapeDtypeStruct((b, t, w), BF16),
        scratch_shapes=[pltpu.VMEM((2 * LANES, cols), BF16), pltpu.VMEM((LANES, cols), F32),
                        pltpu.VMEM((2, 1, cols), F32), pltpu.VMEM((2, 1, cols), F32),
                        pltpu.VMEM((2, LANES, cols), F32)],
        compiler_params=_cparams("arbitrary", "arbitrary"), name="nsa_prompt",
    )(q_a, g_nsa, kc, vct, ks_rows, vs_blk, kw_rows, vw_blk, prm['ovt'])


def _moba_prompt_kernel(q_ref, kt_ref, kb_ref, vb_ref, o_ref, kmean_ref, qaug_ref, m_ref, l_ref, acc_ref):
    qt = MOBA_BLOCK
    cur = pl.program_id(1)
    n_b = kb_ref.shape[1]
    n_pair = MOBA_HEADS // 2
    cols = 2 * qt

    @pl.when(cur == 0)
    def _():
        lane = _lane_iota((kt_ref.shape[1], LANES))
        km = jnp.zeros(lane.shape, F32)
        for n in range(n_b):
            km = jnp.where(lane == n, jnp.mean(kt_ref[0, :, n * qt:(n + 1) * qt], axis=-1, keepdims=True), km)
        kmean_ref[...] = km.T

    own_bias = jnp.where(_row_iota((qt, qt)) <= _lane_iota((qt, qt)), 0.0, NEG)
    own_bias = jnp.concatenate([own_bias, own_bias], axis=1)
    for j in range(n_pair):
        tl = slice(j * LANES, (j + 1) * LANES)
        qj = q_ref[0, tl, :]
        zero = jnp.zeros((HEAD_DIM, qt), qj.dtype)
        qst = jnp.concatenate([jnp.concatenate([qj[:HEAD_DIM], zero], axis=0),
                               jnp.concatenate([zero, qj[HEAD_DIM:]], axis=0)], axis=1)
        km_hi, km_lo = _split(kmean_ref[:, tl])
        s_g = _dot(km_hi, qst) + _dot(km_lo, qst)
        blk = _row_iota(s_g.shape)
        past = blk < cur
        sel = _select_top(jnp.where(past, s_g, -jnp.inf), MOBA_TOPK, blk.astype(F32), 0)
        bias = jnp.where(jnp.logical_and(sel, past), 0.0, NEG).astype(BF16)
        qs = qst * ATT_SCALE
        qaug_ref[j] = jnp.concatenate([qs, bias], axis=0)
        _attend_block(m_ref, l_ref, acc_ref, j, kb_ref[0, cur, :, tl], qs, vb_ref[0, cur, tl, :], True, own_bias)

    def body(n, carry):
        onehot = jnp.where(_lane_iota((qt, LANES)) == n, 1.0, 0.0).astype(BF16)
        for j in range(n_pair):
            tl = slice(j * LANES, (j + 1) * LANES)
            _attend_block(m_ref, l_ref, acc_ref, j, jnp.concatenate([kb_ref[0, n, :, tl], onehot], axis=1),
                          qaug_ref[j], vb_ref[0, n, tl, :], False)
        return carry

    lax.fori_loop(0, cur, body, 0)
    for j in range(n_pair):
        o = acc_ref[j] / jnp.maximum(l_ref[j], 1e-30)
        o = jnp.concatenate([o[:HEAD_DIM, :qt], o[HEAD_DIM:, qt:]], axis=0)
        o_ref[0, :, j * LANES:(j + 1) * LANES] = o.T.astype(BF16)


def _moba_prompt(q_b, kt, kb_rows, vb_blk):
    b, w, t = q_b.shape
    qt = MOBA_BLOCK
    assert kb_rows.shape[2] == qt and t // qt <= LANES
    per_b = lambda i, j: (i, 0, 0)
    per_b4 = lambda i, j: (i, 0, 0, 0)
    n_pair = MOBA_HEADS // 2
    return pl.pallas_call(
        _moba_prompt_kernel, grid=(b, t // qt),
        in_specs=[pl.BlockSpec((1, w, qt), lambda i, j: (i, 0, j)), pl.BlockSpec((1, w, t), per_b),
                  pl.BlockSpec((1,) + kb_rows.shape[1:], per_b4), pl.BlockSpec((1,) + vb_blk.shape[1:], per_b4)],
        out_specs=pl.BlockSpec((1, qt, w), lambda i, j: (i, j, 0)),
        out_shape=jax.ShapeDtypeStruct((b, t, w), BF16),
        scratch_shapes=[pltpu.VMEM((LANES, w), F32), pltpu.VMEM((n_pair, 2 * LANES, 2 * qt), BF16),
                        pltpu.VMEM((n_pair, 1, 2 * qt), F32), pltpu.VMEM((n_pair, 1, 2 * qt), F32),
                        pltpu.VMEM((n_pair, LANES, 2 * qt), F32)],
        compiler_params=_cparams("arbitrary", "arbitrary"), name="moba_prompt",
    )(q_b, kt, kb_rows, vb_blk)


def _merge_kernel(precise, x_ref, oa_ref, ob_ref, gm_ref, wa_ref, wb_ref, wo_ref, nf_ref, wr_hi, wr_lo, br_ref,
                  h_o, xn_o, lg_o):
    gm = gm_ref[...]
    u = (gm[:, :D_MODEL] * _mm(oa_ref[...], wa_ref[...], precise)
         + gm[:, D_MODEL:] * _mm(ob_ref[...], wb_ref[...], precise))
    h = x_ref[...] + _mm(u, wo_ref[...], precise)
    h_o[...] = h
    xn = h * lax.rsqrt(jnp.mean(h * h, axis=-1, keepdims=True) + EPS) * nf_ref[...]
    xn_o[...] = xn
    lg_o[...] = _dot3(xn, (wr_hi[...], wr_lo[...])) + br_ref[...]


def _merge(x2d, o_a, o_b, g_mrg, prm, tm, precise):
    n = x2d.shape[0]
    row = lambda i: (i, 0)
    const = lambda i: (0, 0)
    sfx = '_f32' if precise else ''
    ws = [prm[k] for k in ('w_up_a' + sfx, 'w_up_b' + sfx, 'w_out' + sfx, 'norm_ffn', 'wr_hi', 'wr_lo', 'b_router')]
    return pl.pallas_call(
        functools.partial(_merge_kernel, precise), grid=(n // tm,),
        in_specs=[pl.BlockSpec((tm, D_MODEL), row), pl.BlockSpec((tm, 512), row), pl.BlockSpec((tm, 512), row),
                  pl.BlockSpec((tm, 2 * D_MODEL), row)] + [pl.BlockSpec(w.shape, const) for w in ws],
        out_specs=[pl.BlockSpec((tm, D_MODEL), row), pl.BlockSpec((tm, D_MODEL), row), pl.BlockSpec((tm, LANES), row)],
        out_shape=[jax.ShapeDtypeStruct((n, D_MODEL), F32), jax.ShapeDtypeStruct((n, D_MODEL), F32),
                   jax.ShapeDtypeStruct((n, LANES), F32)],
        compiler_params=_cparams("parallel"), name="merge_router_sample" if precise else "merge_router",
    )(x2d, o_a, o_b, g_mrg, *ws)


def _moe_kernel(be_ref, bv_ref, x_ref, wgu_ref, bgu_ref, wd_ref, bd_ref, y_ref, wgu16_ref, wd16_ref):
    i = pl.program_id(0)

    @pl.when(jnp.logical_or(i == 0, be_ref[i] != be_ref[jnp.maximum(i - 1, 0)]))
    def _():
        wgu16_ref[...] = wgu_ref[0].astype(BF16)
        wd16_ref[...] = wd_ref[0].astype(BF16)

    @pl.when(bv_ref[i] > 0)
    def _():
        h = _dot(x_ref[...].astype(BF16), wgu16_ref[...]) + bgu_ref[0]
        gate = jnp.minimum(h[:, :D_FF], SWIGLU_LIMIT)
        up = jnp.clip(h[:, D_FF:], -SWIGLU_LIMIT, SWIGLU_LIMIT)
        act = (up + 1.0) * gate * jax.nn.sigmoid(SWIGLU_ALPHA * gate)
        y_ref[...] = _dot(act.astype(BF16), wd16_ref[...]) + bd_ref[0]

    @pl.when(bv_ref[i] == 0)
    def _():
        y_ref[...] = jnp.zeros(y_ref.shape, F32)


def _moe_blocks(block_e, block_valid, x_sorted, w_gu, b_gu, w_down, b_down):
    cap = x_sorted.shape[0]
    tm = MOE_TILE
    row = lambda i, be, bv: (i, 0)
    exp = lambda i, be, bv: (be[i], 0, 0)
    grid_spec = pltpu.PrefetchScalarGridSpec(
        num_scalar_prefetch=2, grid=(cap // tm,),
        in_specs=[pl.BlockSpec((tm, D_MODEL), row),
                  pl.BlockSpec((1, D_MODEL, 2 * D_FF), exp), pl.BlockSpec((1, 1, 2 * D_FF), exp),
                  pl.BlockSpec((1, D_FF, D_MODEL), exp), pl.BlockSpec((1, 1, D_MODEL), exp)],
        out_specs=pl.BlockSpec((tm, D_MODEL), row),
        scratch_shapes=[pltpu.VMEM((D_MODEL, 2 * D_FF), BF16), pltpu.VMEM((D_FF, D_MODEL), BF16)])
    return pl.pallas_call(
        _moe_kernel, grid_spec=grid_spec, out_shape=jax.ShapeDtypeStruct((cap, D_MODEL), F32),
        compiler_params=_cparams("arbitrary"), name="moe_experts",
    )(block_e, block_valid, x_sorted, w_gu, b_gu, w_down, b_down)


def _route(logits):
    n = logits.shape[0]
    top_v, top_e = lax.top_k(logits, TOP_K)
    gates = jax.nn.softmax(top_v, axis=-1)
    onehot = jnp.sum((top_e[:, :, None] == jnp.arange(N_EXPERTS)[None, None, :]).astype(jnp.int32), axis=1)
    csum = jnp.cumsum(onehot, axis=0)
    counts = csum[-1]
    padded = (counts + MOE_TILE - 1) // MOE_TILE * MOE_TILE
    pend = jnp.cumsum(padded)
    rank = jnp.take_along_axis(csum - onehot, top_e, axis=1)
    slot = (pend - padded)[top_e] + rank
    n_blocks = -(-(n * TOP_K) // MOE_TILE) + N_EXPERTS
    cap = n_blocks * MOE_TILE
    starts = jnp.arange(n_blocks, dtype=jnp.int32) * MOE_TILE
    block_e = jnp.minimum(jnp.sum((pend[None, :] <= starts[:, None]).astype(jnp.int32), axis=1), N_EXPERTS - 1)
    block_valid = (starts < pend[-1]).astype(jnp.int32)
    assert cap * n < 2 ** 31
    tok = jnp.broadcast_to(jnp.arange(n, dtype=jnp.int32)[:, None], (n, TOP_K))
    s_tok = lax.sort((slot * n + tok).reshape(-1)) % n
    per_slot = lambda v: jnp.broadcast_to(v[block_e][:, None], (n_blocks, MOE_TILE)).reshape(cap)
    rank_in_e = jnp.arange(cap, dtype=jnp.int32) - per_slot(pend - padded)
    used = rank_in_e < per_slot(counts)
    src = jnp.clip(per_slot(jnp.cumsum(counts) - counts) + rank_in_e, 0, n * TOP_K - 1)
    slot_tok = jnp.where(used, s_tok[src], 0)
    return slot, gates, slot_tok, block_e, block_valid


def _column(ref, b):
    x = ref[...]
    return jnp.sum(jnp.where(_lane_iota(x.shape) == b, x, 0.0), axis=-1, keepdims=True)


def _fold_lanes(x):
    return sum(x[..., i * LANES:(i + 1) * LANES] for i in range(x.shape[-1] // LANES))


def _gqa_scores(q3, kt):
    g = NSA_GROUP
    return jnp.concatenate([jnp.sum(q3[i * g:(i + 1) * g] * kt[i * HEAD_DIM:(i + 1) * HEAD_DIM][None], axis=1)
                            for i in range(NSA_KV_HEADS)], axis=0)


def _gqa_weighted(p, vt):
    g = NSA_GROUP
    return jnp.concatenate([p[i * g:(i + 1) * g][:, None, :] * vt[i * HEAD_DIM:(i + 1) * HEAD_DIM][None]
                            for i in range(NSA_KV_HEADS)], axis=0)


_CWS_NAMES = ('pka', 'pkb', 'k1a_hi', 'k1a_lo', 'k1b_hi', 'k1b_lo', 'k2t',
              'pva', 'pvb', 'v1a_hi', 'v1a_lo', 'v1b_hi', 'v1b_lo', 'v2t', 'gkc_col')


def _nsa_sample_kernel(pt_ref, qt_ref, g_ref, ksn_ref, vsn_ref, kwn_ref, vwn_ref, wk_ref, wv_ref, *rest):
    n_seq = g_ref.shape[0]
    n_pg = (len(rest) - len(_CWS_NAMES) - 4) // (4 * n_seq)
    n_pages = 4 * n_seq * n_pg
    cw = dict(zip(_CWS_NAMES, rest[n_pages:n_pages + len(_CWS_NAMES)]))
    ov_ref, e_ref, o_ref, rows_ref = rest[n_pages + len(_CWS_NAMES):]
    step = pl.program_id(0)
    fixed = (cw, ov_ref, e_ref, rows_ref, qt_ref, g_ref, ksn_ref, vsn_ref, kwn_ref, vwn_ref, wk_ref, wv_ref)
    outs = [_nsa_sample_one(step * n_seq + j, j, rest[4 * j * n_pg:4 * (j + 1) * n_pg], *fixed) for j in range(n_seq)]

    @pl.when(step == 0)
    def _():
        o_ref[...] = jnp.zeros(o_ref.shape, F32)

    for j, o in enumerate(outs):
        o_ref[...] = jnp.where(_lane_iota(o_ref.shape) == step * n_seq + j, o, o_ref[...])


def _nsa_sample_one(b, j, pages, cw, ov_ref, e_ref, rows_ref, qt_ref, g_ref, ksn_ref, vsn_ref, kwn_ref, vwn_ref,
                    wk_ref, wv_ref):
    n_pg = len(pages) // 4
    ck_pages, cv_pages, sk_pages, sv_pages = (pages[k * n_pg:(k + 1) * n_pg] for k in range(4))
    past = n_pg * PAGE_SIZE

    def compress_t(pages, p):
        for i, pg in enumerate(pages):
            rows_ref[j, i * PAGE_SIZE:(i + 1) * PAGE_SIZE, :] = pg[0].T
        xs = [rows_ref[j, pl.ds(l, past // CMP_STRIDE, stride=CMP_STRIDE), :] for l in range(CMP_STRIDE)]
        h = _compress_hidden(xs, cw['p%sa' % p][...], cw['p%sb' % p][...],
                             (cw['%s1a_hi' % p][...], cw['%s1a_lo' % p][...]),
                             (cw['%s1b_hi' % p][...], cw['%s1b_lo' % p][...]), True)
        w2t = cw['%s2t' % p][...]
        return jnp.concatenate([_dot3(w2t, h[:LANES], nt=True), _dot3(w2t, h[LANES:], nt=True)], axis=0)

    kct = compress_t(ck_pages, 'k').reshape(NSA_KV_HEADS, HEAD_DIM, LANES)
    ss = jnp.sum(kct * kct, axis=1, keepdims=True)
    kct = (kct * lax.rsqrt(ss * (1.0 / HEAD_DIM) + EPS) * cw['gkc_col'][...][None]).reshape(LANES, LANES)
    vct = compress_t(cv_pages, 'v')

    q3 = (_column(qt_ref, b) * ATT_SCALE).reshape(NSA_HEADS, HEAD_DIM, 1)

    n_c = (past + 1 - CMP_BLOCK) // CMP_STRIDE + 1
    s_c = _gqa_scores(q3, kct)
    p_c = _masked_softmax(s_c, _lane_iota(s_c.shape) < n_c)
    o_c = jnp.sum(_gqa_weighted(p_c, vct), axis=-1, keepdims=True)
    row = _row_iota((NSA_HEADS, LANES))
    i0 = jnp.sum(p_c[:NSA_GROUP], axis=0, keepdims=True)
    i1 = jnp.sum(p_c[NSA_GROUP:], axis=0, keepdims=True)
    imp = _split_dot(jnp.where(row < NSA_GROUP, i0, i1), ov_ref[...])
    blk = _lane_iota((1, LANES))
    cur = past // SEL_BLOCK
    forced = (blk == 0) | (blk == cur) | (blk == cur - 1)
    valid = blk <= cur

    def pick(imp_row):
        score = jnp.where(forced, POS_BIG, jnp.where(valid, imp_row, -jnp.inf))
        return jnp.where(jnp.logical_and(_rank_select(score, N_SEL), valid), 1.0, 0.0)

    sel = jnp.where(row < NSA_GROUP, pick(imp[0:1]), pick(imp[NSA_GROUP:NSA_GROUP + 1]))
    picked = _dot(sel.astype(BF16), e_ref[...]) > 0.5

    def attend_with_new(s, valid, kn_ref, vn_ref, weighted):
        kn = _column(kn_ref, b)
        vn = _column(vn_ref, b)
        s = jnp.where(valid, s, NEG)
        s_n = _gqa_scores(q3, kn)
        m = jnp.maximum(jnp.max(s, axis=-1, keepdims=True), s_n)
        p = jnp.where(valid, jnp.exp(s - m), 0.0)
        p_n = jnp.exp(s_n - m)
        num = jnp.sum(weighted(p), axis=-1, keepdims=True) + _gqa_weighted(p_n, vn)
        den = jnp.maximum(jnp.sum(p, axis=-1, keepdims=True) + p_n, 1e-30)
        return num / den[:, :, None]

    s_s = jnp.concatenate([_gqa_scores(q3, pg[0]) for pg in sk_pages], axis=1)
    o_s = attend_with_new(
        s_s, picked, ksn_ref, vsn_ref,
        lambda p: sum(_gqa_weighted(p[:, i * PAGE_SIZE:(i + 1) * PAGE_SIZE], pg[0]) for i, pg in enumerate(sv_pages)))

    wb = wk_ref.shape[2]
    s_w = _gqa_scores(q3, wk_ref[j])
    inside = (past - wb + _lane_iota(s_w.shape)) > (past - WINDOW)
    o_w = attend_with_new(s_w, inside, kwn_ref, vwn_ref, lambda p: _fold_lanes(_gqa_weighted(p, wv_ref[j])))

    g = jnp.broadcast_to(g_ref[j], (NSA_HEADS, LANES))
    lane = _lane_iota(g.shape)
    gate = lambda br: jnp.sum(jnp.where(lane == br * NSA_HEADS + row, g, 0.0), axis=-1, keepdims=True)[:, :, None]
    return (gate(0) * o_c + gate(1) * o_s + gate(2) * o_w).reshape(NSA_HEADS * HEAD_DIM, 1)


def _nsa_sample(page_table, qt, g_nsa, ksn, vsn, kwn, vwn, win_kt, win_vt, cache_ck, cache_cv, cache_skt, cache_svt,
                prm):
    db, n_pg = page_table.shape
    wb = win_kt.shape[2]
    n_seq = NSA_SAMPLE_SEQS
    assert db % n_seq == 0
    per_b = lambda i, pt: (i, 0, 0)
    const2 = lambda i, pt: (0, 0)

    def page_spec(j, p):
        return pl.BlockSpec((1, PAGE_SIZE, LANES), lambda i, pt, j=j, p=p: (pt[i * n_seq + j, p], 0, 0))

    cws = [prm['s_' + k] for k in _CWS_NAMES]
    full = lambda a: pl.BlockSpec(a.shape, const2)
    in_specs = ([full(qt), pl.BlockSpec((n_seq, 1, LANES), per_b), full(ksn), full(vsn), full(kwn), full(vwn),
                 pl.BlockSpec((n_seq, LANES, wb), per_b), pl.BlockSpec((n_seq, LANES, wb), per_b)]
                + [page_spec(j, p) for j in range(n_seq) for _ in range(4) for p in range(n_pg)]
                + [full(w) for w in cws] + [full(prm['ov']), full(prm['e_key'])])
    grid_spec = pltpu.PrefetchScalarGridSpec(num_scalar_prefetch=1, grid=(db // n_seq,), in_specs=in_specs,
                                             out_specs=pl.BlockSpec((NSA_HEADS * HEAD_DIM, db), const2),
                                             scratch_shapes=[pltpu.VMEM((n_seq, n_pg * PAGE_SIZE, LANES), F32)])
    pages = ([cache_ck] * n_pg + [cache_cv] * n_pg + [cache_skt] * n_pg + [cache_svt] * n_pg) * n_seq
    return pl.pallas_call(
        _nsa_sample_kernel, grid_spec=grid_spec, out_shape=jax.ShapeDtypeStruct((NSA_HEADS * HEAD_DIM, db), F32),
        compiler_params=_cparams("arbitrary"), name="nsa_sample",
    )(page_table, qt, g_nsa, ksn, vsn, kwn, vwn, win_kt, win_vt, *pages, *cws, prm['ov'], prm['e_key'])


def _moba_sample_kernel(pt_ref, qt_ref, kn_ref, vn_ref, *rest):
    n_pg = (len(rest) - 1) // 2
    k_pages = rest[:n_pg]
    v_pages = rest[n_pg:2 * n_pg]
    o_ref = rest[2 * n_pg]
    b = pl.program_id(0)
    pages_per_blk = MOBA_BLOCK // PAGE_SIZE
    n_b = n_pg // pages_per_blk
    heads3 = lambda x: x.reshape(MOBA_HEADS, HEAD_DIM, x.shape[-1])

    q3 = heads3(_column(qt_ref, b) * ATT_SCALE)
    kn3 = heads3(_column(kn_ref, b))
    vn3 = heads3(_column(vn_ref, b))
    s_pages = [jnp.sum(heads3(pg[0]) * q3, axis=1) for pg in k_pages]
    s_new = jnp.sum(q3 * kn3, axis=1)

    lane = _lane_iota((MOBA_HEADS, LANES))
    s_g = jnp.full((MOBA_HEADS, LANES), -jnp.inf, F32)
    for n in range(n_b):
        tot = sum(s_pages[n * pages_per_blk + i] for i in range(pages_per_blk))
        s_g = jnp.where(lane == n, jnp.sum(tot, axis=-1, keepdims=True), s_g)
    sel = _select_top(s_g, min(MOBA_TOPK, n_b), lane.astype(F32), -1)
    sel_blk = [jnp.sum(jnp.where(jnp.logical_and(sel, lane == n), 1.0, 0.0), axis=-1, keepdims=True) > 0.5
               for n in range(n_b)]

    m = s_new
    for i in range(n_pg):
        m = jnp.maximum(m, jnp.max(jnp.where(sel_blk[i // pages_per_blk], s_pages[i], NEG), axis=-1, keepdims=True))
    p_new = jnp.exp(s_new - m)
    l = p_new
    acc = jnp.zeros((MOBA_HEADS, HEAD_DIM, PAGE_SIZE), F32)
    for i in range(n_pg):
        p = jnp.where(sel_blk[i // pages_per_blk], jnp.exp(s_pages[i] - m), 0.0)
        l = l + jnp.sum(p, axis=-1, keepdims=True)
        acc = acc + heads3(v_pages[i][0]) * p[:, None, :]
    o = (jnp.sum(acc, axis=-1, keepdims=True) + p_new[:, :, None] * vn3) / jnp.maximum(l, 1e-30)[:, :, None]

    @pl.when(b == 0)
    def _():
        o_ref[...] = jnp.zeros(o_ref.shape, F32)

    o_ref[...] = jnp.where(_lane_iota(o_ref.shape) == b, o.reshape(MOBA_HEADS * HEAD_DIM, 1), o_ref[...])


def _moba_sample(page_table, qt, knt, vnt, cache_kt, cache_vt):
    db, n_pg = page_table.shape
    w = qt.shape[0]
    assert (n_pg * PAGE_SIZE) % MOBA_BLOCK == 0
    const2 = lambda i, pt: (0, 0)

    def page_spec(p):
        return pl.BlockSpec((1, w, PAGE_SIZE), lambda i, pt, p=p: (pt[i, p], 0, 0))

    in_specs = [pl.BlockSpec((w, db), const2)] * 3 + [page_spec(p) for p in range(n_pg)] * 2
    grid_spec = pltpu.PrefetchScalarGridSpec(num_scalar_prefetch=1, grid=(db,), in_specs=in_specs,
                                             out_specs=pl.BlockSpec((w, db), const2))
    return pl.pallas_call(
        _moba_sample_kernel, grid_spec=grid_spec, out_shape=jax.ShapeDtypeStruct((w, db), F32),
        compiler_params=_cparams("arbitrary"), name="moba_sample",
    )(page_table, qt, knt, vnt, *([cache_kt] * n_pg), *([cache_vt] * n_pg))


def _split_weight(w):
    c = w * (2.0 ** 16 + 1.0)
    hi = c - (c - w)
    return hi.astype(BF16), (w - hi).astype(BF16)


def _prepare(norm_attn, w_in, g_qa, g_ka, g_kc, cmp_pos_k, cmp_w1_k, cmp_w2_k, cmp_pos_v, cmp_w1_v, cmp_w2_v,
             g_qb, g_kb, w_up_a, w_up_b, w_out, norm_ffn, w_router, b_router):
    qa = NSA_HEADS * HEAD_DIM
    kva = NSA_KV_HEADS * HEAD_DIM
    qb = MOBA_HEADS * HEAD_DIM
    widths = [qa, kva, kva, kva, kva, kva, kva, 3 * NSA_HEADS, qb, qb, qb, 2 * D_MODEL]
    cuts = np.cumsum([0] + widths)
    part = lambda i: w_in[:, cuts[i]:cuts[i + 1]]
    gcols = np.array([h * 3 + j for j in range(3) for h in range(NSA_HEADS)])
    wgnt = jnp.zeros((LANES, D_MODEL), F32).at[:3 * NSA_HEADS].set(part(7)[:, gcols].T)
    two = lambda g: jnp.concatenate([g, g]).reshape(1, LANES)
    lane = np.arange(LANES)
    c_idx = np.arange(LANES)[:, None] * CMP_STRIDE
    s_idx = np.arange(LANES)[None, :] * SEL_BLOCK
    ov = ((c_idx < s_idx + SEL_BLOCK) & (c_idx + CMP_BLOCK > s_idx) & (np.arange(LANES)[:, None] < LANES - 1))
    half_w = CMP_STRIDE * HEAD_DIM

    def w2wide(w2):
        z = jnp.zeros_like(w2)
        return jnp.concatenate([w2, z], axis=1).astype(BF16), jnp.concatenate([z, w2], axis=1).astype(BF16)

    k2g0, k2g1 = w2wide(cmp_w2_k)
    wr_hi, wr_lo = _split_weight(jnp.zeros((D_MODEL, LANES), F32).at[:, :N_EXPERTS].set(w_router))
    prm = {
        'norm_attn': norm_attn.reshape(1, D_MODEL),
        'wqat_f32': part(0).T, 'wkvat_f32': w_in[:, cuts[1]:cuts[7]].T, 'wgnt_f32': wgnt,
        'wqbt_f32': part(8).T, 'wkbt_f32': part(9).T, 'wvbt_f32': part(10).T, 'wgm_f32': part(11),
        'g_col': jnp.stack([g_qa, g_qb, g_ka[0], g_ka[1], g_ka[2], g_kb], axis=1),
        'gkc': two(g_kc),
        'bd': jnp.asarray((lane[:, None] // HEAD_DIM) == (lane[None, :] // HEAD_DIM), BF16),
        'pka': cmp_pos_k[:CMP_STRIDE].reshape(1, half_w), 'pkb': cmp_pos_k[CMP_STRIDE:].reshape(1, half_w),
        'pva': cmp_pos_v[:CMP_STRIDE].reshape(1, half_w), 'pvb': cmp_pos_v[CMP_STRIDE:].reshape(1, half_w),
        'k1a': cmp_w1_k[:half_w].astype(BF16), 'k1b': cmp_w1_k[half_w:].astype(BF16),
        'v1a': cmp_w1_v[:half_w].astype(BF16), 'v1b': cmp_w1_v[half_w:].astype(BF16),
        'k2g0': k2g0, 'k2g1': k2g1, 'v2t': cmp_w2_v.T.astype(BF16),
        'ov': jnp.asarray(ov, BF16), 'ovt': jnp.asarray(ov.T, BF16),
        'w_up_a_f32': w_up_a, 'w_up_b_f32': w_up_b, 'w_out_f32': w_out,
        'norm_ffn': norm_ffn.reshape(1, D_MODEL),
        'wr_hi': wr_hi, 'wr_lo': wr_lo,
        'b_router': jnp.zeros((1, LANES), F32).at[0, :N_EXPERTS].set(b_router),
    }
    for k in _INPROJ_W + ('w_up_a', 'w_up_b', 'w_out'):
        prm[k] = prm[k + '_f32'].astype(BF16)
    for p, w1, w2 in (('k', cmp_w1_k, cmp_w2_k), ('v', cmp_w1_v, cmp_w2_v)):
        for half, w in (('a', w1[:half_w]), ('b', w1[half_w:])):
            prm['s_%s1%s_hi' % (p, half)], prm['s_%s1%s_lo' % (p, half)] = _split_weight(w)
        prm['s_%s2t' % p] = w2.T
        prm['s_p%sa' % p] = prm['p%sa' % p]
        prm['s_p%sb' % p] = prm['p%sb' % p]
    prm['s_gkc_col'] = g_kc.reshape(HEAD_DIM, 1)
    return prm


def _rope_tables(pos):
    half = HEAD_DIM // 2
    inv = ROPE_THETA ** (-np.arange(half, dtype=np.float64) / half)
    ang = inv[:, None] * np.asarray(pos, np.float64)[None, :]
    return jnp.asarray(np.cos(ang), F32), jnp.asarray(np.sin(ang), F32)


def _expand_matrix(n_keys, block):
    e = (np.arange(n_keys)[None, :] // block) == np.arange(LANES)[:, None]
    return jnp.asarray(e, BF16)


def kernel(x_prompt, x_sample, cache_nsa_cmp_k, cache_nsa_cmp_v, cache_nsa_sel_k, cache_nsa_sel_v, cache_nsa_win_k, cache_nsa_win_v, cache_moba_k, cache_moba_v, page_table, norm_attn, w_in, g_qa, g_ka, g_kc, cmp_pos_k, cmp_w1_k, cmp_w2_k, cmp_pos_v, cmp_w1_v, cmp_w2_v, g_qb, g_kb, w_up_a, w_up_b, w_out, norm_ffn, w_router, b_router, w_gu, b_gu, w_down, b_down):
    B, T, _ = x_prompt.shape
    DB, DS, _ = x_sample.shape
    assert DS == 1
    n_pg = page_table.shape[1]
    past = n_pg * PAGE_SIZE
    wb = cache_nsa_win_k.shape[1]

    prm = _prepare(norm_attn, w_in, g_qa, g_ka, g_kc, cmp_pos_k, cmp_w1_k, cmp_w2_k, cmp_pos_v, cmp_w1_v, cmp_w2_v,
                   g_qb, g_kb, w_up_a, w_up_b, w_out, norm_ffn, w_router, b_router)
    prm['e_key'] = _expand_matrix(past, SEL_BLOCK)

    P = _in_proj(x_prompt.reshape(B * T, D_MODEL), B, _rope_tables(np.arange(T)), prm, KEY_BLOCK, False)
    r3 = lambda a: a.reshape(B, T, a.shape[-1])
    kc_cmp, vct_cmp = _compress_prompt(r3(P['kc_row']), r3(P['vc_row']), prm)
    oa_p = _nsa_prompt(P['qa'], P['gn'], kc_cmp, vct_cmp, P['ks_rows'], P['vs_blk'], P['kw_rows'], P['vw_blk'], prm)
    ob_p = _moba_prompt(P['qb'], P['kb'], P['kb_rows'], P['vb_blk'])
    h_p, xn_p, lg_p = _merge(x_prompt.reshape(B * T, D_MODEL), oa_p.reshape(B * T, 512), ob_p.reshape(B * T, 512),
                             P['gm'], prm, MERGE_TILE, False)

    S = _in_proj(x_sample.reshape(DB, D_MODEL), 1, _rope_tables(np.full((DB,), past)), prm, DB, True)
    cols2 = lambda c: jnp.transpose(c, (0, 2, 3, 1)).reshape(c.shape[0], c.shape[2] * c.shape[3], c.shape[1])
    oa_t = _nsa_sample(page_table, S['qa'][0], S['gn'][0].T.reshape(DB, 1, LANES),
                       S['ks'][0], S['vs'][0], S['kw'][0], S['vw'][0],
                       cols2(cache_nsa_win_k), cols2(cache_nsa_win_v),
                       cols2(cache_nsa_cmp_k), cols2(cache_nsa_cmp_v), cols2(cache_nsa_sel_k), cols2(cache_nsa_sel_v),
                       prm)
    ob_t = _moba_sample(page_table, S['qb'][0], S['kb'][0], S['vb'][0], cols2(cache_moba_k), cols2(cache_moba_v))
    h_s, xn_s, lg_s = _merge(x_sample.reshape(DB, D_MODEL), oa_t.T, ob_t.T, S['gm'], prm, DB, True)

    xn_all = jnp.concatenate([xn_p, xn_s], axis=0)
    logits = jnp.concatenate([lg_p, lg_s], axis=0)[:, :N_EXPERTS]
    slot, gates, slot_tok, block_e, block_valid = _route(logits)
    y_slot = _moe_blocks(block_e, block_valid, xn_all[slot_tok],
                         w_gu, b_gu.reshape(N_EXPERTS, 1, 2 * D_FF), w_down, b_down.reshape(N_EXPERTS, 1, D_MODEL))
    n_tok = B * T + DB
    picked = y_slot[slot.T.reshape(-1)].reshape(TOP_K, n_tok, D_MODEL)
    mix = lambda lo, hi: sum(gates[lo:hi, k:k + 1] * picked[k, lo:hi] for k in range(TOP_K))
    y_prompt = (h_p + mix(0, B * T)).reshape(B, T, D_MODEL)
    y_sample = (h_s + mix(B * T, n_tok)).reshape(DB, DS, D_MODEL)

    def kv4(a):
        n, w, t = a.shape
        return jnp.transpose(a.reshape(n, w // HEAD_DIM, HEAD_DIM, t), (0, 3, 1, 2))

    new4 = lambda a: kv4(a).reshape(DB, 1, a.shape[1] // HEAD_DIM, HEAD_DIM)
    wp = min(WINDOW, T)
    ws = min(WINDOW, wb + DS)
    kw_all = jnp.concatenate([cache_nsa_win_k, new4(S['kw'])], axis=1)[:, -ws:]
    vw_all = jnp.concatenate([cache_nsa_win_v, new4(S['vw'])], axis=1)[:, -ws:]
    return (y_prompt, y_sample,
            kv4(P['kc']), kv4(P['vc']), kv4(P['ks']), kv4(P['vs']),
            kv4(P['kw'][:, :, T - wp:]), kv4(P['vw'][:, :, T - wp:]), kv4(P['kb']), kv4(P['vb']),
            new4(S['kc']), new4(S['vc']), new4(S['ks']), new4(S['vs']), kw_all, vw_all,
            new4(S['kb']), new4(S['vb']))
```

```python
import functools

import numpy as np
import jax
import jax.numpy as jnp
from jax import lax
from jax.experimental import pallas as pl
from jax.experimental.pallas import tpu as pltpu

F32 = jnp.float32
BF16 = jnp.bfloat16

D_MODEL = 1024
PAGE_SIZE = 128
HEAD_DIM = 64
NSA_HEADS = 8
NSA_KV_HEADS = 2
NSA_GROUP = NSA_HEADS // NSA_KV_HEADS
CMP_BLOCK = 32
CMP_STRIDE = 16
CMP_HIDDEN = 256
SEL_BLOCK = 64
N_SEL = 8
WINDOW = 512
MOBA_HEADS = 8
MOBA_BLOCK = 256
MOBA_TOPK = 3
N_EXPERTS = 32
TOP_K = 4
D_FF = D_MODEL
SWIGLU_LIMIT = 7.0
SWIGLU_ALPHA = 1.702
ROPE_THETA = 10000.0
EPS = 1e-6
NEG = -1e30
POS_BIG = 1e30

LANES = 128
HALF = HEAD_DIM
ATT_SCALE = HEAD_DIM ** -0.5
Q_TILE = 128
KEY_BLOCK = MOBA_BLOCK
MOE_TILE = 512
MERGE_TILE = 512
NSA_SAMPLE_SEQS = 1
VMEM_LIMIT = 56 * 1024 * 1024
FAR = -(1 << 20)


def _cparams(*sem):
    return pltpu.CompilerParams(dimension_semantics=sem, vmem_limit_bytes=VMEM_LIMIT)


def _lane_iota(shape):
    return lax.broadcasted_iota(jnp.int32, shape, len(shape) - 1)


def _row_iota(shape):
    return lax.broadcasted_iota(jnp.int32, shape, 0)


def _dot(a, b):
    return jnp.dot(a, b, preferred_element_type=F32)


def _dot_nt(a, b):
    return lax.dot_general(a, b, (((1,), (1,)), ((), ())), preferred_element_type=F32)


def _split(a):
    hi = a.astype(BF16)
    lo = (a - hi.astype(F32)).astype(BF16)
    return hi, lo


def _split_dot(a, b):
    hi, lo = _split(a)
    return _dot(hi, b) + _dot(lo, b)


def _dot_split(a, b):
    hi, lo = _split(b)
    return _dot(a, hi) + _dot(a, lo)


def _dot3(a, b, nt=False):
    mm = _dot_nt if nt else _dot
    a_hi, a_lo = _split(a)
    b_hi, b_lo = b if isinstance(b, tuple) else _split(b)
    return mm(a_hi, b_hi) + (mm(a_hi, b_lo) + mm(a_lo, b_hi))


def _mm(a, w, precise):
    return _dot3(a, w) if precise else _dot(a.astype(BF16), w)


def _masked_softmax(s, mask):
    s = jnp.where(mask, s, NEG)
    m = jnp.max(s, axis=-1, keepdims=True)
    e = jnp.where(mask, jnp.exp(s - m), 0.0)
    return e / jnp.maximum(jnp.sum(e, axis=-1, keepdims=True), 1e-30)


def _head_norm(y, gain, bd):
    ss = _split_dot(y * y, bd)
    return y * lax.rsqrt(ss * (1.0 / HEAD_DIM) + EPS) * gain


def _select_top(score, n_pick, idx_f, axis):
    sel = jnp.zeros(score.shape, jnp.bool_)
    for _ in range(n_pick):
        m = jnp.max(score, axis=axis, keepdims=True)
        first = jnp.min(jnp.where(score == m, idx_f, 1e9), axis=axis, keepdims=True)
        hit = idx_f == first
        sel = jnp.logical_or(sel, hit)
        score = jnp.where(hit, -jnp.inf, score)
    return sel


def _rank_select(score_row, n_pick):
    a = jnp.broadcast_to(score_row, (LANES, LANES))
    b = a.T
    ahead = jnp.where(b > a, 1.0, jnp.where(b == a, jnp.where(_row_iota(a.shape) < _lane_iota(a.shape), 1.0, 0.0),
                                            0.0))
    return jnp.sum(ahead, axis=0, keepdims=True) < n_pick


def _nsa_select(imp, blk, cur, axis):
    forced = (blk == 0) | (blk == cur) | (blk == cur - 1)
    valid = blk <= cur
    score = jnp.where(forced, POS_BIG, jnp.where(valid, imp, -jnp.inf))
    sel = _select_top(score, N_SEL, blk.astype(F32), axis)
    return jnp.logical_and(sel, valid)


def _attend_block(m_ref, l_ref, acc_ref, i, kmat, qmat, vt, first, bias=None):
    s = _dot(kmat, qmat)
    if bias is not None:
        s = s + bias
    m_cur = jnp.max(s, axis=0, keepdims=True)
    if first:
        p = jnp.exp(s - m_cur)
        m_ref[i] = m_cur
        l_ref[i] = jnp.sum(p, axis=0, keepdims=True)
        acc_ref[i] = _dot(vt, p.astype(BF16))
    else:
        m_prev = m_ref[i]
        m_new = jnp.maximum(m_prev, m_cur)
        alpha = jnp.exp(m_prev - m_new)
        p = jnp.exp(s - m_new)
        m_ref[i] = m_new
        l_ref[i] = alpha * l_ref[i] + jnp.sum(p, axis=0, keepdims=True)
        acc_ref[i] = alpha * acc_ref[i] + _dot(vt, p.astype(BF16))


def _inproj_kernel(precise, x_ref, cos_ref, sin_ref, gn_ref, wqat, wkvat, wgnt, wqbt, wkbt, wvbt, wgm, g_col, *outs):
    (qa_o, kcr_o, vcr_o, kc_o, vc_o, ks_o, vs_o, kw_o, vw_o, ksr_o, vsb_o, kwr_o, vwb_o,
     gn_o, qb_o, kb_o, vb_o, kbr_o, vbb_o, gm_o) = outs
    x = x_ref[...]
    xn = x * lax.rsqrt(jnp.mean(x * x, axis=-1, keepdims=True) + EPS) * gn_ref[...]
    xa = xn if precise else xn.astype(BF16)
    mm_t = (lambda wt: _dot3(wt, xn, nt=True)) if precise else (lambda wt: _dot_nt(wt, xa))
    half = HEAD_DIM // 2
    cos = cos_ref[...][None]
    sin = sin_ref[...][None]

    def tile(j):
        return slice(j * LANES, (j + 1) * LANES)

    def qk_t(y, gain_col):
        y = y.reshape(2, HEAD_DIM, y.shape[-1])
        y = y * lax.rsqrt(jnp.mean(y * y, axis=1, keepdims=True) + EPS) * gain_col[None]
        x1, x2 = y[:, :half], y[:, half:]
        return jnp.concatenate([x1 * cos - x2 * sin, x2 * cos + x1 * sin], axis=1).reshape(LANES, -1)

    y_qa = mm_t(wqat[...])
    y_qb = mm_t(wqbt[...])
    for j in range(4):
        qa_o[0, tile(j), :] = qk_t(y_qa[tile(j)], g_col[:, 0:1]).astype(qa_o.dtype)
        qb_o[0, tile(j), :] = qk_t(y_qb[tile(j)], g_col[:, 1:2]).astype(qb_o.dtype)
    y_kva = mm_t(wkvat[...])
    nsa_kv = ((kc_o, vc_o, kcr_o, vcr_o), (ks_o, vs_o, ksr_o, vsb_o), (kw_o, vw_o, kwr_o, vwb_o))
    for i, (k_o, v_o, k2_o, v2_o) in enumerate(nsa_kv):
        k = qk_t(y_kva[tile(2 * i)], g_col[:, 2 + i:3 + i])
        v = y_kva[tile(2 * i + 1)]
        k_o[0] = k
        v_o[0] = v
        if i == 0:
            k2_o[...] = k.T
            v2_o[...] = v.T
        else:
            k2_o[0, 0] = k.T.astype(BF16)
            v2_o[0, 0] = v.astype(BF16)
    gn_o[0] = jax.nn.sigmoid(mm_t(wgnt[...]))
    y_kb = mm_t(wkbt[...])
    y_vb = mm_t(wvbt[...])
    vb_o[0] = y_vb
    vbb_o[0, 0] = y_vb.astype(BF16)
    for j in range(4):
        k = qk_t(y_kb[tile(j)], g_col[:, 5:6])
        kb_o[0, tile(j), :] = k
        kbr_o[0, 0, :, tile(j)] = k.T.astype(BF16)
    gm_o[...] = jax.nn.sigmoid(_mm(xa, wgm[...], precise))


_INPROJ_W = ('wqat', 'wkvat', 'wgnt', 'wqbt', 'wkbt', 'wvbt', 'wgm')
_INPROJ_OUT = ('qa', 'kc_row', 'vc_row', 'kc', 'vc', 'ks', 'vs', 'kw', 'vw', 'ks_rows', 'vs_blk', 'kw_rows', 'vw_blk',
               'gn', 'qb', 'kb', 'vb', 'kb_rows', 'vb_blk', 'gm')


def _in_proj(x2d, n_batch, rope, prm, tm, precise):
    n = x2d.shape[0]
    t = n // n_batch
    nb = t // tm
    cos_t, sin_t = rope
    row = lambda i: (i, 0)
    const = lambda i: (0, 0)
    pos_t = lambda i: (0, i % nb)
    tr = lambda i: (i // nb, 0, i % nb)
    blk = lambda i: (i // nb, i % nb, 0, 0)
    ws = [prm[k + ('_f32' if precise else '')] for k in _INPROJ_W] + [prm['g_col']]
    in_specs = ([pl.BlockSpec((tm, D_MODEL), row), pl.BlockSpec((HEAD_DIM // 2, tm), pos_t),
                 pl.BlockSpec((HEAD_DIM // 2, tm), pos_t), pl.BlockSpec((1, D_MODEL), const)]
                + [pl.BlockSpec(w.shape, const) for w in ws])
    qdt = F32 if precise else BF16
    rows_out = lambda w, dt: (jax.ShapeDtypeStruct((n, w), dt), pl.BlockSpec((tm, w), row))
    t_out = lambda w, dt=F32: (jax.ShapeDtypeStruct((n_batch, w, t), dt), pl.BlockSpec((1, w, tm), tr))
    r_out = lambda w: (jax.ShapeDtypeStruct((n_batch, nb, tm, w), BF16), pl.BlockSpec((1, 1, tm, w), blk))
    b_out = lambda w: (jax.ShapeDtypeStruct((n_batch, nb, w, tm), BF16), pl.BlockSpec((1, 1, w, tm), blk))
    outs = ([t_out(512, qdt), rows_out(LANES, F32), rows_out(LANES, F32)] + [t_out(LANES)] * 6
            + [r_out(LANES), b_out(LANES), r_out(LANES), b_out(LANES)]
            + [t_out(LANES), t_out(512, qdt), t_out(512), t_out(512), r_out(512), b_out(512),
               rows_out(2 * D_MODEL, F32)])
    res = pl.pallas_call(
        functools.partial(_inproj_kernel, precise), grid=(n // tm,), in_specs=in_specs,
        out_specs=[o[1] for o in outs], out_shape=[o[0] for o in outs], compiler_params=_cparams("parallel"),
        name="in_proj_sample" if precise else "in_proj",
    )(x2d, cos_t, sin_t, prm['norm_attn'], *ws)
    return dict(zip(_INPROJ_OUT, res))


def _compress_hidden(xs, pos_a, pos_b, w1a, w1b, precise):
    lo = _lane_iota((LANES, LANES)) < HALF
    t0, t1 = [], []
    for l in range(0, CMP_STRIDE, 2):
        a, b = xs[l], xs[l + 1]
        t0.append(jnp.where(lo, a, pltpu.roll(b, HALF, 1)))
        t1.append(jnp.where(lo, pltpu.roll(a, HALF, 1), b))
    x = jnp.concatenate([jnp.concatenate(t0, axis=1), jnp.concatenate(t1, axis=1)], axis=0)
    first = _mm(x + pos_a, w1a, precise)
    second = _mm(x + pos_b, w1b, precise)
    n = LANES
    shifted = jnp.concatenate([pltpu.roll(second[:n], n - 1, 0), pltpu.roll(second[n:], n - 1, 0)], axis=0)
    return jax.nn.gelu(first + shifted)


_CW_NAMES = ('pka', 'pkb', 'k1a', 'k1b', 'k2g0', 'k2g1', 'pva', 'pvb', 'v1a', 'v1b', 'v2t', 'gkc', 'bd')


def _compress_prompt_kernel(k_ref, v_ref, *rest):
    cw = dict(zip(_CW_NAMES, rest[:len(_CW_NAMES)]))
    kc_o, vct_o = rest[len(_CW_NAMES):]
    n = k_ref.shape[1] // CMP_STRIDE

    def hidden(ref, p):
        xs = [ref[0, pl.ds(l, n, stride=CMP_STRIDE), :] for l in range(CMP_STRIDE)]
        return _compress_hidden(xs, cw['p%sa' % p][...], cw['p%sb' % p][...], cw['%s1a' % p][...],
                                cw['%s1b' % p][...], False).astype(BF16)

    hk = hidden(k_ref, 'k')
    kc = _dot(hk[:LANES], cw['k2g0'][...]) + _dot(hk[LANES:], cw['k2g1'][...])
    kc_o[0] = _head_norm(kc, cw['gkc'][...], cw['bd'][...])
    hv = hidden(v_ref, 'v')
    w2t = cw['v2t'][...]
    vct_o[0] = jnp.concatenate([_dot_nt(w2t, hv[:LANES]), _dot_nt(w2t, hv[LANES:])], axis=0)


def _compress_prompt(k_c, v_c, prm):
    b, t, _ = k_c.shape
    assert t // CMP_STRIDE == LANES
    ws = [prm[k] for k in _CW_NAMES]
    per_b = lambda i: (i, 0, 0)
    const = lambda i: (0, 0)
    return pl.pallas_call(
        _compress_prompt_kernel, grid=(b,),
        in_specs=[pl.BlockSpec((1, t, LANES), per_b)] * 2 + [pl.BlockSpec(w.shape, const) for w in ws],
        out_specs=[pl.BlockSpec((1, LANES, LANES), per_b)] * 2,
        out_shape=[jax.ShapeDtypeStruct((b, LANES, LANES), F32)] * 2,
        compiler_params=_cparams("parallel"), name="compress_prompt",
    )(k_c, v_c, *ws)


def _nsa_prompt_kernel(q_ref, g_ref, kc_ref, vct_ref, ks_ref, vs_ref, kw_ref, vw_ref, ovt_ref,
                       o_ref, qaug_ref, oc_ref, m_ref, l_ref, acc_ref):
    qt = Q_TILE
    kb = ks_ref.shape[2]
    s0 = pl.program_id(1) * qt
    t_q = s0 + _lane_iota((1, qt))
    head_cols = lambda h: slice(h * qt, (h + 1) * qt)
    zero = jnp.zeros((HEAD_DIM, qt), BF16)

    per_head = lambda x: jnp.concatenate([x] * NSA_HEADS, axis=1)
    blocks = []
    for h in range(NSA_HEADS):
        qh = q_ref[0, h * HEAD_DIM:(h + 1) * HEAD_DIM, :] * ATT_SCALE
        blocks.append(jnp.concatenate([qh, zero] if h < NSA_GROUP else [zero, qh], axis=0))
    qst = jnp.concatenate(blocks, axis=1)

    n_c = kc_ref.shape[1] - 1
    c_end = _row_iota((LANES, qt)) * CMP_STRIDE + (CMP_BLOCK - 1)
    ok = per_head(jnp.where(c_end <= jnp.minimum(t_q, (n_c - 1) * CMP_STRIDE + CMP_BLOCK - 1), 1.0, 0.0)) > 0.5
    s_c = jnp.where(ok, _dot(kc_ref[0].astype(BF16), qst), NEG)
    e = jnp.where(ok, jnp.exp(s_c - jnp.max(s_c, axis=0, keepdims=True)), 0.0)
    p_c = e / jnp.maximum(jnp.sum(e, axis=0, keepdims=True), 1e-30)
    oc_ref[...] = _dot(vct_ref[0].astype(BF16), p_c.astype(BF16))
    g4 = NSA_GROUP * qt
    imp = jnp.concatenate([sum(p_c[:, g * g4 + r * qt: g * g4 + (r + 1) * qt] for r in range(NSA_GROUP))
                           for g in range(NSA_KV_HEADS)], axis=1)
    imp = _dot_split(ovt_ref[...], imp)
    cur = (s0 + (_lane_iota((1, 2 * qt)) & (qt - 1))) // SEL_BLOCK
    sel = _nsa_select(imp, _row_iota(imp.shape), cur, 0)

    bias = jnp.where(sel, 0.0, NEG).astype(BF16)
    bias = jnp.concatenate([bias[:, :qt]] * NSA_GROUP + [bias[:, qt:]] * NSA_GROUP, axis=1)
    qaug_ref[...] = jnp.concatenate([qst, bias], axis=0)
    per_kb = kb // SEL_BLOCK

    def sel_block(c, causal, first):
        key = _row_iota((kb, LANES))
        onehot = jnp.where(_lane_iota((kb, LANES)) - c * per_kb == key // SEL_BLOCK, 1.0, 0.0).astype(BF16)
        kmat = jnp.concatenate([ks_ref[0, c], onehot], axis=1)
        cbias = per_head(jnp.where(c * kb + _row_iota((kb, qt)) <= t_q, 0.0, NEG)) if causal else None
        _attend_block(m_ref, l_ref, acc_ref, 0, kmat, qaug_ref[...], vs_ref[0, c], first, cbias)

    c_last = s0 // kb
    sel_block(0, True, True)

    def body(c, carry):
        sel_block(c, False, False)
        return carry

    lax.fori_loop(1, c_last, body, 0)

    @pl.when(c_last > 0)
    def _():
        sel_block(c_last, True, False)

    w0 = s0 // kb - (WINDOW // kb)
    n_wb = WINDOW // kb + 1
    for step, j in enumerate([n_wb - 1] + list(range(n_wb - 1))):
        blk = w0 + j
        base = jnp.where(blk < 0, FAR, blk * kb)
        dist = t_q - (base + _row_iota((kb, qt)))
        wbias = jnp.where(dist >= 0, jnp.where(dist < WINDOW, 0.0, NEG), NEG)
        idx = jnp.maximum(blk, 0)
        _attend_block(m_ref, l_ref, acc_ref, 1, kw_ref[0, idx], qaug_ref[:LANES, :], vw_ref[0, idx], step == 0,
                      per_head(wbias))

    g = g_ref[0]
    heads = []
    for h in range(NSA_HEADS):
        cs = head_cols(h)
        rows = slice((h // NSA_GROUP) * HEAD_DIM, (h // NSA_GROUP + 1) * HEAD_DIM)
        gate = lambda j: g[j * NSA_HEADS + h: j * NSA_HEADS + h + 1]
        o_s = acc_ref[0, rows, cs] / jnp.maximum(l_ref[0, :, cs], 1e-30)
        o_w = acc_ref[1, rows, cs] / jnp.maximum(l_ref[1, :, cs], 1e-30)
        heads.append(gate(0) * oc_ref[rows, cs] + gate(1) * o_s + gate(2) * o_w)
    o_ref[0] = jnp.concatenate(heads, axis=0).T.astype(BF16)


def _nsa_prompt(q_a, g_nsa, kc, vct, ks_rows, vs_blk, kw_rows, vw_blk, prm):
    b, w, t = q_a.shape
    qt = Q_TILE
    kb = ks_rows.shape[2]
    assert qt == LANES and kb % qt == 0 and WINDOW % kb == 0 and kb % SEL_BLOCK == 0 and t // SEL_BLOCK <= LANES
    cols = NSA_HEADS * qt
    per_q = lambda i, j: (i, 0, j)
    per_b = lambda i, j: (i, 0, 0)
    per_b4 = lambda i, j: (i, 0, 0, 0)
    return pl.pallas_call(
        _nsa_prompt_kernel, grid=(b, t // qt),
        in_specs=[pl.BlockSpec((1, w, qt), per_q), pl.BlockSpec((1, LANES, qt), per_q),
                  pl.BlockSpec((1, LANES, LANES), per_b), pl.BlockSpec((1, LANES, LANES), per_b),
                  pl.BlockSpec((1,) + ks_rows.shape[1:], per_b4), pl.BlockSpec((1,) + vs_blk.shape[1:], per_b4),
                  pl.BlockSpec((1,) + kw_rows.shape[1:], per_b4), pl.BlockSpec((1,) + vw_blk.shape[1:], per_b4),
                  pl.BlockSpec(prm['ovt'].shape, lambda i, j: (0, 0))],
        out_specs=pl.BlockSpec((1, qt, w), lambda i, j: (i, j, 0)),
        out_shape=jax.ShapeDtypeStruct((b, t, w), BF16),
        scratch_shapes=[pltpu.VMEM((2 * LANES, cols), BF16), pltpu.VMEM((LANES, cols), F32),
                        pltpu.VMEM((2, 1, cols), F32), pltpu.VMEM((2, 1, cols), F32),
                        pltpu.VMEM((2, LANES, cols), F32)],
        compiler_params=_cparams("arbitrary", "arbitrary"), name="nsa_prompt",
    )(q_a, g_nsa, kc, vct, ks_rows, vs_blk, kw_rows, vw_blk, prm['ovt'])


def _moba_prompt_kernel(q_ref, kt_ref, kb_ref, vb_ref, o_ref, kmean_ref, qaug_ref, m_ref, l_ref, acc_ref):
    qt = MOBA_BLOCK
    cur = pl.program_id(1)
    n_b = kb_ref.shape[1]
    n_pair = MOBA_HEADS // 2
    cols = 2 * qt

    @pl.when(cur == 0)
    def _():
        lane = _lane_iota((kt_ref.shape[1], LANES))
        km = jnp.zeros(lane.shape, F32)
        for n in range(n_b):
            km = jnp.where(lane == n, jnp.mean(kt_ref[0, :, n * qt:(n + 1) * qt], axis=-1, keepdims=True), km)
        kmean_ref[...] = km.T

    own_bias = jnp.where(_row_iota((qt, qt)) <= _lane_iota((qt, qt)), 0.0, NEG)
    own_bias = jnp.concatenate([own_bias, own_bias], axis=1)
    for j in range(n_pair):
        tl = slice(j * LANES, (j + 1) * LANES)
        qj = q_ref[0, tl, :]
        zero = jnp.zeros((HEAD_DIM, qt), qj.dtype)
        qst = jnp.concatenate([jnp.concatenate([qj[:HEAD_DIM], zero], axis=0),
                               jnp.concatenate([zero, qj[HEAD_DIM:]], axis=0)], axis=1)
        km_hi, km_lo = _split(kmean_ref[:, tl])
        s_g = _dot(km_hi, qst) + _dot(km_lo, qst)
        blk = _row_iota(s_g.shape)
        past = blk < cur
        sel = _select_top(jnp.where(past, s_g, -jnp.inf), MOBA_TOPK, blk.astype(F32), 0)
        bias = jnp.where(jnp.logical_and(sel, past), 0.0, NEG).astype(BF16)
        qs = qst * ATT_SCALE
        qaug_ref[j] = jnp.concatenate([qs, bias], axis=0)
        _attend_block(m_ref, l_ref, acc_ref, j, kb_ref[0, cur, :, tl], qs, vb_ref[0, cur, tl, :], True, own_bias)

    def body(n, carry):
        onehot = jnp.where(_lane_iota((qt, LANES)) == n, 1.0, 0.0).astype(BF16)
        for j in range(n_pair):
            tl = slice(j * LANES, (j + 1) * LANES)
            _attend_block(m_ref, l_ref, acc_ref, j, jnp.concatenate([kb_ref[0, n, :, tl], onehot], axis=1),
                          qaug_ref[j], vb_ref[0, n, tl, :], False)
        return carry

    lax.fori_loop(0, cur, body, 0)
    for j in range(n_pair):
        o = acc_ref[j] / jnp.maximum(l_ref[j], 1e-30)
        o = jnp.concatenate([o[:HEAD_DIM, :qt], o[HEAD_DIM:, qt:]], axis=0)
        o_ref[0, :, j * LANES:(j + 1) * LANES] = o.T.astype(BF16)


def _moba_prompt(q_b, kt, kb_rows, vb_blk):
    b, w, t = q_b.shape
    qt = MOBA_BLOCK
    assert kb_rows.shape[2] == qt and t // qt <= LANES
    per_b = lambda i, j: (i, 0, 0)
    per_b4 = lambda i, j: (i, 0, 0, 0)
    n_pair = MOBA_HEADS // 2
    return pl.pallas_call(
        _moba_prompt_kernel, grid=(b, t // qt),
        in_specs=[pl.BlockSpec((1, w, qt), lambda i, j: (i, 0, j)), pl.BlockSpec((1, w, t), per_b),
                  pl.BlockSpec((1,) + kb_rows.shape[1:], per_b4), pl.BlockSpec((1,) + vb_blk.shape[1:], per_b4)],
        out_specs=pl.BlockSpec((1, qt, w), lambda i, j: (i, j, 0)),
        out_shape=jax.ShapeDtypeStruct((b, t, w), BF16),
        scratch_shapes=[pltpu.VMEM((LANES, w), F32), pltpu.VMEM((n_pair, 2 * LANES, 2 * qt), BF16),
                        pltpu.VMEM((n_pair, 1, 2 * qt), F32), pltpu.VMEM((n_pair, 1, 2 * qt), F32),
                        pltpu.VMEM((n_pair, LANES, 2 * qt), F32)],
        compiler_params=_cparams("arbitrary", "arbitrary"), name="moba_prompt",
    )(q_b, kt, kb_rows, vb_blk)


def _merge_kernel(precise, x_ref, oa_ref, ob_ref, gm_ref, wa_ref, wb_ref, wo_ref, nf_ref, wr_hi, wr_lo, br_ref,
                  h_o, xn_o, lg_o):
    gm = gm_ref[...]
    u = (gm[:, :D_MODEL] * _mm(oa_ref[...], wa_ref[...], precise)
         + gm[:, D_MODEL:] * _mm(ob_ref[...], wb_ref[...], precise))
    h = x_ref[...] + _mm(u, wo_ref[...], precise)
    h_o[...] = h
    xn = h * lax.rsqrt(jnp.mean(h * h, axis=-1, keepdims=True) + EPS) * nf_ref[...]
    xn_o[...] = xn
    lg_o[...] = _dot3(xn, (wr_hi[...], wr_lo[...])) + br_ref[...]


def _merge(x2d, o_a, o_b, g_mrg, prm, tm, precise):
    n = x2d.shape[0]
    row = lambda i: (i, 0)
    const = lambda i: (0, 0)
    sfx = '_f32' if precise else ''
    ws = [prm[k] for k in ('w_up_a' + sfx, 'w_up_b' + sfx, 'w_out' + sfx, 'norm_ffn', 'wr_hi', 'wr_lo', 'b_router')]
    return pl.pallas_call(
        functools.partial(_merge_kernel, precise), grid=(n // tm,),
        in_specs=[pl.BlockSpec((tm, D_MODEL), row), pl.BlockSpec((tm, 512), row), pl.BlockSpec((tm, 512), row),
                  pl.BlockSpec((tm, 2 * D_MODEL), row)] + [pl.BlockSpec(w.shape, const) for w in ws],
        out_specs=[pl.BlockSpec((tm, D_MODEL), row), pl.BlockSpec((tm, D_MODEL), row), pl.BlockSpec((tm, LANES), row)],
        out_shape=[jax.ShapeDtypeStruct((n, D_MODEL), F32), jax.ShapeDtypeStruct((n, D_MODEL), F32),
                   jax.ShapeDtypeStruct((n, LANES), F32)],
        compiler_params=_cparams("parallel"), name="merge_router_sample" if precise else "merge_router",
    )(x2d, o_a, o_b, g_mrg, *ws)


def _moe_kernel(be_ref, bv_ref, x_ref, wgu_ref, bgu_ref, wd_ref, bd_ref, y_ref, wgu16_ref, wd16_ref):
    i = pl.program_id(0)

    @pl.when(jnp.logical_or(i == 0, be_ref[i] != be_ref[jnp.maximum(i - 1, 0)]))
    def _():
        wgu16_ref[...] = wgu_ref[0].astype(BF16)
        wd16_ref[...] = wd_ref[0].astype(BF16)

    @pl.when(bv_ref[i] > 0)
    def _():
        h = _dot(x_ref[...].astype(BF16), wgu16_ref[...]) + bgu_ref[0]
        gate = jnp.minimum(h[:, :D_FF], SWIGLU_LIMIT)
        up = jnp.clip(h[:, D_FF:], -SWIGLU_LIMIT, SWIGLU_LIMIT)
        act = (up + 1.0) * gate * jax.nn.sigmoid(SWIGLU_ALPHA * gate)
        y_ref[...] = _dot(act.astype(BF16), wd16_ref[...]) + bd_ref[0]

    @pl.when(bv_ref[i] == 0)
    def _():
        y_ref[...] = jnp.zeros(y_ref.shape, F32)


def _moe_blocks(block_e, block_valid, x_sorted, w_gu, b_gu, w_down, b_down):
    cap = x_sorted.shape[0]
    tm = MOE_TILE
    row = lambda i, be, bv: (i, 0)
    exp = lambda i, be, bv: (be[i], 0, 0)
    grid_spec = pltpu.PrefetchScalarGridSpec(
        num_scalar_prefetch=2, grid=(cap // tm,),
        in_specs=[pl.BlockSpec((tm, D_MODEL), row),
                  pl.BlockSpec((1, D_MODEL, 2 * D_FF), exp), pl.BlockSpec((1, 1, 2 * D_FF), exp),
                  pl.BlockSpec((1, D_FF, D_MODEL), exp), pl.BlockSpec((1, 1, D_MODEL), exp)],
        out_specs=pl.BlockSpec((tm, D_MODEL), row),
        scratch_shapes=[pltpu.VMEM((D_MODEL, 2 * D_FF), BF16), pltpu.VMEM((D_FF, D_MODEL), BF16)])
    return pl.pallas_call(
        _moe_kernel, grid_spec=grid_spec, out_shape=jax.ShapeDtypeStruct((cap, D_MODEL), F32),
        compiler_params=_cparams("arbitrary"), name="moe_experts",
    )(block_e, block_valid, x_sorted, w_gu, b_gu, w_down, b_down)


def _route(logits):
    n = logits.shape[0]
    top_v, top_e = lax.top_k(logits, TOP_K)
    gates = jax.nn.softmax(top_v, axis=-1)
    onehot = jnp.sum((top_e[:, :, None] == jnp.arange(N_EXPERTS)[None, None, :]).astype(jnp.int32), axis=1)
    csum = jnp.cumsum(onehot, axis=0)
    counts = csum[-1]
    padded = (counts + MOE_TILE - 1) // MOE_TILE * MOE_TILE
    pend = jnp.cumsum(padded)
    rank = jnp.take_along_axis(csum - onehot, top_e, axis=1)
    slot = (pend - padded)[top_e] + rank
    n_blocks = -(-(n * TOP_K) // MOE_TILE) + N_EXPERTS
    cap = n_blocks * MOE_TILE
    starts = jnp.arange(n_blocks, dtype=jnp.int32) * MOE_TILE
    block_e = jnp.minimum(jnp.sum((pend[None, :] <= starts[:, None]).astype(jnp.int32), axis=1), N_EXPERTS - 1)
    block_valid = (starts < pend[-1]).astype(jnp.int32)
    assert cap * n < 2 ** 31
    tok = jnp.broadcast_to(jnp.arange(n, dtype=jnp.int32)[:, None], (n, TOP_K))
    s_tok = lax.sort((slot * n + tok).reshape(-1)) % n
    per_slot = lambda v: jnp.broadcast_to(v[block_e][:, None], (n_blocks, MOE_TILE)).reshape(cap)
    rank_in_e = jnp.arange(cap, dtype=jnp.int32) - per_slot(pend - padded)
    used = rank_in_e < per_slot(counts)
    src = jnp.clip(per_slot(jnp.cumsum(counts) - counts) + rank_in_e, 0, n * TOP_K - 1)
    slot_tok = jnp.where(used, s_tok[src], 0)
    return slot, gates, slot_tok, block_e, block_valid


def _column(ref, b):
    x = ref[...]
    return jnp.sum(jnp.where(_lane_iota(x.shape) == b, x, 0.0), axis=-1, keepdims=True)


def _fold_lanes(x):
    return sum(x[..., i * LANES:(i + 1) * LANES] for i in range(x.shape[-1] // LANES))


def _gqa_scores(q3, kt):
    g = NSA_GROUP
    return jnp.concatenate([jnp.sum(q3[i * g:(i + 1) * g] * kt[i * HEAD_DIM:(i + 1) * HEAD_DIM][None], axis=1)
                            for i in range(NSA_KV_HEADS)], axis=0)


def _gqa_weighted(p, vt):
    g = NSA_GROUP
    return jnp.concatenate([p[i * g:(i + 1) * g][:, None, :] * vt[i * HEAD_DIM:(i + 1) * HEAD_DIM][None]
                            for i in range(NSA_KV_HEADS)], axis=0)


_CWS_NAMES = ('pka', 'pkb', 'k1a_hi', 'k1a_lo', 'k1b_hi', 'k1b_lo', 'k2t',
              'pva', 'pvb', 'v1a_hi', 'v1a_lo', 'v1b_hi', 'v1b_lo', 'v2t', 'gkc_col')


def _nsa_sample_kernel(pt_ref, qt_ref, g_ref, ksn_ref, vsn_ref, kwn_ref, vwn_ref, wk_ref, wv_ref, *rest):
    n_seq = g_ref.shape[0]
    n_pg = (len(rest) - len(_CWS_NAMES) - 6) // (4 * n_seq)
    n_pages = 4 * n_seq * n_pg
    cw = dict(zip(_CWS_NAMES, rest[n_pages:n_pages + len(_CWS_NAMES)]))
    ov_ref, e_ref, o_ref, wk_o, wv_o, rows_ref = rest[n_pages + len(_CWS_NAMES):]
    wb = wk_ref.shape[2]
    last = _lane_iota((LANES, wb)) == wb - 1
    for j in range(n_seq):
        b = pl.program_id(0) * n_seq + j
        wk_o[j] = jnp.where(last, _column(kwn_ref, b), pltpu.roll(wk_ref[j], wb - 1, 1))
        wv_o[j] = jnp.where(last, _column(vwn_ref, b), pltpu.roll(wv_ref[j], wb - 1, 1))
    step = pl.program_id(0)
    fixed = (cw, ov_ref, e_ref, rows_ref, qt_ref, g_ref, ksn_ref, vsn_ref, kwn_ref, vwn_ref, wk_ref, wv_ref)
    outs = [_nsa_sample_one(step * n_seq + j, j, rest[4 * j * n_pg:4 * (j + 1) * n_pg], *fixed) for j in range(n_seq)]

    @pl.when(step == 0)
    def _():
        o_ref[...] = jnp.zeros(o_ref.shape, F32)

    for j, o in enumerate(outs):
        o_ref[...] = jnp.where(_lane_iota(o_ref.shape) == step * n_seq + j, o, o_ref[...])


def _nsa_sample_one(b, j, pages, cw, ov_ref, e_ref, rows_ref, qt_ref, g_ref, ksn_ref, vsn_ref, kwn_ref, vwn_ref,
                    wk_ref, wv_ref):
    n_pg = len(pages) // 4
    ck_pages, cv_pages, sk_pages, sv_pages = (pages[k * n_pg:(k + 1) * n_pg] for k in range(4))
    past = n_pg * PAGE_SIZE

    def compress_t(pages, p):
        for i, pg in enumerate(pages):
            rows_ref[j, i * PAGE_SIZE:(i + 1) * PAGE_SIZE, :] = pg[0].T
        xs = [rows_ref[j, pl.ds(l, past // CMP_STRIDE, stride=CMP_STRIDE), :] for l in range(CMP_STRIDE)]
        h = _compress_hidden(xs, cw['p%sa' % p][...], cw['p%sb' % p][...],
                             (cw['%s1a_hi' % p][...], cw['%s1a_lo' % p][...]),
                             (cw['%s1b_hi' % p][...], cw['%s1b_lo' % p][...]), True)
        w2t = cw['%s2t' % p][...]
        return jnp.concatenate([_dot3(w2t, h[:LANES], nt=True), _dot3(w2t, h[LANES:], nt=True)], axis=0)

    kct = compress_t(ck_pages, 'k').reshape(NSA_KV_HEADS, HEAD_DIM, LANES)
    ss = jnp.sum(kct * kct, axis=1, keepdims=True)
    kct = (kct * lax.rsqrt(ss * (1.0 / HEAD_DIM) + EPS) * cw['gkc_col'][...][None]).reshape(LANES, LANES)
    vct = compress_t(cv_pages, 'v')

    q3 = (_column(qt_ref, b) * ATT_SCALE).reshape(NSA_HEADS, HEAD_DIM, 1)

    n_c = (past + 1 - CMP_BLOCK) // CMP_STRIDE + 1
    s_c = _gqa_scores(q3, kct)
    p_c = _masked_softmax(s_c, _lane_iota(s_c.shape) < n_c)
    o_c = jnp.sum(_gqa_weighted(p_c, vct), axis=-1, keepdims=True)
    row = _row_iota((NSA_HEADS, LANES))
    i0 = jnp.sum(p_c[:NSA_GROUP], axis=0, keepdims=True)
    i1 = jnp.sum(p_c[NSA_GROUP:], axis=0, keepdims=True)
    imp = _split_dot(jnp.where(row < NSA_GROUP, i0, i1), ov_ref[...])
    blk = _lane_iota((1, LANES))
    cur = past // SEL_BLOCK
    forced = (blk == 0) | (blk == cur) | (blk == cur - 1)
    valid = blk <= cur

    def pick(imp_row):
        score = jnp.where(forced, POS_BIG, jnp.where(valid, imp_row, -jnp.inf))
        return jnp.where(jnp.logical_and(_rank_select(score, N_SEL), valid), 1.0, 0.0)

    sel = jnp.where(row < NSA_GROUP, pick(imp[0:1]), pick(imp[NSA_GROUP:NSA_GROUP + 1]))
    picked = _dot(sel.astype(BF16), e_ref[...]) > 0.5

    def attend_with_new(s, valid, kn_ref, vn_ref, weighted):
        kn = _column(kn_ref, b)
        vn = _column(vn_ref, b)
        s = jnp.where(valid, s, NEG)
        s_n = _gqa_scores(q3, kn)
        m = jnp.maximum(jnp.max(s, axis=-1, keepdims=True), s_n)
        p = jnp.where(valid, jnp.exp(s - m), 0.0)
        p_n = jnp.exp(s_n - m)
        num = jnp.sum(weighted(p), axis=-1, keepdims=True) + _gqa_weighted(p_n, vn)
        den = jnp.maximum(jnp.sum(p, axis=-1, keepdims=True) + p_n, 1e-30)
        return num / den[:, :, None]

    s_s = jnp.concatenate([_gqa_scores(q3, pg[0]) for pg in sk_pages], axis=1)
    o_s = attend_with_new(
        s_s, picked, ksn_ref, vsn_ref,
        lambda p: sum(_gqa_weighted(p[:, i * PAGE_SIZE:(i + 1) * PAGE_SIZE], pg[0]) for i, pg in enumerate(sv_pages)))

    wb = wk_ref.shape[2]
    s_w = _gqa_scores(q3, wk_ref[j])
    inside = (past - wb + _lane_iota(s_w.shape)) > (past - WINDOW)
    o_w = attend_with_new(s_w, inside, kwn_ref, vwn_ref, lambda p: _fold_lanes(_gqa_weighted(p, wv_ref[j])))

    g = jnp.broadcast_to(g_ref[j], (NSA_HEADS, LANES))
    lane = _lane_iota(g.shape)
    gate = lambda br: jnp.sum(jnp.where(lane == br * NSA_HEADS + row, g, 0.0), axis=-1, keepdims=True)[:, :, None]
    return (gate(0) * o_c + gate(1) * o_s + gate(2) * o_w).reshape(NSA_HEADS * HEAD_DIM, 1)


def _nsa_sample(page_table, qt, g_nsa, ksn, vsn, kwn, vwn, win_kt, win_vt, cache_ck, cache_cv, cache_skt, cache_svt,
                prm):
    db, n_pg = page_table.shape
    wb = win_kt.shape[2]
    n_seq = NSA_SAMPLE_SEQS
    assert db % n_seq == 0
    per_b = lambda i, pt: (i, 0, 0)
    const2 = lambda i, pt: (0, 0)

    def page_spec(j, p):
        return pl.BlockSpec((1, PAGE_SIZE, LANES), lambda i, pt, j=j, p=p: (pt[i * n_seq + j, p], 0, 0))

    cws = [prm['s_' + k] for k in _CWS_NAMES]
    full = lambda a: pl.BlockSpec(a.shape, const2)
    in_specs = ([full(qt), pl.BlockSpec((n_seq, 1, LANES), per_b), full(ksn), full(vsn), full(kwn), full(vwn),
                 pl.BlockSpec((n_seq, LANES, wb), per_b), pl.BlockSpec((n_seq, LANES, wb), per_b)]
                + [page_spec(j, p) for j in range(n_seq) for _ in range(4) for p in range(n_pg)]
                + [full(w) for w in cws] + [full(prm['ov']), full(prm['e_key'])])
    grid_spec = pltpu.PrefetchScalarGridSpec(num_scalar_prefetch=1, grid=(db // n_seq,), in_specs=in_specs,
                                             out_specs=[pl.BlockSpec((NSA_HEADS * HEAD_DIM, db), const2),
                                                        pl.BlockSpec((n_seq, LANES, wb), per_b),
                                                        pl.BlockSpec((n_seq, LANES, wb), per_b)],
                                             scratch_shapes=[pltpu.VMEM((n_seq, n_pg * PAGE_SIZE, LANES), F32)])
    pages = ([cache_ck] * n_pg + [cache_cv] * n_pg + [cache_skt] * n_pg + [cache_svt] * n_pg) * n_seq
    return pl.pallas_call(
        _nsa_sample_kernel, grid_spec=grid_spec,
        out_shape=[jax.ShapeDtypeStruct((NSA_HEADS * HEAD_DIM, db), F32),
                   jax.ShapeDtypeStruct((db, LANES, wb), F32), jax.ShapeDtypeStruct((db, LANES, wb), F32)],
        compiler_params=_cparams("arbitrary"), name="nsa_sample",
    )(page_table, qt, g_nsa, ksn, vsn, kwn, vwn, win_kt, win_vt, *pages, *cws, prm['ov'], prm['e_key'])


def _moba_sample_kernel(pt_ref, qt_ref, kn_ref, vn_ref, *rest):
    n_pg = (len(rest) - 1) // 2
    k_pages = rest[:n_pg]
    v_pages = rest[n_pg:2 * n_pg]
    o_ref = rest[2 * n_pg]
    b = pl.program_id(0)
    pages_per_blk = MOBA_BLOCK // PAGE_SIZE
    n_b = n_pg // pages_per_blk
    heads3 = lambda x: x.reshape(MOBA_HEADS, HEAD_DIM, x.shape[-1])

    q3 = heads3(_column(qt_ref, b) * ATT_SCALE)
    kn3 = heads3(_column(kn_ref, b))
    vn3 = heads3(_column(vn_ref, b))
    s_pages = [jnp.sum(heads3(pg[0]) * q3, axis=1) for pg in k_pages]
    s_new = jnp.sum(q3 * kn3, axis=1)

    lane = _lane_iota((MOBA_HEADS, LANES))
    s_g = jnp.full((MOBA_HEADS, LANES), -jnp.inf, F32)
    for n in range(n_b):
        tot = sum(s_pages[n * pages_per_blk + i] for i in range(pages_per_blk))
        s_g = jnp.where(lane == n, jnp.sum(tot, axis=-1, keepdims=True), s_g)
    sel = _select_top(s_g, min(MOBA_TOPK, n_b), lane.astype(F32), -1)
    sel_blk = [jnp.sum(jnp.where(jnp.logical_and(sel, lane == n), 1.0, 0.0), axis=-1, keepdims=True) > 0.5
               for n in range(n_b)]

    m = s_new
    for i in range(n_pg):
        m = jnp.maximum(m, jnp.max(jnp.where(sel_blk[i // pages_per_blk], s_pages[i], NEG), axis=-1, keepdims=True))
    p_new = jnp.exp(s_new - m)
    l = p_new
    acc = jnp.zeros((MOBA_HEADS, HEAD_DIM, PAGE_SIZE), F32)
    for i in range(n_pg):
        p = jnp.where(sel_blk[i // pages_per_blk], jnp.exp(s_pages[i] - m), 0.0)
        l = l + jnp.sum(p, axis=-1, keepdims=True)
        acc = acc + heads3(v_pages[i][0]) * p[:, None, :]
    o = (jnp.sum(acc, axis=-1, keepdims=True) + p_new[:, :, None] * vn3) / jnp.maximum(l, 1e-30)[:, :, None]

    @pl.when(b == 0)
    def _():
        o_ref[...] = jnp.zeros(o_ref.shape, F32)

    o_ref[...] = jnp.where(_lane_iota(o_ref.shape) == b, o.reshape(MOBA_HEADS * HEAD_DIM, 1), o_ref[...])


def _moba_sample(page_table, qt, knt, vnt, cache_kt, cache_vt):
    db, n_pg = page_table.shape
    w = qt.shape[0]
    assert (n_pg * PAGE_SIZE) % MOBA_BLOCK == 0
    const2 = lambda i, pt: (0, 0)

    def page_spec(p):
        return pl.BlockSpec((1, w, PAGE_SIZE), lambda i, pt, p=p: (pt[i, p], 0, 0))

    in_specs = [pl.BlockSpec((w, db), const2)] * 3 + [page_spec(p) for p in range(n_pg)] * 2
    grid_spec = pltpu.PrefetchScalarGridSpec(num_scalar_prefetch=1, grid=(db,), in_specs=in_specs,
                                             out_specs=pl.BlockSpec((w, db), const2))
    return pl.pallas_call(
        _moba_sample_kernel, grid_spec=grid_spec, out_shape=jax.ShapeDtypeStruct((w, db), F32),
        compiler_params=_cparams("arbitrary"), name="moba_sample",
    )(page_table, qt, knt, vnt, *([cache_kt] * n_pg), *([cache_vt] * n_pg))


def _split_weight(w):
    c = w * (2.0 ** 16 + 1.0)
    hi = c - (c - w)
    return hi.astype(BF16), (w - hi).astype(BF16)


def _prepare(norm_attn, w_in, g_qa, g_ka, g_kc, cmp_pos_k, cmp_w1_k, cmp_w2_k, cmp_pos_v, cmp_w1_v, cmp_w2_v,
             g_qb, g_kb, w_up_a, w_up_b, w_out, norm_ffn, w_router, b_router):
    qa = NSA_HEADS * HEAD_DIM
    kva = NSA_KV_HEADS * HEAD_DIM
    qb = MOBA_HEADS * HEAD_DIM
    widths = [qa, kva, kva, kva, kva, kva, kva, 3 * NSA_HEADS, qb, qb, qb, 2 * D_MODEL]
    cuts = np.cumsum([0] + widths)
    part = lambda i: w_in[:, cuts[i]:cuts[i + 1]]
    gcols = np.array([h * 3 + j for j in range(3) for h in range(NSA_HEADS)])
    wgnt = jnp.zeros((LANES, D_MODEL), F32).at[:3 * NSA_HEADS].set(part(7)[:, gcols].T)
    two = lambda g: jnp.concatenate([g, g]).reshape(1, LANES)
    lane = np.arange(LANES)
    c_idx = np.arange(LANES)[:, None] * CMP_STRIDE
    s_idx = np.arange(LANES)[None, :] * SEL_BLOCK
    ov = ((c_idx < s_idx + SEL_BLOCK) & (c_idx + CMP_BLOCK > s_idx) & (np.arange(LANES)[:, None] < LANES - 1))
    half_w = CMP_STRIDE * HEAD_DIM

    def w2wide(w2):
        z = jnp.zeros_like(w2)
        return jnp.concatenate([w2, z], axis=1).astype(BF16), jnp.concatenate([z, w2], axis=1).astype(BF16)

    k2g0, k2g1 = w2wide(cmp_w2_k)
    wr_hi, wr_lo = _split_weight(jnp.zeros((D_MODEL, LANES), F32).at[:, :N_EXPERTS].set(w_router))
    prm = {
        'norm_attn': norm_attn.reshape(1, D_MODEL),
        'wqat_f32': part(0).T, 'wkvat_f32': w_in[:, cuts[1]:cuts[7]].T, 'wgnt_f32': wgnt,
        'wqbt_f32': part(8).T, 'wkbt_f32': part(9).T, 'wvbt_f32': part(10).T, 'wgm_f32': part(11),
        'g_col': jnp.stack([g_qa, g_qb, g_ka[0], g_ka[1], g_ka[2], g_kb], axis=1),
        'gkc': two(g_kc),
        'bd': jnp.asarray((lane[:, None] // HEAD_DIM) == (lane[None, :] // HEAD_DIM), BF16),
        'pka': cmp_pos_k[:CMP_STRIDE].reshape(1, half_w), 'pkb': cmp_pos_k[CMP_STRIDE:].reshape(1, half_w),
        'pva': cmp_pos_v[:CMP_STRIDE].reshape(1, half_w), 'pvb': cmp_pos_v[CMP_STRIDE:].reshape(1, half_w),
        'k1a': cmp_w1_k[:half_w].astype(BF16), 'k1b': cmp_w1_k[half_w:].astype(BF16),
        'v1a': cmp_w1_v[:half_w].astype(BF16), 'v1b': cmp_w1_v[half_w:].astype(BF16),
        'k2g0': k2g0, 'k2g1': k2g1, 'v2t': cmp_w2_v.T.astype(BF16),
        'ov': jnp.asarray(ov, BF16), 'ovt': jnp.asarray(ov.T, BF16),
        'w_up_a_f32': w_up_a, 'w_up_b_f32': w_up_b, 'w_out_f32': w_out,
        'norm_ffn': norm_ffn.reshape(1, D_MODEL),
        'wr_hi': wr_hi, 'wr_lo': wr_lo,
        'b_router': jnp.zeros((1, LANES), F32).at[0, :N_EXPERTS].set(b_router),
    }
    for k in _INPROJ_W + ('w_up_a', 'w_up_b', 'w_out'):
        prm[k] = prm[k + '_f32'].astype(BF16)
    for p, w1, w2 in (('k', cmp_w1_k, cmp_w2_k), ('v', cmp_w1_v, cmp_w2_v)):
        for half, w in (('a', w1[:half_w]), ('b', w1[half_w:])):
            prm['s_%s1%s_hi' % (p, half)], prm['s_%s1%s_lo' % (p, half)] = _split_weight(w)
        prm['s_%s2t' % p] = w2.T
        prm['s_p%sa' % p] = prm['p%sa' % p]
        prm['s_p%sb' % p] = prm['p%sb' % p]
    prm['s_gkc_col'] = g_kc.reshape(HEAD_DIM, 1)
    return prm


def _rope_tables(pos):
    half = HEAD_DIM // 2
    inv = ROPE_THETA ** (-np.arange(half, dtype=np.float64) / half)
    ang = inv[:, None] * np.asarray(pos, np.float64)[None, :]
    return jnp.asarray(np.cos(ang), F32), jnp.asarray(np.sin(ang), F32)


def _expand_matrix(n_keys, block):
    e = (np.arange(n_keys)[None, :] // block) == np.arange(LANES)[:, None]
    return jnp.asarray(e, BF16)


def kernel(x_prompt, x_sample, cache_nsa_cmp_k, cache_nsa_cmp_v, cache_nsa_sel_k, cache_nsa_sel_v, cache_nsa_win_k, cache_nsa_win_v, cache_moba_k, cache_moba_v, page_table, norm_attn, w_in, g_qa, g_ka, g_kc, cmp_pos_k, cmp_w1_k, cmp_w2_k, cmp_pos_v, cmp_w1_v, cmp_w2_v, g_qb, g_kb, w_up_a, w_up_b, w_out, norm_ffn, w_router, b_router, w_gu, b_gu, w_down, b_down):
    B, T, _ = x_prompt.shape
    DB, DS, _ = x_sample.shape
    assert DS == 1
    n_pg = page_table.shape[1]
    past = n_pg * PAGE_SIZE
    wb = cache_nsa_win_k.shape[1]

    prm = _prepare(norm_attn, w_in, g_qa, g_ka, g_kc, cmp_pos_k, cmp_w1_k, cmp_w2_k, cmp_pos_v, cmp_w1_v, cmp_w2_v,
                   g_qb, g_kb, w_up_a, w_up_b, w_out, norm_ffn, w_router, b_router)
    prm['e_key'] = _expand_matrix(past, SEL_BLOCK)

    P = _in_proj(x_prompt.reshape(B * T, D_MODEL), B, _rope_tables(np.arange(T)), prm, KEY_BLOCK, False)
    r3 = lambda a: a.reshape(B, T, a.shape[-1])
    kc_cmp, vct_cmp = _compress_prompt(r3(P['kc_row']), r3(P['vc_row']), prm)
    oa_p = _nsa_prompt(P['qa'], P['gn'], kc_cmp, vct_cmp, P['ks_rows'], P['vs_blk'], P['kw_rows'], P['vw_blk'], prm)
    ob_p = _moba_prompt(P['qb'], P['kb'], P['kb_rows'], P['vb_blk'])
    h_p, xn_p, lg_p = _merge(x_prompt.reshape(B * T, D_MODEL), oa_p.reshape(B * T, 512), ob_p.reshape(B * T, 512),
                             P['gm'], prm, MERGE_TILE, False)

    S = _in_proj(x_sample.reshape(DB, D_MODEL), 1, _rope_tables(np.full((DB,), past)), prm, DB, True)
    cols2 = lambda c: jnp.transpose(c, (0, 2, 3, 1)).reshape(c.shape[0], c.shape[2] * c.shape[3], c.shape[1])
    assert wb == WINDOW
    oa_t, kw_all, vw_all = _nsa_sample(page_table, S['qa'][0], S['gn'][0].T.reshape(DB, 1, LANES),
                       S['ks'][0], S['vs'][0], S['kw'][0], S['vw'][0],
                       cols2(cache_nsa_win_k), cols2(cache_nsa_win_v),
                       cols2(cache_nsa_cmp_k), cols2(cache_nsa_cmp_v), cols2(cache_nsa_sel_k), cols2(cache_nsa_sel_v),
                       prm)
    ob_t = _moba_sample(page_table, S['qb'][0], S['kb'][0], S['vb'][0], cols2(cache_moba_k), cols2(cache_moba_v))
    h_s, xn_s, lg_s = _merge(x_sample.reshape(DB, D_MODEL), oa_t.T, ob_t.T, S['gm'], prm, DB, True)

    xn_all = jnp.concatenate([xn_p, xn_s], axis=0)
    logits = jnp.concatenate([lg_p, lg_s], axis=0)[:, :N_EXPERTS]
    slot, gates, slot_tok, block_e, block_valid = _route(logits)
    y_slot = _moe_blocks(block_e, block_valid, xn_all[slot_tok],
                         w_gu, b_gu.reshape(N_EXPERTS, 1, 2 * D_FF), w_down, b_down.reshape(N_EXPERTS, 1, D_MODEL))
    n_tok = B * T + DB
    picked = y_slot[slot.T.reshape(-1)].reshape(TOP_K, n_tok, D_MODEL)
    mix = lambda lo, hi: sum(gates[lo:hi, k:k + 1] * picked[k, lo:hi] for k in range(TOP_K))
    y_prompt = (h_p + mix(0, B * T)).reshape(B, T, D_MODEL)
    y_sample = (h_s + mix(B * T, n_tok)).reshape(DB, DS, D_MODEL)

    def kv4(a):
        n, w, t = a.shape
        return jnp.transpose(a.reshape(n, w // HEAD_DIM, HEAD_DIM, t), (0, 3, 1, 2))

    new4 = lambda a: kv4(a).reshape(DB, 1, a.shape[1] // HEAD_DIM, HEAD_DIM)
    wp = min(WINDOW, T)
    return (y_prompt, y_sample,
            kv4(P['kc']), kv4(P['vc']), kv4(P['ks']), kv4(P['vs']),
            kv4(P['kw'][:, :, T - wp:]), kv4(P['vw'][:, :, T - wp:]), kv4(P['kb']), kv4(P['vb']),
            new4(S['kc']), new4(S['vc']), new4(S['ks']), new4(S['vs']), kv4(kw_all), kv4(vw_all),
            new4(S['kb']), new4(S['vb']))
```
